```python
import numpy as np
import jax
import jax.numpy as jnp
from jax import lax

D_MODEL = 1024
BATCH = 8
SEQ = 8192
DEPTH = 2

CTX_LEN = 256
GRID_W = 64
N_MOD = 9
D_FF = 2816
NA_HEADS = 6
NA_HEAD_DIM = 64
NA_WIDTH = NA_HEADS * NA_HEAD_DIM
NA_WIN_R = 8
NA_WIN_C = 16
SG_GROUPS = 4
SG_GROUP_DIM = 64
SG_WIDTH = SG_GROUPS * SG_GROUP_DIM
SG_CHUNK = 128
RET_HEADS = 6
RET_HEAD_DIM = 64
RET_WIDTH = RET_HEADS * RET_HEAD_DIM
RET_CHUNK = 128
D_MIX = NA_WIDTH + SG_WIDTH + RET_WIDTH
IN_COLS = 3 * NA_WIDTH + 2 * SG_WIDTH + 5 * RET_WIDTH
ROPE_BASE = 10000.0
ROPE_FREQS = RET_HEAD_DIM // 4
RMS_EPS = 1e-6
LN_EPS = 1e-5

kernel_name = 'hybrid_na_gmlp_retention_dit_block'


def rms_norm(h, w):
    hf = h.astype(jnp.float32)
    hf = hf * lax.rsqrt(jnp.mean(hf * hf, axis=-1, keepdims=True) + RMS_EPS)
    return hf.astype(h.dtype) * w


def ada_rms(h, w, shift, scale):
    return rms_norm(h, w) * (1.0 + scale) + shift


def swiglu(h, w1, w2):
    g, u = jnp.split(h @ w1, 2, axis=-1)
    return (jax.nn.silu(g) * u) @ w2


def ffn_branch(h, mod, i, nw, w1, w2):
    return h + 0.5 * mod[:, i + 2] * swiglu(ada_rms(h, nw, mod[:, i], mod[:, i + 1]), w1, w2)


def split_heads(t, n):
    return t.reshape(*t.shape[:-1], n, t.shape[-1] // n)


def qk_norm(t, w):
    tf = t.astype(jnp.float32)
    tf = tf * lax.rsqrt(jnp.mean(tf * tf, axis=-1, keepdims=True) + RMS_EPS)
    return tf.astype(t.dtype) * w


def rope_rotate(t, ang):
    cos = jnp.cos(ang)[None, :, None, :].astype(t.dtype)
    sin = jnp.sin(ang)[None, :, None, :].astype(t.dtype)
    t1, t2 = jnp.split(t, 2, axis=-1)
    return jnp.concatenate([t1 * cos - t2 * sin, t1 * sin + t2 * cos], axis=-1)


def axial_rope(t, ang_r, ang_c):
    tr, tc = jnp.split(t, 2, axis=-1)
    return jnp.concatenate([rope_rotate(tr, ang_r), rope_rotate(tc, ang_c)], axis=-1)


def neighbourhood_attention(q, k, v, kc, vc, rpb, rows):
    b, s, h, d = q.shape
    kr = min(NA_WIN_R, rows)
    scale = NA_HEAD_DIM ** -0.5
    qg = q.reshape(b, rows, GRID_W, h, d)
    kg = k.reshape(b, rows, GRID_W, h, d)
    vg = v.reshape(b, rows, GRID_W, h, d)
    col = jnp.arange(GRID_W)
    col_start = jnp.clip(col - NA_WIN_C // 2, 0, GRID_W - NA_WIN_C)
    col_idx = col_start[:, None] + jnp.arange(NA_WIN_C)[None, :]
    col_off = col_idx - col[:, None] + (NA_WIN_C - 1)
    n_loc = kr * NA_WIN_C

    def one_row(r):
        r_start = jnp.clip(r - kr // 2, 0, rows - kr)
        q_r = lax.dynamic_index_in_dim(qg, r, axis=1, keepdims=False)
        k_rows = lax.dynamic_slice_in_dim(kg, r_start, kr, axis=1)
        v_rows = lax.dynamic_slice_in_dim(vg, r_start, kr, axis=1)
        k_win = jnp.take(k_rows, col_idx, axis=2)
        v_win = jnp.take(v_rows, col_idx, axis=2)
        row_off = r_start + jnp.arange(kr) - r + (NA_WIN_R - 1)
        bias = rpb[:, row_off[:, None, None], col_off[None, :, :]]
        bias = jnp.transpose(bias, (0, 2, 1, 3))
        s_loc = jnp.einsum('bqhd,biqjhd->bhqij', q_r, k_win) * scale + bias[None]
        s_ctx = jnp.einsum('bqhd,bchd->bhqc', q_r, kc) * scale
        logits = jnp.concatenate([s_loc.reshape(b, h, GRID_W, n_loc), s_ctx], axis=-1)
        p = jax.nn.softmax(logits.astype(jnp.float32), axis=-1).astype(v.dtype)
        p_loc = p[..., :n_loc].reshape(b, h, GRID_W, kr, NA_WIN_C)
        p_ctx = p[..., n_loc:]
        return (jnp.einsum('bhqij,biqjhd->bqhd', p_loc, v_win)
                + jnp.einsum('bhqc,bchd->bqhd', p_ctx, vc))

    out = lax.map(one_row, jnp.arange(rows))
    return jnp.transpose(out, (1, 0, 2, 3, 4)).reshape(b, s, h, d)


def context_attention(q, k, v):
    s = jnp.einsum('bqhd,bkhd->bhqk', q, k) * (NA_HEAD_DIM ** -0.5)
    p = jax.nn.softmax(s.astype(jnp.float32), axis=-1).astype(v.dtype)
    return jnp.einsum('bhqk,bkhd->bqhd', p, v)


def spatial_gating(u, v, w_s, b_s, ln_w, ln_b):
    b, l, _ = v.shape
    u = jax.nn.gelu(u)
    v = jax.nn.gelu(v)
    vf = v.reshape(b, l, SG_GROUPS, SG_GROUP_DIM).astype(jnp.float32)
    mu = jnp.mean(vf, axis=-1, keepdims=True)
    var = jnp.mean(jnp.square(vf - mu), axis=-1, keepdims=True)
    vn = ((vf - mu) * lax.rsqrt(var + LN_EPS)).reshape(b, l, SG_WIDTH).astype(v.dtype) * ln_w + ln_b
    vch = vn.reshape(b, l // SG_CHUNK, SG_CHUNK, SG_GROUPS, SG_GROUP_DIM)
    mixed = jnp.einsum('gpq,bnqgc->bnpgc', w_s, vch) + b_s.T[:, :, None]
    return u * mixed.reshape(b, l, SG_WIDTH)


def retention_final_state(k, v, log_g):
    l = k.shape[1]
    w = jnp.exp((l - 1 - jnp.arange(l, dtype=jnp.float32))[None, :] * log_g[:, None])
    return jnp.einsum('blhd,hl,blhe->bhde', k.astype(jnp.float32), w, v.astype(jnp.float32))


def retention_chunkwise(q, k, v, log_g, state0):
    q, k, v = (a.astype(jnp.float32) for a in (q, k, v))
    b, l, h, d = q.shape
    n = l // RET_CHUNK
    qc = q.reshape(b, n, RET_CHUNK, h, d)
    kc = k.reshape(b, n, RET_CHUNK, h, d)
    vc = v.reshape(b, n, RET_CHUNK, h, d)
    pos = jnp.arange(RET_CHUNK, dtype=jnp.float32)
    diff = pos[:, None] - pos[None, :]
    intra_decay = jnp.where(diff >= 0, jnp.exp(jnp.maximum(diff, 0.0)[None] * log_g[:, None, None]), 0.0)
    scores = jnp.einsum('bnihd,bnjhd->bnhij', qc, kc) * intra_decay
    o_intra = jnp.einsum('bnhij,bnjhe->bnihe', scores, vc)
    to_end = jnp.exp((RET_CHUNK - 1 - pos)[None, :] * log_g[:, None])
    chunk_kv = jnp.einsum('bnjhd,hj,bnjhe->nbhde', kc, to_end, vc)
    chunk_decay = jnp.exp(RET_CHUNK * log_g)[None, :, None, None]

    def step(state, kv):
        return chunk_decay * state + kv, state

    _, prev = lax.scan(step, state0, chunk_kv)
    from_start = jnp.exp((pos + 1.0)[None, :] * log_g[:, None])
    o_cross = jnp.einsum('bnihd,nbhde,hi->bnihe', qc, prev, from_start)
    return (o_intra + o_cross).reshape(b, l, h, d)


def head_group_norm(o):
    b, l, h, d = o.shape
    mu = jnp.mean(o, axis=-1, keepdims=True)
    var = jnp.mean(jnp.square(o - mu), axis=-1, keepdims=True)
    return ((o - mu) * lax.rsqrt(var + LN_EPS)).reshape(b, l, h * d)


def retention_merge(o_f, o_b, g_f, g_b, gn_w):
    y = (head_group_norm(o_f) * jax.nn.silu(g_f.astype(jnp.float32))
         + head_group_norm(o_b) * jax.nn.silu(g_b.astype(jnp.float32)))
    return (y * gn_w.astype(jnp.float32)).astype(g_f.dtype)


def merge_groups(na, sg, ret, w_out):
    b, l = sg.shape[:2]
    return jnp.concatenate([na.reshape(b, l, NA_WIDTH), sg, ret], axis=-1) @ w_out


def hybrid_mixer(hx, hc, rows, ang_r, ang_c, w_in, w_out, na_qn, na_kn, na_rpb,
                 sg_w, sg_b, sg_ln_w, sg_ln_b, ret_logit, ret_gn_w, with_ctx_out):
    sizes = (NA_WIDTH,) * 3 + (SG_WIDTH,) * 2 + (RET_WIDTH,) * 5
    cuts = tuple(int(s) for s in np.cumsum(sizes)[:-1])
    px = jnp.split(hx @ w_in, cuts, axis=-1)
    pc = jnp.split(hc @ w_in, cuts, axis=-1)
    log_g = jax.nn.log_sigmoid(ret_logit.astype(jnp.float32))

    kc_na = qk_norm(split_heads(pc[1], NA_HEADS), na_kn)
    vc_na = split_heads(pc[2], NA_HEADS)
    na_x = neighbourhood_attention(qk_norm(split_heads(px[0], NA_HEADS), na_qn),
                                   qk_norm(split_heads(px[1], NA_HEADS), na_kn),
                                   split_heads(px[2], NA_HEADS), kc_na, vc_na, na_rpb, rows)
    sg_x = spatial_gating(px[3], px[4], sg_w, sg_b, sg_ln_w, sg_ln_b)
    k_scale = RET_HEAD_DIM ** -0.5
    rq_x = axial_rope(split_heads(px[5], RET_HEADS), ang_r, ang_c)
    rk_x = axial_rope(split_heads(px[6], RET_HEADS), ang_r, ang_c) * k_scale
    rv_x = split_heads(px[7], RET_HEADS)
    rk_c = split_heads(pc[6], RET_HEADS) * k_scale
    rv_c = split_heads(pc[7], RET_HEADS)
    flip = lambda t: jnp.flip(t, axis=1)
    state_f = retention_final_state(rk_c, rv_c, log_g[0])
    state_b = retention_final_state(flip(rk_c), flip(rv_c), log_g[1])
    o_f = retention_chunkwise(rq_x, rk_x, rv_x, log_g[0], state_f)
    o_b = flip(retention_chunkwise(flip(rq_x), flip(rk_x), flip(rv_x), log_g[1], state_b))
    ret_x = retention_merge(o_f, o_b, px[8], px[9], ret_gn_w)
    yx = merge_groups(na_x, sg_x, ret_x, w_out)
    if not with_ctx_out:
        return yx, None

    na_c = context_attention(qk_norm(split_heads(pc[0], NA_HEADS), na_qn), kc_na, vc_na)
    sg_c = spatial_gating(pc[3], pc[4], sg_w, sg_b, sg_ln_w, sg_ln_b)
    rq_c = split_heads(pc[5], RET_HEADS)
    zero = jnp.zeros_like(state_f)
    oc_f = retention_chunkwise(rq_c, rk_c, rv_c, log_g[0], zero)
    oc_b = flip(retention_chunkwise(flip(rq_c), flip(rk_c), flip(rv_c), log_g[1], zero))
    ret_c = retention_merge(oc_f, oc_b, pc[8], pc[9], ret_gn_w)
    yc = merge_groups(na_c, sg_c, ret_c, w_out)
    return yx, yc


def setup_inputs(seed: int = 0) -> dict:
    key = jax.random.key(seed)
    ks = jax.random.split(key, 20)
    f32 = jnp.float32

    def nrm(k, shape, scale):
        return jax.random.normal(k, shape, f32) * scale

    base_logit = jnp.log(2.0 ** (5.0 + jnp.arange(RET_HEADS, dtype=f32)) - 1.0)
    return {
        'x': nrm(ks[0], (BATCH, SEQ, D_MODEL), 1.0),
        'c': nrm(ks[1], (BATCH, D_MODEL), 1.0),
        'ctx': nrm(ks[2], (BATCH, CTX_LEN, D_MODEL), 1.0),
        'c_ctx': nrm(ks[3], (D_MODEL,), 1.0),
        'ada_w': nrm(ks[4], (DEPTH, D_MODEL, N_MOD * D_MODEL), D_MODEL ** -0.5),
        'ada_b': nrm(ks[5], (DEPTH, N_MOD * D_MODEL), 0.02),
        'norm_w': 1.0 + nrm(ks[6], (DEPTH, 3, D_MODEL), 0.02),
        'ffn_w1': nrm(ks[7], (DEPTH, 2, D_MODEL, 2 * D_FF), D_MODEL ** -0.5),
        'ffn_w2': nrm(ks[8], (DEPTH, 2, D_FF, D_MODEL), D_FF ** -0.5),
        'mix_w_in': nrm(ks[9], (DEPTH, D_MODEL, IN_COLS), D_MODEL ** -0.5),
        'mix_w_out': nrm(ks[10], (DEPTH, D_MIX, D_MODEL), D_MIX ** -0.5),
        'na_q_norm': 1.0 + nrm(ks[11], (DEPTH, NA_HEAD_DIM), 0.02),
        'na_k_norm': 1.0 + nrm(ks[12], (DEPTH, NA_HEAD_DIM), 0.02),
        'na_rpb': nrm(ks[13], (DEPTH, NA_HEADS, 2 * NA_WIN_R - 1, 2 * NA_WIN_C - 1), 0.02),
        'sg_w': nrm(ks[14], (DEPTH, SG_GROUPS, SG_CHUNK, SG_CHUNK), SG_CHUNK ** -0.5),
        'sg_b': 1.0 + nrm(ks[15], (DEPTH, SG_GROUPS, SG_CHUNK), 0.02),
        'sg_ln_w': 1.0 + nrm(ks[16], (DEPTH, SG_WIDTH), 0.02),
        'sg_ln_b': nrm(ks[17], (DEPTH, SG_WIDTH), 0.02),
        'ret_decay_logit': base_logit + nrm(ks[18], (DEPTH, 2, RET_HEADS), 0.1),
        'ret_gn_w': 1.0 + nrm(ks[19], (DEPTH, RET_WIDTH), 0.02),
    }


def reference(x, c, ctx, c_ctx, ada_w, ada_b, norm_w, ffn_w1, ffn_w2, mix_w_in, mix_w_out,
              na_q_norm, na_k_norm, na_rpb, sg_w, sg_b, sg_ln_w, sg_ln_b, ret_decay_logit, ret_gn_w):
    b, n_lat, d = x.shape
    rows = n_lat // GRID_W
    t = jnp.arange(n_lat)
    inv = ROPE_BASE ** (-jnp.arange(ROPE_FREQS, dtype=jnp.float32) / ROPE_FREQS)
    ang_r = (t // GRID_W).astype(jnp.float32)[:, None] * inv[None, :]
    ang_c = (t % GRID_W).astype(jnp.float32)[:, None] * inv[None, :]
    for layer in range(DEPTH):
        last = layer == DEPTH - 1
        mod_x = (jax.nn.silu(c) @ ada_w[layer] + ada_b[layer]).reshape(b, N_MOD, 1, d)
        mod_c = (jax.nn.silu(c_ctx) @ ada_w[layer] + ada_b[layer]).reshape(1, N_MOD, 1, d)
        x = ffn_branch(x, mod_x, 0, norm_w[layer, 0], ffn_w1[layer, 0], ffn_w2[layer, 0])
        ctx = ffn_branch(ctx, mod_c, 0, norm_w[layer, 0], ffn_w1[layer, 0], ffn_w2[layer, 0])
        hx = ada_rms(x, norm_w[layer, 1], mod_x[:, 3], mod_x[:, 4])
        hc = ada_rms(ctx, norm_w[layer, 1], mod_c[:, 3], mod_c[:, 4])
        yx, yc = hybrid_mixer(hx, hc, rows, ang_r, ang_c, mix_w_in[layer], mix_w_out[layer],
                              na_q_norm[layer], na_k_norm[layer], na_rpb[layer],
                              sg_w[layer], sg_b[layer], sg_ln_w[layer], sg_ln_b[layer],
                              ret_decay_logit[layer], ret_gn_w[layer], not last)
        x = x + mod_x[:, 5] * yx
        x = ffn_branch(x, mod_x, 6, norm_w[layer, 2], ffn_w1[layer, 1], ffn_w2[layer, 1])
        if not last:
            ctx = ctx + mod_c[:, 5] * yc
            ctx = ffn_branch(ctx, mod_c, 6, norm_w[layer, 2], ffn_w1[layer, 1], ffn_w2[layer, 1])
    return x
```

```python
import functools

import numpy as np
import jax
import jax.numpy as jnp
from jax import lax
from jax.experimental import pallas as pl
from jax.experimental.pallas import tpu as pltpu

D_MODEL = 1024
GRID_W = 64
N_MOD = 9
D_FF = 2816
NA_HEADS = 6
HEAD_DIM = 64
NA_WIDTH = NA_HEADS * HEAD_DIM
NA_WIN_R = 8
NA_WIN_C = 16
SG_GROUPS = 4
SG_WIDTH = SG_GROUPS * HEAD_DIM
CHUNK = 128
RET_HEADS = 6
RET_WIDTH = RET_HEADS * HEAD_DIM
IN_COLS = 3 * NA_WIDTH + 2 * SG_WIDTH + 5 * RET_WIDTH
ROPE_BASE = 10000.0
ROPE_FREQS = HEAD_DIM // 4
RMS_EPS = 1e-6
LN_EPS = 1e-5
NEG_INF = -1e30

OFF_NQ, OFF_NK, OFF_NV = 0, NA_WIDTH, 2 * NA_WIDTH
OFF_SU = 3 * NA_WIDTH
OFF_SV = OFF_SU + SG_WIDTH
OFF_RQ = OFF_SV + SG_WIDTH
OFF_RK = OFF_RQ + RET_WIDTH
OFF_RV = OFF_RK + RET_WIDTH
OFF_GF = OFF_RV + RET_WIDTH
OFF_GB = OFF_GF + RET_WIDTH

V7X_VMEM_LIMIT_BYTES = 56 * 1024 * 1024
NA_BLOCK_ROWS = 4
NA_BAND_ROWS = NA_BLOCK_ROWS + NA_WIN_R
FFN_CHUNKS = ((0, 1536), (1536, 1280))

BF16 = jnp.bfloat16
F32 = jnp.float32


def _params(*sem):
    return pltpu.CompilerParams(dimension_semantics=sem, vmem_limit_bytes=V7X_VMEM_LIMIT_BYTES)


def _resident(shape, index_map):
    return pl.BlockSpec(shape, index_map, pipeline_mode=pl.Buffered(1))


def _dot(a, b):
    return jnp.dot(a, b, preferred_element_type=F32)


def _dot_nt(a, b):
    return lax.dot_general(a, b, (((1,), (1,)), ((), ())), preferred_element_type=F32)


def _dot_tn(a, b):
    return lax.dot_general(a, b, (((0,), (0,)), ((), ())), preferred_element_type=F32)


def _seg_sum(t, ones_blk):
    hi = t.astype(BF16)
    lo = (t - hi.astype(F32)).astype(BF16)
    return _dot(hi, ones_blk) + _dot(lo, ones_blk)


def _ada_rms(x, nw, shift, scale):
    ms = jnp.mean(x * x, axis=-1, keepdims=True)
    return (x * lax.rsqrt(ms + RMS_EPS)) * nw * (1.0 + scale) + shift


def _adaln_kernel(c_ref, w_ref, b_ref, o_ref):
    a = jax.nn.silu(c_ref[...])
    o_ref[...] = jnp.dot(a, w_ref[...], precision=lax.Precision.HIGHEST,
                         preferred_element_type=F32) + b_ref[...]


def _adaln(cc, ada_w, ada_b):
    depth = ada_w.shape[0]
    rows = cc.shape[0]
    tn = D_MODEL
    return pl.pallas_call(
        _adaln_kernel,
        out_shape=jax.ShapeDtypeStruct((depth, rows, N_MOD * D_MODEL), F32),
        grid=(depth, N_MOD * D_MODEL // tn),
        in_specs=[
            pl.BlockSpec((rows, D_MODEL), lambda l, j: (0, 0)),
            pl.BlockSpec((None, D_MODEL, tn), lambda l, j: (l, 0, j)),
            pl.BlockSpec((None, 1, tn), lambda l, j: (l, 0, j)),
        ],
        out_specs=pl.BlockSpec((None, rows, tn), lambda l, j: (l, 0, j)),
        compiler_params=_params("arbitrary", "arbitrary"),
        name="adaln",
    )(cc, ada_w, ada_b.reshape(depth, 1, N_MOD * D_MODEL))


def _ffn_kernel(x_ref, mod_ref, nw_ref, w1_ref, w2_ref, o_ref, *, mod_off):
    x = x_ref[...]
    shift = mod_ref[mod_off:mod_off + 1, :]
    scale = mod_ref[mod_off + 1:mod_off + 2, :]
    gate = mod_ref[mod_off + 2:mod_off + 3, :]
    xb = _ada_rms(x, nw_ref[...], shift, scale).astype(BF16)
    acc = None
    for off, width in FFN_CHUNKS:
        g = _dot(xb, w1_ref[:, off:off + width])
        u = _dot(xb, w1_ref[:, D_FF + off:D_FF + off + width])
        a = (jax.nn.silu(g) * u).astype(BF16)
        part = _dot(a, w2_ref[off:off + width, :])
        acc = part if acc is None else acc + part
    o_ref[...] = x + (0.5 * gate) * acc


def _ffn(x, mod, mod_off, nw, w1, w2, tm):
    b, l, d = x.shape
    mod_map = (lambda i, j: (i, 0, 0)) if mod.shape[0] == b else (lambda i, j: (0, 0, 0))
    return pl.pallas_call(
        functools.partial(_ffn_kernel, mod_off=mod_off),
        out_shape=jax.ShapeDtypeStruct(x.shape, x.dtype),
        grid=(b, l // tm),
        in_specs=[
            pl.BlockSpec((None, tm, d), lambda i, j: (i, j, 0)),
            pl.BlockSpec((None, N_MOD, d), mod_map),
            _resident((1, d), lambda i, j: (0, 0)),
            _resident((d, 2 * D_FF), lambda i, j: (0, 0)),
            _resident((D_FF, d), lambda i, j: (0, 0)),
        ],
        out_specs=pl.BlockSpec((None, tm, d), lambda i, j: (i, j, 0)),
        compiler_params=_params("parallel", "parallel"),
        name="ffn",
    )(x, mod, nw.reshape(1, d), w1, w2)


def _head_rms(t, w, ones_blk):
    ms = _seg_sum(t * t, ones_blk) * (1.0 / HEAD_DIM)
    return (t * lax.rsqrt(ms + RMS_EPS)) * w


def _rope(t, cos, sin_next, sin_prev):
    outs = []
    for s in range(RET_WIDTH // 128):
        ts = t[:, s * 128:(s + 1) * 128]
        nxt = pltpu.roll(ts, 128 - ROPE_FREQS, axis=1)
        prv = pltpu.roll(ts, ROPE_FREQS, axis=1)
        outs.append(ts * cos + nxt * sin_next + prv * sin_prev)
    return jnp.concatenate(outs, axis=1)


def _mix_in_kernel(x_ref, mod_ref, nw_ref, w_ref, qn_ref, kn_ref, cos_ref, sn_ref, sp_ref,
                   sgw_ref, sgb_ref, lnw_ref, lnb_ref, e384_ref, e256_ref,
                   nq_ref, nk_ref, nv_ref, sg_ref, rq_ref, rk_ref, rv_ref, gf_ref, gb_ref):
    tm = x_ref.shape[0]
    x = x_ref[...]
    hb = _ada_rms(x, nw_ref[...], mod_ref[3:4, :], mod_ref[4:5, :]).astype(BF16)
    p = _dot(hb, w_ref[...])
    e384 = e384_ref[...]
    scale = HEAD_DIM ** -0.5

    nq_ref[...] = (_head_rms(p[:, OFF_NQ:OFF_NQ + NA_WIDTH], qn_ref[...], e384) * scale).astype(BF16)
    nk_ref[...] = _head_rms(p[:, OFF_NK:OFF_NK + NA_WIDTH], kn_ref[...], e384).astype(BF16)
    nv_ref[...] = p[:, OFF_NV:OFF_NV + NA_WIDTH].astype(BF16)

    u = jax.nn.gelu(p[:, OFF_SU:OFF_SU + SG_WIDTH])
    v = jax.nn.gelu(p[:, OFF_SV:OFF_SV + SG_WIDTH])
    e256 = e256_ref[...]
    mu = _seg_sum(v, e256) * (1.0 / HEAD_DIM)
    dv = v - mu
    var = _seg_sum(dv * dv, e256) * (1.0 / HEAD_DIM)
    vn = ((dv * lax.rsqrt(var + LN_EPS)) * lnw_ref[...] + lnb_ref[...]).astype(BF16)
    lane_grp = lax.broadcasted_iota(jnp.int32, (CHUNK, SG_WIDTH), 1) // HEAD_DIM
    for j in range(tm // CHUNK):
        vch = vn[j * CHUNK:(j + 1) * CHUNK, :]
        mixed = sgb_ref[...]
        for g in range(SG_GROUPS):
            mixed = mixed + jnp.where(lane_grp == g, _dot(sgw_ref[g], vch), 0.0)
        sg_ref[j * CHUNK:(j + 1) * CHUNK, :] = (u[j * CHUNK:(j + 1) * CHUNK, :] * mixed).astype(BF16)

    cos, sn, sp = cos_ref[...], sn_ref[...], sp_ref[...]
    rq_ref[...] = _rope(p[:, OFF_RQ:OFF_RQ + RET_WIDTH], cos, sn, sp).astype(BF16)
    rk_ref[...] = (_rope(p[:, OFF_RK:OFF_RK + RET_WIDTH], cos, sn, sp) * scale).astype(BF16)
    rv_ref[...] = p[:, OFF_RV:OFF_RV + RET_WIDTH].astype(BF16)
    gf_ref[...] = jax.nn.silu(p[:, OFF_GF:OFF_GF + RET_WIDTH])
    gb_ref[...] = jax.nn.silu(p[:, OFF_GB:OFF_GB + RET_WIDTH])


def _mix_in(x, mod, nw, w_in, qn, kn, rope_tabs, sgw, sgb_tab, lnw, lnb, e384, e256, tm):
    b, l, d = x.shape
    mod_map = (lambda i, j: (i, 0, 0)) if mod.shape[0] == b else (lambda i, j: (0, 0, 0))
    tok = lambda w: pl.BlockSpec((None, tm, w), lambda i, j: (i, j, 0))
    tab = pl.BlockSpec((tm, 128), lambda i, j: (j, 0))
    const = lambda shape: _resident(shape, lambda i, j: (0,) * len(shape))
    out_bf = lambda w: jax.ShapeDtypeStruct((b, l, w), BF16)
    out_f32 = lambda w: jax.ShapeDtypeStruct((b, l, w), F32)
    return pl.pallas_call(
        _mix_in_kernel,
        out_shape=(out_bf(NA_WIDTH), out_bf(NA_WIDTH), out_bf(NA_WIDTH), out_bf(SG_WIDTH),
                   out_bf(RET_WIDTH), out_bf(RET_WIDTH), out_bf(RET_WIDTH),
                   out_f32(RET_WIDTH), out_f32(RET_WIDTH)),
        grid=(b, l // tm),
        in_specs=[
            tok(d),
            pl.BlockSpec((None, N_MOD, d), mod_map),
            const((1, d)),
            const((d, IN_COLS)),
            const((1, NA_WIDTH)), const((1, NA_WIDTH)),
            tab, tab, tab,
            const((SG_GROUPS, CHUNK, CHUNK)), const((CHUNK, SG_WIDTH)),
            const((1, SG_WIDTH)), const((1, SG_WIDTH)),
            const((NA_WIDTH, NA_WIDTH)), const((SG_WIDTH, SG_WIDTH)),
        ],
        out_specs=(tok(NA_WIDTH), tok(NA_WIDTH), tok(NA_WIDTH), tok(SG_WIDTH),
                   tok(RET_WIDTH), tok(RET_WIDTH), tok(RET_WIDTH), tok(RET_WIDTH), tok(RET_WIDTH)),
        compiler_params=_params("parallel", "parallel"),
        name="mix_in",
    )(x, mod, nw.reshape(1, d), w_in, qn, kn, *rope_tabs, sgw, sgb_tab, lnw, lnb, e384, e256)


def _softmax_pv(s_list, v_list):
    m = functools.reduce(jnp.maximum, [jnp.max(s, axis=-1, keepdims=True) for s in s_list])
    ps = [jnp.exp(s - m) for s in s_list]
    den = functools.reduce(lambda a, c: a + c, [jnp.sum(p, axis=-1, keepdims=True) for p in ps])
    num = functools.reduce(lambda a, c: a + c, [_dot(p.astype(BF16), v) for p, v in zip(ps, v_list)])
    return num / den


def _na_kernel(q_ref, k_ref, v_ref, kc_ref, vc_ref, bias_ref, o_ref, *, rows):
    i = pl.program_id(1)
    band_row = jnp.clip(i * NA_BLOCK_ROWS - NA_WIN_R // 2, 0, rows - NA_BAND_ROWS)
    start = pl.multiple_of(band_row * GRID_W, GRID_W)
    kband = k_ref[pl.ds(start, NA_BAND_ROWS * GRID_W), :]
    vband = v_ref[pl.ds(start, NA_BAND_ROWS * GRID_W), :]
    q = q_ref[...]
    kc = kc_ref[...]
    vc = vc_ref[...]
    for h in range(NA_HEADS):
        sl = slice(h * HEAD_DIM, (h + 1) * HEAD_DIM)
        qh = q[:, sl]
        s_loc = _dot_nt(qh, kband[:, sl]) + bias_ref[h]
        s_ctx = _dot_nt(qh, kc[:, sl])
        o_ref[:, sl] = _softmax_pv([s_loc, s_ctx], [vband[:, sl], vc[:, sl]]).astype(BF16)


def _na_bias_table(rpb, rows):
    r, band = NA_BLOCK_ROWS, NA_BAND_ROWS
    n_blocks = rows // r
    tabs = []
    for blk in (0, 1, n_blocks - 1):
        r0 = blk * r
        b0 = int(np.clip(r0 - NA_WIN_R // 2, 0, rows - band))
        qr = np.arange(r0, r0 + r)[:, None, None, None]
        qc = np.arange(GRID_W)[None, :, None, None]
        kr = np.arange(b0, b0 + band)[None, None, :, None]
        kcol = np.arange(GRID_W)[None, None, None, :]
        r_start = np.clip(qr - NA_WIN_R // 2, 0, rows - NA_WIN_R)
        c_start = np.clip(qc - NA_WIN_C // 2, 0, GRID_W - NA_WIN_C)
        ok = (kr >= r_start) & (kr < r_start + NA_WIN_R) & (kcol >= c_start) & (kcol < c_start + NA_WIN_C)
        ok = np.broadcast_to(ok, (r, GRID_W, band, GRID_W))
        ro = np.clip(np.broadcast_to(kr - qr + NA_WIN_R - 1, ok.shape), 0, 2 * NA_WIN_R - 2)
        co = np.clip(np.broadcast_to(kcol - qc + NA_WIN_C - 1, ok.shape), 0, 2 * NA_WIN_C - 2)
        bias = rpb[:, ro, co]
        bias = jnp.where(ok[None], bias, NEG_INF)
        tabs.append(bias.reshape(NA_HEADS, r * GRID_W, band * GRID_W))
    return jnp.stack(tabs)


def _na(q, k, v, kc, vc, bias_tab, rows):
    b, s, w = q.shape
    lc = kc.shape[1]
    r = NA_BLOCK_ROWS
    n_blocks = rows // r
    tq, tk = r * GRID_W, NA_BAND_ROWS * GRID_W

    def variant(i, j):
        return (jnp.where(j == 0, 0, jnp.where(j == n_blocks - 1, 2, 1)), 0, 0, 0)

    return pl.pallas_call(
        functools.partial(_na_kernel, rows=rows),
        out_shape=jax.ShapeDtypeStruct((b, s, w), BF16),
        grid=(b, n_blocks),
        in_specs=[
            pl.BlockSpec((None, tq, w), lambda i, j: (i, j, 0)),
            pl.BlockSpec((None, s, w), lambda i, j: (i, 0, 0)),
            pl.BlockSpec((None, s, w), lambda i, j: (i, 0, 0)),
            pl.BlockSpec((None, lc, w), lambda i, j: (i, 0, 0)),
            pl.BlockSpec((None, lc, w), lambda i, j: (i, 0, 0)),
            pl.BlockSpec((None, NA_HEADS, tq, tk), variant),
        ],
        out_specs=pl.BlockSpec((None, tq, w), lambda i, j: (i, j, 0)),
        compiler_params=_params("parallel", "arbitrary"),
        name="na",
    )(q, k, v, kc, vc, bias_tab)


def _ctx_attn_kernel(q_ref, k_ref, v_ref, o_ref):
    q, k, v = q_ref[...], k_ref[...], v_ref[...]
    for h in range(NA_HEADS):
        sl = slice(h * HEAD_DIM, (h + 1) * HEAD_DIM)
        o_ref[:, sl] = _softmax_pv([_dot_nt(q[:, sl], k[:, sl])], [v[:, sl]]).astype(BF16)


def _ctx_attn(q, k, v):
    b, lc, w = q.shape
    spec = pl.BlockSpec((None, lc, w), lambda i: (i, 0, 0))
    return pl.pallas_call(
        _ctx_attn_kernel,
        out_shape=jax.ShapeDtypeStruct((b, lc, w), BF16),
        grid=(b,),
        in_specs=[spec, spec, spec],
        out_specs=spec,
        compiler_params=_params("parallel"),
        name="ctx_attn",
    )(q, k, v)


def _ret_kernel(q_ref, k_ref, v_ref, ks_ref, vs_ref, ws_ref, dm_ref, te_ref, fs_ref, cd_ref,
                o_ref, state_ref, *, n_chunks):
    d = pl.program_id(1)
    n = pl.program_id(2)

    @pl.when(n == 0)
    def _seed():
        kw = (ks_ref[...].astype(F32) * ws_ref[...]).astype(BF16)
        vs = vs_ref[...]
        for h in range(RET_HEADS):
            sl = slice(h * HEAD_DIM, (h + 1) * HEAD_DIM)
            state_ref[h] = _dot_tn(kw[:, sl], vs[:, sl])

    te = te_ref[...]
    fs = fs_ref[...]
    for j in range(n_chunks):
        c = jnp.where(d == 0, j, n_chunks - 1 - j)
        rows = pl.ds(pl.multiple_of(c * CHUNK, CHUNK), CHUNK)
        q = q_ref[rows, :]
        k = k_ref[rows, :]
        v = v_ref[rows, :]
        qf = (q.astype(F32) * fs).astype(BF16)
        kt = (k.astype(F32) * te).astype(BF16)
        for h in range(RET_HEADS):
            sl = slice(h * HEAD_DIM, (h + 1) * HEAD_DIM)
            st = state_ref[h]
            scores = (_dot_nt(q[:, sl], k[:, sl]) * dm_ref[h]).astype(BF16)
            o_ref[rows, sl] = _dot(scores, v[:, sl]) + _dot(qf[:, sl], st.astype(BF16))
            state_ref[h] = cd_ref[h] * st + _dot_tn(kt[:, sl], v[:, sl])


def _ret_tables(log_g, seed_len, seeded):
    pos = jnp.arange(CHUNK, dtype=F32)
    diff = pos[:, None] - pos[None, :]
    lg = log_g[:, :, None, None]
    dm_f = jnp.where(diff >= 0, jnp.exp(jnp.maximum(diff, 0.0)[None, None] * lg), 0.0)[0]
    dm_b = jnp.where(diff <= 0, jnp.exp(jnp.maximum(-diff, 0.0)[None, None] * lg), 0.0)[1]
    dm = jnp.stack([dm_f, dm_b])
    lane = lambda t: jnp.repeat(t, HEAD_DIM, axis=-1)
    te = jnp.stack([jnp.exp((CHUNK - 1 - pos)[:, None] * log_g[0][None, :]),
                    jnp.exp(pos[:, None] * log_g[1][None, :])])
    fs = jnp.stack([jnp.exp((pos + 1.0)[:, None] * log_g[0][None, :]),
                    jnp.exp((CHUNK - pos)[:, None] * log_g[1][None, :])])
    cd = jnp.broadcast_to(jnp.exp(CHUNK * log_g)[:, :, None, None],
                          (2, RET_HEADS, HEAD_DIM, HEAD_DIM))
    sp = jnp.arange(seed_len, dtype=F32)
    ws = jnp.stack([jnp.exp((seed_len - 1 - sp)[:, None] * log_g[0][None, :]),
                    jnp.exp(sp[:, None] * log_g[1][None, :])])
    if not seeded:
        ws = jnp.zeros_like(ws)
    return dm, lane(te), lane(fs), cd, lane(ws)


def _retention(q, k, v, k_seed, v_seed, tables, tb):
    b, l, w = q.shape
    ls = k_seed.shape[1]
    nb = l // tb
    dm, te, fs, cd, ws = tables
    blk = lambda i, d, n: (i, jnp.where(d == 0, n, nb - 1 - n), 0)
    tok = pl.BlockSpec((None, tb, w), blk)
    seed = pl.BlockSpec((None, ls, w), lambda i, d, n: (i, 0, 0))
    return pl.pallas_call(
        functools.partial(_ret_kernel, n_chunks=tb // CHUNK),
        out_shape=jax.ShapeDtypeStruct((2, b, l, w), F32),
        grid=(b, 2, nb),
        in_specs=[
            tok, tok, tok, seed, seed,
            pl.BlockSpec((None, ls, w), lambda i, d, n: (d, 0, 0)),
            pl.BlockSpec((None, RET_HEADS, CHUNK, CHUNK), lambda i, d, n: (d, 0, 0, 0)),
            pl.BlockSpec((None, CHUNK, w), lambda i, d, n: (d, 0, 0)),
            pl.BlockSpec((None, CHUNK, w), lambda i, d, n: (d, 0, 0)),
            pl.BlockSpec((None, RET_HEADS, HEAD_DIM, HEAD_DIM), lambda i, d, n: (d, 0, 0, 0)),
        ],
        out_specs=pl.BlockSpec((None, None, tb, w), lambda i, d, n: (d,) + blk(i, d, n)),
        scratch_shapes=[pltpu.VMEM((RET_HEADS, HEAD_DIM, HEAD_DIM), F32)],
        compiler_params=_params("parallel", "arbitrary", "arbitrary"),
        name="retention",
    )(q, k, v, k_seed, v_seed, ws, dm, te, fs, cd)


def _head_group_norm(o, ones_blk):
    mu = _seg_sum(o, ones_blk) * (1.0 / HEAD_DIM)
    dlt = o - mu
    var = _seg_sum(dlt * dlt, ones_blk) * (1.0 / HEAD_DIM)
    return dlt * lax.rsqrt(var + LN_EPS)


def _mix_out_kernel(x_ref, mod_ref, na_ref, sg_ref, of_ref, ob_ref, gf_ref, gb_ref, gnw_ref,
                    w_ref, e384_ref, o_ref):
    e384 = e384_ref[...]
    y = (_head_group_norm(of_ref[...], e384) * gf_ref[...]
         + _head_group_norm(ob_ref[...], e384) * gb_ref[...])
    ret = (y * gnw_ref[...]).astype(BF16)
    o_sg = NA_WIDTH + SG_WIDTH
    out = (_dot(na_ref[...], w_ref[0:NA_WIDTH, :])
           + _dot(sg_ref[...], w_ref[NA_WIDTH:o_sg, :])
           + _dot(ret, w_ref[o_sg:, :]))
    o_ref[...] = x_ref[...] + mod_ref[5:6, :] * out


def _mix_out(x, mod, na, sg, o_ret, gf, gb, gnw, w_out, e384, tm):
    b, l, d = x.shape
    mod_map = (lambda i, j: (i, 0, 0)) if mod.shape[0] == b else (lambda i, j: (0, 0, 0))
    tok = lambda w: pl.BlockSpec((None, tm, w), lambda i, j: (i, j, 0))
    dirn = lambda dd: pl.BlockSpec((None, None, tm, RET_WIDTH), lambda i, j: (dd, i, j, 0))
    const = lambda shape: _resident(shape, lambda i, j: (0,) * len(shape))
    return pl.pallas_call(
        _mix_out_kernel,
        out_shape=jax.ShapeDtypeStruct(x.shape, x.dtype),
        grid=(b, l // tm),
        in_specs=[
            tok(d),
            pl.BlockSpec((None, N_MOD, d), mod_map),
            tok(NA_WIDTH), tok(SG_WIDTH), dirn(0), dirn(1), tok(RET_WIDTH), tok(RET_WIDTH),
            const((1, RET_WIDTH)), const((d, d)), const((NA_WIDTH, NA_WIDTH)),
        ],
        out_specs=tok(d),
        compiler_params=_params("parallel", "parallel"),
        name="mix_out",
    )(x, mod, na, sg, o_ret, o_ret, gf, gb, gnw, w_out, e384)


def _ones_blocks(width):
    seg = np.arange(width) // HEAD_DIM
    return jnp.asarray(seg[:, None] == seg[None, :], dtype=BF16)


def _rope_tables(n_lat):
    t = jnp.arange(n_lat)
    inv = ROPE_BASE ** (-jnp.arange(ROPE_FREQS, dtype=F32) / ROPE_FREQS)
    ang_r = (t // GRID_W).astype(F32)[:, None] * inv[None, :]
    ang_c = (t % GRID_W).astype(F32)[:, None] * inv[None, :]
    ang = jnp.concatenate([ang_r, ang_r, ang_c, ang_c] * 2, axis=1)
    first = (np.arange(128) // ROPE_FREQS) % 2 == 0
    sin = jnp.sin(ang)
    return jnp.cos(ang), jnp.where(first, -sin, 0.0), jnp.where(first, 0.0, sin)


def kernel(x, c, ctx, c_ctx, ada_w, ada_b, norm_w, ffn_w1, ffn_w2, mix_w_in, mix_w_out, na_q_norm,
           na_k_norm, na_rpb, sg_w, sg_b, sg_ln_w, sg_ln_b, ret_decay_logit, ret_gn_w):
    b, n_lat, d = x.shape
    lc = ctx.shape[1]
    depth = ada_w.shape[0]
    rows = n_lat // GRID_W
    assert d == D_MODEL and n_lat % (GRID_W * NA_BLOCK_ROWS) == 0 and rows >= 2 * NA_BAND_ROWS - NA_WIN_R
    assert lc % CHUNK == 0
    tm_x = min(512, n_lat)
    tm_c = min(256, lc)
    assert n_lat % tm_x == 0 and lc % tm_c == 0 and tm_x % CHUNK == 0 and tm_c % CHUNK == 0

    cc = jnp.zeros((16, d), F32).at[:b].set(c).at[b].set(c_ctx)
    mod = _adaln(cc, ada_w, ada_b)

    e384, e256 = _ones_blocks(NA_WIDTH), _ones_blocks(SG_WIDTH)
    rope_x = _rope_tables(n_lat)
    rope_c = (jnp.ones((lc, 128), F32), jnp.zeros((lc, 128), F32), jnp.zeros((lc, 128), F32))
    w1 = ffn_w1.astype(BF16)
    w2 = ffn_w2.astype(BF16)
    w_in = mix_w_in.astype(BF16)
    w_out = mix_w_out.astype(BF16)
    sgw = sg_w.astype(BF16)
    log_g = jax.nn.log_sigmoid(ret_decay_logit.astype(F32))

    for layer in range(depth):
        last = layer == depth - 1
        mod_x = mod[layer, :b].reshape(b, N_MOD, d)
        mod_c = mod[layer, b:b + 1].reshape(1, N_MOD, d)
        nw = norm_w[layer]
        x = _ffn(x, mod_x, 0, nw[0], w1[layer, 0], w2[layer, 0], tm_x)
        ctx = _ffn(ctx, mod_c, 0, nw[0], w1[layer, 0], w2[layer, 0], tm_c)

        qn = jnp.tile(na_q_norm[layer], NA_HEADS).reshape(1, NA_WIDTH)
        kn = jnp.tile(na_k_norm[layer], NA_HEADS).reshape(1, NA_WIDTH)
        sgb_tab = jnp.repeat(sg_b[layer].T, HEAD_DIM, axis=1)
        lnw = sg_ln_w[layer].reshape(1, SG_WIDTH)
        lnb = sg_ln_b[layer].reshape(1, SG_WIDTH)
        gnw = ret_gn_w[layer].reshape(1, RET_WIDTH)
        px = _mix_in(x, mod_x, nw[1], w_in[layer], qn, kn, rope_x, sgw[layer], sgb_tab, lnw, lnb,
                     e384, e256, tm_x)
        pc = _mix_in(ctx, mod_c, nw[1], w_in[layer], qn, kn, rope_c, sgw[layer], sgb_tab, lnw, lnb,
                     e384, e256, tm_c)
        nq_x, nk_x, nv_x, sg_x, rq_x, rk_x, rv_x, gf_x, gb_x = px
        nq_c, nk_c, nv_c, sg_c, rq_c, rk_c, rv_c, gf_c, gb_c = pc

        na_x = _na(nq_x, nk_x, nv_x, nk_c, nv_c, _na_bias_table(na_rpb[layer], rows), rows)
        o_x = _retention(rq_x, rk_x, rv_x, rk_c, rv_c, _ret_tables(log_g[layer], lc, True), tm_x)
        x = _mix_out(x, mod_x, na_x, sg_x, o_x, gf_x, gb_x, gnw, w_out[layer], e384, tm_x)
        x = _ffn(x, mod_x, 6, nw[2], w1[layer, 1], w2[layer, 1], tm_x)
        if not last:
            na_c = _ctx_attn(nq_c, nk_c, nv_c)
            o_c = _retention(rq_c, rk_c, rv_c, rk_c, rv_c, _ret_tables(log_g[layer], lc, False), tm_c)
            ctx = _mix_out(ctx, mod_c, na_c, sg_c, o_c, gf_c, gb_c, gnw, w_out[layer], e384, tm_c)
            ctx = _ffn(ctx, mod_c, 6, nw[2], w1[layer, 1], w2[layer, 1], tm_c)
    return x
```

```python
import functools

import numpy as np
import jax
import jax.numpy as jnp
from jax import lax
from jax.experimental import pallas as pl
from jax.experimental.pallas import tpu as pltpu

D_MODEL = 1024
GRID_W = 64
N_MOD = 9
D_FF = 2816
NA_HEADS = 6
HEAD_DIM = 64
NA_WIDTH = NA_HEADS * HEAD_DIM
NA_WIN_R = 8
NA_WIN_C = 16
SG_GROUPS = 4
SG_WIDTH = SG_GROUPS * HEAD_DIM
CHUNK = 128
RET_HEADS = 6
RET_WIDTH = RET_HEADS * HEAD_DIM
IN_COLS = 3 * NA_WIDTH + 2 * SG_WIDTH + 5 * RET_WIDTH
ROPE_BASE = 10000.0
ROPE_FREQS = HEAD_DIM // 4
RMS_EPS = 1e-6
LN_EPS = 1e-5
NEG_INF = -1e30
LOG2E = 1.4426950408889634

OFF_NQ, OFF_NK, OFF_NV = 0, NA_WIDTH, 2 * NA_WIDTH
OFF_SU = 3 * NA_WIDTH
OFF_SV = OFF_SU + SG_WIDTH
OFF_RQ = OFF_SV + SG_WIDTH
OFF_RK = OFF_RQ + RET_WIDTH
OFF_RV = OFF_RK + RET_WIDTH
OFF_GF = OFF_RV + RET_WIDTH
OFF_GB = OFF_GF + RET_WIDTH

V7X_VMEM_LIMIT_BYTES = 56 * 1024 * 1024
NA_BLOCK_ROWS = 4
NA_BAND_ROWS = NA_BLOCK_ROWS + NA_WIN_R
FFN_CHUNKS = ((0, 1536), (1536, 1280))

BF16 = jnp.bfloat16
F32 = jnp.float32


def _params(*sem):
    return pltpu.CompilerParams(dimension_semantics=sem, vmem_limit_bytes=V7X_VMEM_LIMIT_BYTES)


def _resident(shape, index_map):
    return pl.BlockSpec(shape, index_map, pipeline_mode=pl.Buffered(1))


def _dot(a, b):
    return jnp.dot(a, b, preferred_element_type=F32)


def _dot_nt(a, b):
    return lax.dot_general(a, b, (((1,), (1,)), ((), ())), preferred_element_type=F32)


def _dot_tn(a, b):
    return lax.dot_general(a, b, (((0,), (0,)), ((), ())), preferred_element_type=F32)


def _seg_sum(t, ones_blk):
    hi = t.astype(BF16)
    lo = (t - hi.astype(F32)).astype(BF16)
    return _dot(hi, ones_blk) + _dot(lo, ones_blk)


def _ada_rms(x, nw, shift, scale):
    ms = jnp.mean(x * x, axis=-1, keepdims=True)
    return (x * lax.rsqrt(ms + RMS_EPS)) * nw * (1.0 + scale) + shift


def _adaln_kernel(c_ref, w_ref, b_ref, o_ref):
    a = jax.nn.silu(c_ref[...])
    o_ref[...] = jnp.dot(a, w_ref[...], precision=lax.Precision.HIGHEST,
                         preferred_element_type=F32) + b_ref[...]


def _adaln(cc, ada_w, ada_b):
    depth = ada_w.shape[0]
    rows = cc.shape[0]
    tn = D_MODEL
    return pl.pallas_call(
        _adaln_kernel,
        out_shape=jax.ShapeDtypeStruct((depth, rows, N_MOD * D_MODEL), F32),
        grid=(depth, N_MOD * D_MODEL // tn),
        in_specs=[
            pl.BlockSpec((rows, D_MODEL), lambda l, j: (0, 0)),
            pl.BlockSpec((None, D_MODEL, tn), lambda l, j: (l, 0, j)),
            pl.BlockSpec((None, 1, tn), lambda l, j: (l, 0, j)),
        ],
        out_specs=pl.BlockSpec((None, rows, tn), lambda l, j: (l, 0, j)),
        compiler_params=_params("arbitrary", "arbitrary"),
        name="adaln",
    )(cc, ada_w, ada_b.reshape(depth, 1, N_MOD * D_MODEL))


def _ffn_kernel(x_ref, mod_ref, nw_ref, w1_ref, w2_ref, o_ref, *, mod_off):
    x = x_ref[...]
    shift = mod_ref[mod_off:mod_off + 1, :]
    scale = mod_ref[mod_off + 1:mod_off + 2, :]
    gate = mod_ref[mod_off + 2:mod_off + 3, :]
    xb = _ada_rms(x, nw_ref[...], shift, scale).astype(BF16)
    acc = None
    for off, width in FFN_CHUNKS:
        g = _dot(xb, w1_ref[:, off:off + width])
        u = _dot(xb, w1_ref[:, D_FF + off:D_FF + off + width])
        a = (jax.nn.silu(g) * u).astype(BF16)
        part = _dot(a, w2_ref[off:off + width, :])
        acc = part if acc is None else acc + part
    o_ref[...] = x + (0.5 * gate) * acc


def _ffn(x, mod, mod_off, nw, w1, w2, tm):
    b, l, d = x.shape
    mod_map = (lambda i, j: (i, 0, 0)) if mod.shape[0] == b else (lambda i, j: (0, 0, 0))
    return pl.pallas_call(
        functools.partial(_ffn_kernel, mod_off=mod_off),
        out_shape=jax.ShapeDtypeStruct(x.shape, x.dtype),
        grid=(b, l // tm),
        in_specs=[
            pl.BlockSpec((None, tm, d), lambda i, j: (i, j, 0)),
            pl.BlockSpec((None, N_MOD, d), mod_map),
            _resident((1, d), lambda i, j: (0, 0)),
            _resident((d, 2 * D_FF), lambda i, j: (0, 0)),
            _resident((D_FF, d), lambda i, j: (0, 0)),
        ],
        out_specs=pl.BlockSpec((None, tm, d), lambda i, j: (i, j, 0)),
        compiler_params=_params("parallel", "parallel"),
        name="ffn",
    )(x, mod, nw.reshape(1, d), w1, w2)


def _head_rms(t, w, ones_blk):
    ms = _seg_sum(t * t, ones_blk) * (1.0 / HEAD_DIM)
    return (t * lax.rsqrt(ms + RMS_EPS)) * w


def _rope(t, cos, sin_next, sin_prev):
    outs = []
    for s in range(RET_WIDTH // 128):
        ts = t[:, s * 128:(s + 1) * 128]
        nxt = pltpu.roll(ts, 128 - ROPE_FREQS, axis=1)
        prv = pltpu.roll(ts, ROPE_FREQS, axis=1)
        outs.append(ts * cos + nxt * sin_next + prv * sin_prev)
    return jnp.concatenate(outs, axis=1)


def _mix_in_kernel(x_ref, mod_ref, nw_ref, w_ref, qn_ref, kn_ref, cos_ref, sn_ref, sp_ref,
                   sgw_ref, sgb_ref, lnw_ref, lnb_ref, e384_ref, e256_ref,
                   nq_ref, nk_ref, nv_ref, sg_ref, rq_ref, rk_ref, rv_ref, gf_ref, gb_ref):
    tm = x_ref.shape[0]
    x = x_ref[...]
    hb = _ada_rms(x, nw_ref[...], mod_ref[3:4, :], mod_ref[4:5, :]).astype(BF16)
    p = _dot(hb, w_ref[...])
    e384 = e384_ref[...]
    scale = HEAD_DIM ** -0.5

    nq_ref[...] = (_head_rms(p[:, OFF_NQ:OFF_NQ + NA_WIDTH], qn_ref[...], e384)
                   * (scale * LOG2E)).astype(BF16)
    nk_ref[...] = _head_rms(p[:, OFF_NK:OFF_NK + NA_WIDTH], kn_ref[...], e384).astype(BF16)
    nv_ref[...] = p[:, OFF_NV:OFF_NV + NA_WIDTH].astype(BF16)

    u = jax.nn.gelu(p[:, OFF_SU:OFF_SU + SG_WIDTH])
    v = jax.nn.gelu(p[:, OFF_SV:OFF_SV + SG_WIDTH])
    e256 = e256_ref[...]
    mu = _seg_sum(v, e256) * (1.0 / HEAD_DIM)
    dv = v - mu
    var = _seg_sum(dv * dv, e256) * (1.0 / HEAD_DIM)
    vn = ((dv * lax.rsqrt(var + LN_EPS)) * lnw_ref[...] + lnb_ref[...]).astype(BF16)
    lane_grp = lax.broadcasted_iota(jnp.int32, (CHUNK, SG_WIDTH), 1) // HEAD_DIM
    for j in range(tm // CHUNK):
        vch = vn[j * CHUNK:(j + 1) * CHUNK, :]
        mixed = sgb_ref[...]
        for g in range(SG_GROUPS):
            mixed = mixed + jnp.where(lane_grp == g, _dot(sgw_ref[g], vch), 0.0)
        sg_ref[j * CHUNK:(j + 1) * CHUNK, :] = (u[j * CHUNK:(j + 1) * CHUNK, :] * mixed).astype(BF16)

    cos, sn, sp = cos_ref[...], sn_ref[...], sp_ref[...]
    rq_ref[...] = _rope(p[:, OFF_RQ:OFF_RQ + RET_WIDTH], cos, sn, sp).astype(BF16)
    rk_ref[...] = (_rope(p[:, OFF_RK:OFF_RK + RET_WIDTH], cos, sn, sp) * scale).astype(BF16)
    rv_ref[...] = p[:, OFF_RV:OFF_RV + RET_WIDTH].astype(BF16)
    gf_ref[...] = jax.nn.silu(p[:, OFF_GF:OFF_GF + RET_WIDTH])
    gb_ref[...] = jax.nn.silu(p[:, OFF_GB:OFF_GB + RET_WIDTH])


def _mix_in(x, mod, nw, w_in, qn, kn, rope_tabs, sgw, sgb_tab, lnw, lnb, e384, e256, tm):
    b, l, d = x.shape
    mod_map = (lambda i, j: (i, 0, 0)) if mod.shape[0] == b else (lambda i, j: (0, 0, 0))
    tok = lambda w: pl.BlockSpec((None, tm, w), lambda i, j: (i, j, 0))
    tab = pl.BlockSpec((tm, 128), lambda i, j: (j, 0))
    const = lambda shape: _resident(shape, lambda i, j: (0,) * len(shape))
    out_bf = lambda w: jax.ShapeDtypeStruct((b, l, w), BF16)
    out_f32 = lambda w: jax.ShapeDtypeStruct((b, l, w), F32)
    return pl.pallas_call(
        _mix_in_kernel,
        out_shape=(out_bf(NA_WIDTH), out_bf(NA_WIDTH), out_bf(NA_WIDTH), out_bf(SG_WIDTH),
                   out_bf(RET_WIDTH), out_bf(RET_WIDTH), out_bf(RET_WIDTH),
                   out_f32(RET_WIDTH), out_f32(RET_WIDTH)),
        grid=(b, l // tm),
        in_specs=[
            tok(d),
            pl.BlockSpec((None, N_MOD, d), mod_map),
            const((1, d)),
            const((d, IN_COLS)),
            const((1, NA_WIDTH)), const((1, NA_WIDTH)),
            tab, tab, tab,
            const((SG_GROUPS, CHUNK, CHUNK)), const((CHUNK, SG_WIDTH)),
            const((1, SG_WIDTH)), const((1, SG_WIDTH)),
            const((NA_WIDTH, NA_WIDTH)), const((SG_WIDTH, SG_WIDTH)),
        ],
        out_specs=(tok(NA_WIDTH), tok(NA_WIDTH), tok(NA_WIDTH), tok(SG_WIDTH),
                   tok(RET_WIDTH), tok(RET_WIDTH), tok(RET_WIDTH), tok(RET_WIDTH), tok(RET_WIDTH)),
        compiler_params=_params("parallel", "parallel"),
        name="mix_in",
    )(x, mod, nw.reshape(1, d), w_in, qn, kn, *rope_tabs, sgw, sgb_tab, lnw, lnb, e384, e256)


def _attend_head_pair(q, keys, vals, biases):
    m = q.shape[0]
    even_q = lax.broadcasted_iota(jnp.int32, (m, 128), 1) < HEAD_DIM
    nums = []
    for parity in range(2):
        own = even_q if parity == 0 else jnp.logical_not(even_q)
        qp = jnp.where(own, q, jnp.zeros_like(q))
        scores = []
        for kk, bias in zip(keys, biases):
            s = _dot_nt(qp, kk)
            scores.append(s if bias is None else s + bias(parity))
        mx = functools.reduce(jnp.maximum, [jnp.max(s, axis=-1, keepdims=True) for s in scores])
        num = None
        for s, vv in zip(scores, vals):
            even_v = lax.broadcasted_iota(jnp.int32, vv.shape, 1) < HEAD_DIM
            own_v = even_v if parity == 0 else jnp.logical_not(even_v)
            part = _dot(jnp.exp2(s - mx).astype(BF16), jnp.where(own_v, vv, jnp.ones_like(vv)))
            num = part if num is None else num + part
        nums.append(num)
    numer = jnp.where(even_q, nums[0], nums[1])
    den = pltpu.roll(jnp.where(even_q, nums[1], nums[0]), HEAD_DIM, axis=1)
    return numer / den


def _na_kernel(q_ref, k_ref, v_ref, kc_ref, vc_ref, bias_ref, o_ref, *, rows):
    i = pl.program_id(1)
    band_row = jnp.clip(i * NA_BLOCK_ROWS - NA_WIN_R // 2, 0, rows - NA_BAND_ROWS)
    start = pl.multiple_of(band_row * GRID_W, GRID_W)
    band = pl.ds(start, NA_BAND_ROWS * GRID_W)
    for s in range(NA_WIDTH // 128):
        sl = slice(s * 128, (s + 1) * 128)
        out = _attend_head_pair(
            q_ref[:, sl], [k_ref[band, sl], kc_ref[:, sl]], [v_ref[band, sl], vc_ref[:, sl]],
            [lambda parity, s=s: bias_ref[2 * s + parity], None])
        o_ref[:, sl] = out.astype(BF16)


def _na_bias_table(rpb, rows):
    r, band = NA_BLOCK_ROWS, NA_BAND_ROWS
    n_blocks = rows // r
    qc = np.arange(GRID_W)[:, None]
    kcol = np.arange(GRID_W)[None, :]
    c_start = np.clip(qc - NA_WIN_C // 2, 0, GRID_W - NA_WIN_C)
    col_ok = (kcol >= c_start) & (kcol < c_start + NA_WIN_C)
    col_sel = ((kcol - qc + NA_WIN_C - 1)[:, :, None] == np.arange(2 * NA_WIN_C - 1)) & col_ok[:, :, None]
    row_sel = np.zeros((3, r, band, 2 * NA_WIN_R - 1), np.float32)
    for vi, blk in enumerate((0, 1, n_blocks - 1)):
        r0 = blk * r
        b0 = int(np.clip(r0 - NA_WIN_R // 2, 0, rows - band))
        for qi in range(r):
            r_start = int(np.clip(r0 + qi - NA_WIN_R // 2, 0, rows - NA_WIN_R))
            for ki in range(band):
                if r_start <= b0 + ki < r_start + NA_WIN_R:
                    row_sel[vi, qi, ki, b0 + ki - (r0 + qi) + NA_WIN_R - 1] = 1.0
    ok = (row_sel.sum(-1) > 0)[:, None, :, None, :, None] & col_ok[None, None, None, :, None, :]
    hp = lax.Precision.HIGHEST
    t_col = jnp.einsum('hij,qkj->hiqk', rpb, jnp.asarray(col_sel, F32), precision=hp)
    tab = jnp.einsum('hiqk,vrsi->vhrqsk', t_col, jnp.asarray(row_sel), precision=hp)
    tab = jnp.where(ok, tab * LOG2E, NEG_INF)
    return tab.reshape(3, NA_HEADS, r * GRID_W, band * GRID_W)


def _na(q, k, v, kc, vc, bias_tab, rows):
    b, s, w = q.shape
    lc = kc.shape[1]
    r = NA_BLOCK_ROWS
    n_blocks = rows // r
    tq, tk = r * GRID_W, NA_BAND_ROWS * GRID_W

    def variant(i, j):
        return (jnp.where(j == 0, 0, jnp.where(j == n_blocks - 1, 2, 1)), 0, 0, 0)

    return pl.pallas_call(
        functools.partial(_na_kernel, rows=rows),
        out_shape=jax.ShapeDtypeStruct((b, s, w), BF16),
        grid=(b, n_blocks),
        in_specs=[
            pl.BlockSpec((None, tq, w), lambda i, j: (i, j, 0)),
            pl.BlockSpec((None, s, w), lambda i, j: (i, 0, 0)),
            pl.BlockSpec((None, s, w), lambda i, j: (i, 0, 0)),
            pl.BlockSpec((None, lc, w), lambda i, j: (i, 0, 0)),
            pl.BlockSpec((None, lc, w), lambda i, j: (i, 0, 0)),
            pl.BlockSpec((None, NA_HEADS, tq, tk), variant),
        ],
        out_specs=pl.BlockSpec((None, tq, w), lambda i, j: (i, j, 0)),
        compiler_params=_params("parallel", "arbitrary"),
        name="na",
    )(q, k, v, kc, vc, bias_tab)


def _ctx_attn_kernel(q_ref, k_ref, v_ref, o_ref):
    for s in range(NA_WIDTH // 128):
        sl = slice(s * 128, (s + 1) * 128)
        o_ref[:, sl] = _attend_head_pair(q_ref[:, sl], [k_ref[:, sl]], [v_ref[:, sl]], [None]).astype(BF16)


def _ctx_attn(q, k, v):
    b, lc, w = q.shape
    spec = pl.BlockSpec((None, lc, w), lambda i: (i, 0, 0))
    return pl.pallas_call(
        _ctx_attn_kernel,
        out_shape=jax.ShapeDtypeStruct((b, lc, w), BF16),
        grid=(b,),
        in_specs=[spec, spec, spec],
        out_specs=spec,
        compiler_params=_params("parallel"),
        name="ctx_attn",
    )(q, k, v)


def _ret_kernel(q_ref, k_ref, v_ref, ks_ref, vs_ref, ws_ref, dm_ref, te_ref, fs_ref, cd_ref,
                o_ref, state_ref, *, n_chunks):
    d = pl.program_id(1)
    n = pl.program_id(2)
    same_head = (lax.broadcasted_iota(jnp.int32, (128, 128), 0) // HEAD_DIM
                 == lax.broadcasted_iota(jnp.int32, (128, 128), 1) // HEAD_DIM)
    even = lax.broadcasted_iota(jnp.int32, (CHUNK, 128), 1) < HEAD_DIM

    @pl.when(n == 0)
    def _seed():
        for s in range(RET_WIDTH // 128):
            sl = slice(s * 128, (s + 1) * 128)
            kw = (ks_ref[:, sl].astype(F32) * ws_ref[:, sl]).astype(BF16)
            state_ref[s] = jnp.where(same_head, _dot_tn(kw, vs_ref[:, sl]), 0.0)

    for s in range(RET_WIDTH // 128):
        sl = slice(s * 128, (s + 1) * 128)
        st = state_ref[s]
        for j in range(n_chunks):
            c = jnp.where(d == 0, j, n_chunks - 1 - j)
            rows = pl.ds(pl.multiple_of(c * CHUNK, CHUNK), CHUNK)
            q = q_ref[rows, sl]
            k = k_ref[rows, sl]
            v = v_ref[rows, sl]
            zero = jnp.zeros_like(k)
            k2 = jnp.concatenate([jnp.where(even, k, zero), jnp.where(even, zero, k)], axis=0)
            v2 = jnp.concatenate([jnp.where(even, v, zero), jnp.where(even, zero, v)], axis=0)
            scores = (_dot_nt(q, k2) * dm_ref[s]).astype(BF16)
            qf = (q.astype(F32) * fs_ref[:, sl]).astype(BF16)
            o_ref[rows, sl] = _dot(scores, v2) + _dot(qf, st.astype(BF16))
            kt = (k.astype(F32) * te_ref[:, sl]).astype(BF16)
            st = cd_ref[s] * st + jnp.where(same_head, _dot_tn(kt, v), 0.0)
        state_ref[s] = st


def _ret_tables(log_g, seed_len, seeded):
    pos = jnp.arange(CHUNK, dtype=F32)
    diff = pos[:, None] - pos[None, :]
    lg = log_g[:, :, None, None]
    dm_f = jnp.where(diff >= 0, jnp.exp(jnp.maximum(diff, 0.0)[None, None] * lg), 0.0)[0]
    dm_b = jnp.where(diff <= 0, jnp.exp(jnp.maximum(-diff, 0.0)[None, None] * lg), 0.0)[1]
    dm = jnp.stack([dm_f, dm_b])
    dm = dm.reshape(2, RET_HEADS // 2, 2, CHUNK, CHUNK).transpose(0, 1, 3, 2, 4)
    dm = dm.reshape(2, RET_HEADS // 2, CHUNK, 2 * CHUNK)
    lane = lambda t: jnp.repeat(t, HEAD_DIM, axis=-1)
    te = jnp.stack([jnp.exp((CHUNK - 1 - pos)[:, None] * log_g[0][None, :]),
                    jnp.exp(pos[:, None] * log_g[1][None, :])])
    fs = jnp.stack([jnp.exp((pos + 1.0)[:, None] * log_g[0][None, :]),
                    jnp.exp((CHUNK - pos)[:, None] * log_g[1][None, :])])
    cd = jnp.broadcast_to(lane(jnp.exp(CHUNK * log_g)).reshape(2, RET_HEADS // 2, 128, 1),
                          (2, RET_HEADS // 2, 128, 128))
    sp = jnp.arange(seed_len, dtype=F32)
    ws = jnp.stack([jnp.exp((seed_len - 1 - sp)[:, None] * log_g[0][None, :]),
                    jnp.exp(sp[:, None] * log_g[1][None, :])])
    if not seeded:
        ws = jnp.zeros_like(ws)
    return dm, lane(te), lane(fs), cd, lane(ws)


def _retention(q, k, v, k_seed, v_seed, tables, tb):
    b, l, w = q.shape
    ls = k_seed.shape[1]
    nb = l // tb
    dm, te, fs, cd, ws = tables
    blk = lambda i, d, n: (i, jnp.where(d == 0, n, nb - 1 - n), 0)
    tok = pl.BlockSpec((None, tb, w), blk)
    seed = pl.BlockSpec((None, ls, w), lambda i, d, n: (i, 0, 0))
    return pl.pallas_call(
        functools.partial(_ret_kernel, n_chunks=tb // CHUNK),
        out_shape=jax.ShapeDtypeStruct((2, b, l, w), F32),
        grid=(b, 2, nb),
        in_specs=[
            tok, tok, tok, seed, seed,
            pl.BlockSpec((None, ls, w), lambda i, d, n: (d, 0, 0)),
            pl.BlockSpec((None, RET_HEADS // 2, CHUNK, 2 * CHUNK), lambda i, d, n: (d, 0, 0, 0)),
            pl.BlockSpec((None, CHUNK, w), lambda i, d, n: (d, 0, 0)),
            pl.BlockSpec((None, CHUNK, w), lambda i, d, n: (d, 0, 0)),
            pl.BlockSpec((None, RET_HEADS // 2, 128, 128), lambda i, d, n: (d, 0, 0, 0)),
        ],
        out_specs=pl.BlockSpec((None, None, tb, w), lambda i, d, n: (d,) + blk(i, d, n)),
        scratch_shapes=[pltpu.VMEM((RET_HEADS // 2, 128, 128), F32)],
        compiler_params=_params("parallel", "arbitrary", "arbitrary"),
        name="retention",
    )(q, k, v, k_seed, v_seed, ws, dm, te, fs, cd)


def _head_group_norm(o, ones_blk):
    mu = _seg_sum(o, ones_blk) * (1.0 / HEAD_DIM)
    dlt = o - mu
    var = _seg_sum(dlt * dlt, ones_blk) * (1.0 / HEAD_DIM)
    return dlt * lax.rsqrt(var + LN_EPS)


def _mix_out_kernel(x_ref, mod_ref, na_ref, sg_ref, of_ref, ob_ref, gf_ref, gb_ref, gnw_ref,
                    w_ref, e384_ref, o_ref):
    e384 = e384_ref[...]
    y = (_head_group_norm(of_ref[...], e384) * gf_ref[...]
         + _head_group_norm(ob_ref[...], e384) * gb_ref[...])
    ret = (y * gnw_ref[...]).astype(BF16)
    o_sg = NA_WIDTH + SG_WIDTH
    out = (_dot(na_ref[...], w_ref[0:NA_WIDTH, :])
           + _dot(sg_ref[...], w_ref[NA_WIDTH:o_sg, :])
           + _dot(ret, w_ref[o_sg:, :]))
    o_ref[...] = x_ref[...] + mod_ref[5:6, :] * out


def _mix_out(x, mod, na, sg, o_ret, gf, gb, gnw, w_out, e384, tm):
    b, l, d = x.shape
    mod_map = (lambda i, j: (i, 0, 0)) if mod.shape[0] == b else (lambda i, j: (0, 0, 0))
    tok = lambda w: pl.BlockSpec((None, tm, w), lambda i, j: (i, j, 0))
    dirn = lambda dd: pl.BlockSpec((None, None, tm, RET_WIDTH), lambda i, j: (dd, i, j, 0))
    const = lambda shape: _resident(shape, lambda i, j: (0,) * len(shape))
    return pl.pallas_call(
        _mix_out_kernel,
        out_shape=jax.ShapeDtypeStruct(x.shape, x.dtype),
        grid=(b, l // tm),
        in_specs=[
            tok(d),
            pl.BlockSpec((None, N_MOD, d), mod_map),
            tok(NA_WIDTH), tok(SG_WIDTH), dirn(0), dirn(1), tok(RET_WIDTH), tok(RET_WIDTH),
            const((1, RET_WIDTH)), const((d, d)), const((NA_WIDTH, NA_WIDTH)),
        ],
        out_specs=tok(d),
        compiler_params=_params("parallel", "parallel"),
        name="mix_out",
    )(x, mod, na, sg, o_ret, o_ret, gf, gb, gnw, w_out, e384)


def _ones_blocks(width):
    seg = np.arange(width) // HEAD_DIM
    return jnp.asarray(seg[:, None] == seg[None, :], dtype=BF16)


def _rope_tables(n_lat):
    t = jnp.arange(n_lat)
    inv = ROPE_BASE ** (-jnp.arange(ROPE_FREQS, dtype=F32) / ROPE_FREQS)
    ang_r = (t // GRID_W).astype(F32)[:, None] * inv[None, :]
    ang_c = (t % GRID_W).astype(F32)[:, None] * inv[None, :]
    ang = jnp.concatenate([ang_r, ang_r, ang_c, ang_c] * 2, axis=1)
    first = (np.arange(128) // ROPE_FREQS) % 2 == 0
    sin = jnp.sin(ang)
    return jnp.cos(ang), jnp.where(first, -sin, 0.0), jnp.where(first, 0.0, sin)


def kernel(x, c, ctx, c_ctx, ada_w, ada_b, norm_w, ffn_w1, ffn_w2, mix_w_in, mix_w_out, na_q_norm,
           na_k_norm, na_rpb, sg_w, sg_b, sg_ln_w, sg_ln_b, ret_decay_logit, ret_gn_w):
    b, n_lat, d = x.shape
    lc = ctx.shape[1]
    depth = ada_w.shape[0]
    rows = n_lat // GRID_W
    assert d == D_MODEL and n_lat % (GRID_W * NA_BLOCK_ROWS) == 0 and rows >= 2 * NA_BAND_ROWS - NA_WIN_R
    assert lc % CHUNK == 0
    tm_x = min(512, n_lat)
    tm_c = min(256, lc)
    assert n_lat % tm_x == 0 and lc % tm_c == 0 and tm_x % CHUNK == 0 and tm_c % CHUNK == 0

    cc = jnp.zeros((16, d), F32).at[:b].set(c).at[b].set(c_ctx)
    mod = _adaln(cc, ada_w, ada_b)

    e384, e256 = _ones_blocks(NA_WIDTH), _ones_blocks(SG_WIDTH)
    rope_x = _rope_tables(n_lat)
    rope_c = (jnp.ones((lc, 128), F32), jnp.zeros((lc, 128), F32), jnp.zeros((lc, 128), F32))
    w1 = ffn_w1.astype(BF16)
    w2 = ffn_w2.astype(BF16)
    w_in = mix_w_in.astype(BF16)
    w_out = mix_w_out.astype(BF16)
    sgw = sg_w.astype(BF16)
    log_g = jax.nn.log_sigmoid(ret_decay_logit.astype(F32))

    for layer in range(depth):
        last = layer == depth - 1
        mod_x = mod[layer, :b].reshape(b, N_MOD, d)
        mod_c = mod[layer, b:b + 1].reshape(1, N_MOD, d)
        nw = norm_w[layer]
        x = _ffn(x, mod_x, 0, nw[0], w1[layer, 0], w2[layer, 0], tm_x)
        ctx = _ffn(ctx, mod_c, 0, nw[0], w1[layer, 0], w2[layer, 0], tm_c)

        qn = jnp.tile(na_q_norm[layer], NA_HEADS).reshape(1, NA_WIDTH)
        kn = jnp.tile(na_k_norm[layer], NA_HEADS).reshape(1, NA_WIDTH)
        sgb_tab = jnp.repeat(sg_b[layer].T, HEAD_DIM, axis=1)
        lnw = sg_ln_w[layer].reshape(1, SG_WIDTH)
        lnb = sg_ln_b[layer].reshape(1, SG_WIDTH)
        gnw = ret_gn_w[layer].reshape(1, RET_WIDTH)
        px = _mix_in(x, mod_x, nw[1], w_in[layer], qn, kn, rope_x, sgw[layer], sgb_tab, lnw, lnb,
                     e384, e256, tm_x)
        pc = _mix_in(ctx, mod_c, nw[1], w_in[layer], qn, kn, rope_c, sgw[layer], sgb_tab, lnw, lnb,
                     e384, e256, tm_c)
        nq_x, nk_x, nv_x, sg_x, rq_x, rk_x, rv_x, gf_x, gb_x = px
        nq_c, nk_c, nv_c, sg_c, rq_c, rk_c, rv_c, gf_c, gb_c = pc

        na_x = _na(nq_x, nk_x, nv_x, nk_c, nv_c, _na_bias_table(na_rpb[layer], rows), rows)
        o_x = _retention(rq_x, rk_x, rv_x, rk_c, rv_c, _ret_tables(log_g[layer], lc, True), tm_x)
        x = _mix_out(x, mod_x, na_x, sg_x, o_x, gf_x, gb_x, gnw, w_out[layer], e384, tm_x)
        x = _ffn(x, mod_x, 6, nw[2], w1[layer, 1], w2[layer, 1], tm_x)
        if not last:
            na_c = _ctx_attn(nq_c, nk_c, nv_c)
            o_c = _retention(rq_c, rk_c, rv_c, rk_c, rv_c, _ret_tables(log_g[layer], lc, False), tm_c)
            ctx = _mix_out(ctx, mod_c, na_c, sg_c, o_c, gf_c, gb_c, gnw, w_out[layer], e384, tm_c)
            ctx = _ffn(ctx, mod_c, 6, nw[2], w1[layer, 1], w2[layer, 1], tm_c)
    return x
```

```python
import functools

import numpy as np
import jax
import jax.numpy as jnp
from jax import lax
from jax.experimental import pallas as pl
from jax.experimental.pallas import tpu as pltpu

D_MODEL = 1024
GRID_W = 64
N_MOD = 9
D_FF = 2816
NA_HEADS = 6
HEAD_DIM = 64
NA_WIDTH = NA_HEADS * HEAD_DIM
NA_WIN_R = 8
NA_WIN_C = 16
SG_GROUPS = 4
SG_WIDTH = SG_GROUPS * HEAD_DIM
CHUNK = 128
RET_HEADS = 6
RET_WIDTH = RET_HEADS * HEAD_DIM
IN_COLS = 3 * NA_WIDTH + 2 * SG_WIDTH + 5 * RET_WIDTH
ROPE_BASE = 10000.0
ROPE_FREQS = HEAD_DIM // 4
RMS_EPS = 1e-6
LN_EPS = 1e-5
NEG_INF = -1e30
LOG2E = 1.4426950408889634

OFF_NQ, OFF_NK, OFF_NV = 0, NA_WIDTH, 2 * NA_WIDTH
OFF_SU = 3 * NA_WIDTH
OFF_SV = OFF_SU + SG_WIDTH
OFF_RQ = OFF_SV + SG_WIDTH
OFF_RK = OFF_RQ + RET_WIDTH
OFF_RV = OFF_RK + RET_WIDTH
OFF_GF = OFF_RV + RET_WIDTH
OFF_GB = OFF_GF + RET_WIDTH

V7X_VMEM_LIMIT_BYTES = 56 * 1024 * 1024
NA_BLOCK_ROWS = 4
NA_BAND_ROWS = NA_BLOCK_ROWS + NA_WIN_R
FFN_CHUNKS = ((0, 1536), (1536, 1280))

BF16 = jnp.bfloat16
F32 = jnp.float32


def _params(*sem):
    return pltpu.CompilerParams(dimension_semantics=sem, vmem_limit_bytes=V7X_VMEM_LIMIT_BYTES)


def _resident(shape, index_map):
    return pl.BlockSpec(shape, index_map, pipeline_mode=pl.Buffered(1))


def _dot(a, b):
    return jnp.dot(a, b, preferred_element_type=F32)


def _dot_nt(a, b):
    return lax.dot_general(a, b, (((1,), (1,)), ((), ())), preferred_element_type=F32)


def _dot_tn(a, b):
    return lax.dot_general(a, b, (((0,), (0,)), ((), ())), preferred_element_type=F32)


def _seg_ones():
    r = lax.broadcasted_iota(jnp.int32, (256, 128), 0)
    c = lax.broadcasted_iota(jnp.int32, (256, 128), 1)
    return jnp.where(((r >> 6) & 1) == (c >> 6), 1.0, 0.0).astype(BF16)


def _seg_sum(t, seg_ones):
    hi = t.astype(BF16)
    lo = (t - hi.astype(F32)).astype(BF16)
    outs = []
    for s in range(t.shape[1] // 128):
        sl = slice(s * 128, (s + 1) * 128)
        outs.append(_dot(jnp.concatenate([hi[:, sl], lo[:, sl]], axis=1), seg_ones))
    return jnp.concatenate(outs, axis=1)


def _ada_rms(x, nw, shift, scale):
    ms = jnp.mean(x * x, axis=-1, keepdims=True)
    return (x * lax.rsqrt(ms + RMS_EPS)) * nw * (1.0 + scale) + shift


def _adaln_kernel(c_ref, w_ref, b_ref, o_ref):
    a = jax.nn.silu(c_ref[...])
    o_ref[...] = jnp.dot(a, w_ref[...], precision=lax.Precision.HIGHEST,
                         preferred_element_type=F32) + b_ref[...]


def _adaln(cc, ada_w, ada_b):
    depth = ada_w.shape[0]
    rows = cc.shape[0]
    tn = D_MODEL
    return pl.pallas_call(
        _adaln_kernel,
        out_shape=jax.ShapeDtypeStruct((depth, rows, N_MOD * D_MODEL), F32),
        grid=(depth, N_MOD * D_MODEL // tn),
        in_specs=[
            pl.BlockSpec((rows, D_MODEL), lambda l, j: (0, 0)),
            pl.BlockSpec((None, D_MODEL, tn), lambda l, j: (l, 0, j)),
            pl.BlockSpec((None, 1, tn), lambda l, j: (l, 0, j)),
        ],
        out_specs=pl.BlockSpec((None, rows, tn), lambda l, j: (l, 0, j)),
        compiler_params=_params("arbitrary", "arbitrary"),
        name="adaln",
    )(cc, ada_w, ada_b.reshape(depth, 1, N_MOD * D_MODEL))


def _ffn_kernel(x_ref, mod_ref, nw_ref, w1_ref, w2_ref, o_ref, *, mod_off):
    x = x_ref[...]
    shift = mod_ref[mod_off:mod_off + 1, :]
    scale = mod_ref[mod_off + 1:mod_off + 2, :]
    gate = mod_ref[mod_off + 2:mod_off + 3, :]
    xb = _ada_rms(x, nw_ref[...], shift, scale).astype(BF16)
    acc = None
    for off, width in FFN_CHUNKS:
        g = _dot(xb, w1_ref[:, off:off + width])
        u = _dot(xb, w1_ref[:, D_FF + off:D_FF + off + width])
        a = (jax.nn.silu(g) * u).astype(BF16)
        part = _dot(a, w2_ref[off:off + width, :])
        acc = part if acc is None else acc + part
    o_ref[...] = x + (0.5 * gate) * acc


def _ffn(x, mod, mod_off, nw, w1, w2, tm):
    b, l, d = x.shape
    mod_map = (lambda i, j: (i, 0, 0)) if mod.shape[0] == b else (lambda i, j: (0, 0, 0))
    return pl.pallas_call(
        functools.partial(_ffn_kernel, mod_off=mod_off),
        out_shape=jax.ShapeDtypeStruct(x.shape, x.dtype),
        grid=(b, l // tm),
        in_specs=[
            pl.BlockSpec((None, tm, d), lambda i, j: (i, j, 0)),
            pl.BlockSpec((None, N_MOD, d), mod_map),
            _resident((1, d), lambda i, j: (0, 0)),
            _resident((d, 2 * D_FF), lambda i, j: (0, 0)),
            _resident((D_FF, d), lambda i, j: (0, 0)),
        ],
        out_specs=pl.BlockSpec((None, tm, d), lambda i, j: (i, j, 0)),
        compiler_params=_params("parallel", "parallel"),
        name="ffn",
    )(x, mod, nw.reshape(1, d), w1, w2)


def _head_rms(t, w, ones_blk):
    ms = _seg_sum(t * t, ones_blk) * (1.0 / HEAD_DIM)
    return (t * lax.rsqrt(ms + RMS_EPS)) * w


def _rope(t, cos, sin_next, sin_prev):
    outs = []
    for s in range(RET_WIDTH // 128):
        ts = t[:, s * 128:(s + 1) * 128]
        nxt = pltpu.roll(ts, 128 - ROPE_FREQS, axis=1)
        prv = pltpu.roll(ts, ROPE_FREQS, axis=1)
        outs.append(ts * cos + nxt * sin_next + prv * sin_prev)
    return jnp.concatenate(outs, axis=1)


def _mix_in_kernel(x_ref, mod_ref, nw_ref, w_ref, qn_ref, kn_ref, cos_ref, sn_ref, sp_ref,
                   sgw_ref, sgb_ref, lnw_ref, lnb_ref,
                   nq_ref, nk_ref, nv_ref, sg_ref, rq_ref, rk_ref, rv_ref, gf_ref, gb_ref):
    tm = x_ref.shape[0]
    x = x_ref[...]
    hb = _ada_rms(x, nw_ref[...], mod_ref[3:4, :], mod_ref[4:5, :]).astype(BF16)
    p = _dot(hb, w_ref[...])
    seg_ones = _seg_ones()
    scale = HEAD_DIM ** -0.5

    nq_ref[...] = (_head_rms(p[:, OFF_NQ:OFF_NQ + NA_WIDTH], qn_ref[...], seg_ones)
                   * (scale * LOG2E)).astype(BF16)
    nk_ref[...] = _head_rms(p[:, OFF_NK:OFF_NK + NA_WIDTH], kn_ref[...], seg_ones).astype(BF16)
    nv_ref[...] = p[:, OFF_NV:OFF_NV + NA_WIDTH].astype(BF16)

    u = jax.nn.gelu(p[:, OFF_SU:OFF_SU + SG_WIDTH])
    v = jax.nn.gelu(p[:, OFF_SV:OFF_SV + SG_WIDTH])
    mu = _seg_sum(v, seg_ones) * (1.0 / HEAD_DIM)
    dv = v - mu
    var = _seg_sum(dv * dv, seg_ones) * (1.0 / HEAD_DIM)
    vn = ((dv * lax.rsqrt(var + LN_EPS)) * lnw_ref[...] + lnb_ref[...]).astype(BF16)
    lane_grp = lax.broadcasted_iota(jnp.int32, (CHUNK, SG_WIDTH), 1) >> 6
    for j in range(tm // CHUNK):
        vch = vn[j * CHUNK:(j + 1) * CHUNK, :]
        stacked = jnp.concatenate(
            [jnp.where(lane_grp == g, vch, jnp.zeros_like(vch)) for g in range(SG_GROUPS)], axis=0)
        mixed = _dot(sgw_ref[...], stacked) + sgb_ref[...]
        sg_ref[j * CHUNK:(j + 1) * CHUNK, :] = (u[j * CHUNK:(j + 1) * CHUNK, :] * mixed).astype(BF16)

    cos, sn, sp = cos_ref[...], sn_ref[...], sp_ref[...]
    rq_ref[...] = _rope(p[:, OFF_RQ:OFF_RQ + RET_WIDTH], cos, sn, sp).astype(BF16)
    rk_ref[...] = (_rope(p[:, OFF_RK:OFF_RK + RET_WIDTH], cos, sn, sp) * scale).astype(BF16)
    rv_ref[...] = p[:, OFF_RV:OFF_RV + RET_WIDTH].astype(BF16)
    gf_ref[...] = jax.nn.silu(p[:, OFF_GF:OFF_GF + RET_WIDTH])
    gb_ref[...] = jax.nn.silu(p[:, OFF_GB:OFF_GB + RET_WIDTH])


def _mix_in(x, mod, nw, w_in, qn, kn, rope_tabs, sgw, sgb_tab, lnw, lnb, tm):
    b, l, d = x.shape
    mod_map = (lambda i, j: (i, 0, 0)) if mod.shape[0] == b else (lambda i, j: (0, 0, 0))
    tok = lambda w: pl.BlockSpec((None, tm, w), lambda i, j: (i, j, 0))
    tab = pl.BlockSpec((tm, 128), lambda i, j: (j, 0))
    const = lambda shape: _resident(shape, lambda i, j: (0,) * len(shape))
    out_bf = lambda w: jax.ShapeDtypeStruct((b, l, w), BF16)
    out_f32 = lambda w: jax.ShapeDtypeStruct((b, l, w), F32)
    return pl.pallas_call(
        _mix_in_kernel,
        out_shape=(out_bf(NA_WIDTH), out_bf(NA_WIDTH), out_bf(NA_WIDTH), out_bf(SG_WIDTH),
                   out_bf(RET_WIDTH), out_bf(RET_WIDTH), out_bf(RET_WIDTH),
                   out_f32(RET_WIDTH), out_f32(RET_WIDTH)),
        grid=(b, l // tm),
        in_specs=[
            tok(d),
            pl.BlockSpec((None, N_MOD, d), mod_map),
            const((1, d)),
            const((d, IN_COLS)),
            const((1, NA_WIDTH)), const((1, NA_WIDTH)),
            tab, tab, tab,
            const((CHUNK, SG_GROUPS * CHUNK)), const((CHUNK, SG_WIDTH)),
            const((1, SG_WIDTH)), const((1, SG_WIDTH)),
        ],
        out_specs=(tok(NA_WIDTH), tok(NA_WIDTH), tok(NA_WIDTH), tok(SG_WIDTH),
                   tok(RET_WIDTH), tok(RET_WIDTH), tok(RET_WIDTH), tok(RET_WIDTH), tok(RET_WIDTH)),
        compiler_params=_params("parallel", "parallel"),
        name="mix_in",
    )(x, mod, nw.reshape(1, d), w_in, qn, kn, *rope_tabs, sgw, sgb_tab, lnw, lnb)


def _attend_head_pair(q, keys, vals, biases):
    m = q.shape[0]
    even_q = lax.broadcasted_iota(jnp.int32, (m, 128), 1) < HEAD_DIM
    nums = []
    for parity in range(2):
        own = even_q if parity == 0 else jnp.logical_not(even_q)
        qp = jnp.where(own, q, jnp.zeros_like(q))
        scores = []
        for kk, bias in zip(keys, biases):
            s = _dot_nt(qp, kk)
            scores.append(s if bias is None else s + bias(parity))
        mx = functools.reduce(jnp.maximum, [jnp.max(s, axis=-1, keepdims=True) for s in scores])
        num = None
        for s, vv in zip(scores, vals):
            even_v = lax.broadcasted_iota(jnp.int32, vv.shape, 1) < HEAD_DIM
            own_v = even_v if parity == 0 else jnp.logical_not(even_v)
            part = _dot(jnp.exp2(s - mx).astype(BF16), jnp.where(own_v, vv, jnp.ones_like(vv)))
            num = part if num is None else num + part
        nums.append(num)
    numer = jnp.where(even_q, nums[0], nums[1])
    den = pltpu.roll(jnp.where(even_q, nums[1], nums[0]), HEAD_DIM, axis=1)
    return numer / den


def _na_kernel(q_ref, k_ref, v_ref, kc_ref, vc_ref, bias_ref, o_ref, *, rows):
    i = pl.program_id(1)
    band_row = jnp.clip(i * NA_BLOCK_ROWS - NA_WIN_R // 2, 0, rows - NA_BAND_ROWS)
    start = pl.multiple_of(band_row * GRID_W, GRID_W)
    band = pl.ds(start, NA_BAND_ROWS * GRID_W)
    for s in range(NA_WIDTH // 128):
        sl = slice(s * 128, (s + 1) * 128)
        out = _attend_head_pair(
            q_ref[:, sl], [k_ref[band, sl], kc_ref[:, sl]], [v_ref[band, sl], vc_ref[:, sl]],
            [lambda parity, s=s: bias_ref[2 * s + parity], None])
        o_ref[:, sl] = out.astype(BF16)


def _na_bias_table(rpb, rows):
    r, band = NA_BLOCK_ROWS, NA_BAND_ROWS
    n_blocks = rows // r
    qc = np.arange(GRID_W)[:, None]
    kcol = np.arange(GRID_W)[None, :]
    c_start = np.clip(qc - NA_WIN_C // 2, 0, GRID_W - NA_WIN_C)
    col_ok = (kcol >= c_start) & (kcol < c_start + NA_WIN_C)
    col_sel = ((kcol - qc + NA_WIN_C - 1)[:, :, None] == np.arange(2 * NA_WIN_C - 1)) & col_ok[:, :, None]
    row_sel = np.zeros((3, r, band, 2 * NA_WIN_R - 1), np.float32)
    for vi, blk in enumerate((0, 1, n_blocks - 1)):
        r0 = blk * r
        b0 = int(np.clip(r0 - NA_WIN_R // 2, 0, rows - band))
        for qi in range(r):
            r_start = int(np.clip(r0 + qi - NA_WIN_R // 2, 0, rows - NA_WIN_R))
            for ki in range(band):
                if r_start <= b0 + ki < r_start + NA_WIN_R:
                    row_sel[vi, qi, ki, b0 + ki - (r0 + qi) + NA_WIN_R - 1] = 1.0
    ok = (row_sel.sum(-1) > 0)[:, None, :, None, :, None] & col_ok[None, None, None, :, None, :]
    hp = lax.Precision.HIGHEST
    t_col = jnp.einsum('hij,qkj->hiqk', rpb, jnp.asarray(col_sel, F32), precision=hp)
    tab = jnp.einsum('hiqk,vrsi->vhrqsk', t_col, jnp.asarray(row_sel), precision=hp)
    tab = jnp.where(ok, tab * LOG2E, NEG_INF)
    return tab.reshape(3, NA_HEADS, r * GRID_W, band * GRID_W)


def _na(q, k, v, kc, vc, bias_tab, rows):
    b, s, w = q.shape
    lc = kc.shape[1]
    r = NA_BLOCK_ROWS
    n_blocks = rows // r
    tq, tk = r * GRID_W, NA_BAND_ROWS * GRID_W

    def variant(i, j):
        return (jnp.where(j == 0, 0, jnp.where(j == n_blocks - 1, 2, 1)), 0, 0, 0)

    return pl.pallas_call(
        functools.partial(_na_kernel, rows=rows),
        out_shape=jax.ShapeDtypeStruct((b, s, w), BF16),
        grid=(b, n_blocks),
        in_specs=[
            pl.BlockSpec((None, tq, w), lambda i, j: (i, j, 0)),
            pl.BlockSpec((None, s, w), lambda i, j: (i, 0, 0)),
            pl.BlockSpec((None, s, w), lambda i, j: (i, 0, 0)),
            pl.BlockSpec((None, lc, w), lambda i, j: (i, 0, 0)),
            pl.BlockSpec((None, lc, w), lambda i, j: (i, 0, 0)),
            pl.BlockSpec((None, NA_HEADS, tq, tk), variant),
        ],
        out_specs=pl.BlockSpec((None, tq, w), lambda i, j: (i, j, 0)),
        compiler_params=_params("parallel", "arbitrary"),
        name="na",
    )(q, k, v, kc, vc, bias_tab)


def _ctx_attn_kernel(q_ref, k_ref, v_ref, o_ref):
    for s in range(NA_WIDTH // 128):
        sl = slice(s * 128, (s + 1) * 128)
        o_ref[:, sl] = _attend_head_pair(q_ref[:, sl], [k_ref[:, sl]], [v_ref[:, sl]], [None]).astype(BF16)


def _ctx_attn(q, k, v):
    b, lc, w = q.shape
    spec = pl.BlockSpec((None, lc, w), lambda i: (i, 0, 0))
    return pl.pallas_call(
        _ctx_attn_kernel,
        out_shape=jax.ShapeDtypeStruct((b, lc, w), BF16),
        grid=(b,),
        in_specs=[spec, spec, spec],
        out_specs=spec,
        compiler_params=_params("parallel"),
        name="ctx_attn",
    )(q, k, v)


def _ret_kernel(qf_ref, kf_ref, vf_ref, qb_ref, kb_ref, vb_ref, ks_ref, vs_ref, ws_ref, dm_ref, te_ref,
                fs_ref, cd_ref, of_ref, ob_ref, state_ref, *, n_chunks):
    n = pl.program_id(1)
    same_head = ((lax.broadcasted_iota(jnp.int32, (128, 128), 0) >> 6)
                 == (lax.broadcasted_iota(jnp.int32, (128, 128), 1) >> 6))
    even = lax.broadcasted_iota(jnp.int32, (CHUNK, 128), 1) < HEAD_DIM

    @pl.when(n == 0)
    def _seed():
        for d in range(2):
            for s in range(RET_WIDTH // 128):
                sl = slice(s * 128, (s + 1) * 128)
                kw = (ks_ref[:, sl].astype(F32) * ws_ref[d, :, sl]).astype(BF16)
                state_ref[d, s] = jnp.where(same_head, _dot_tn(kw, vs_ref[:, sl]), 0.0)

    streams = ((qf_ref, kf_ref, vf_ref, of_ref), (qb_ref, kb_ref, vb_ref, ob_ref))
    for s in range(RET_WIDTH // 128):
        sl = slice(s * 128, (s + 1) * 128)
        for d, (q_ref, k_ref, v_ref, o_ref) in enumerate(streams):
            st = state_ref[d, s]
            for j in range(n_chunks):
                c = j if d == 0 else n_chunks - 1 - j
                rows = slice(c * CHUNK, (c + 1) * CHUNK)
                q = q_ref[rows, sl]
                k = k_ref[rows, sl]
                v = v_ref[rows, sl]
                zero = jnp.zeros_like(k)
                k2 = jnp.concatenate([jnp.where(even, k, zero), jnp.where(even, zero, k)], axis=0)
                v2 = jnp.concatenate([jnp.where(even, v, zero), jnp.where(even, zero, v)], axis=0)
                scores = (_dot_nt(q, k2) * dm_ref[d, s]).astype(BF16)
                qf = (q.astype(F32) * fs_ref[d, :, sl]).astype(BF16)
                o_ref[rows, sl] = _dot(scores, v2) + _dot(qf, st.astype(BF16))
                kt = (k.astype(F32) * te_ref[d, :, sl]).astype(BF16)
                st = cd_ref[d, s] * st + jnp.where(same_head, _dot_tn(kt, v), 0.0)
            state_ref[d, s] = st


def _ret_tables(log_g, seed_len, seeded):
    pos = jnp.arange(CHUNK, dtype=F32)
    diff = pos[:, None] - pos[None, :]
    lg = log_g[:, :, None, None]
    dm_f = jnp.where(diff >= 0, jnp.exp(jnp.maximum(diff, 0.0)[None, None] * lg), 0.0)[0]
    dm_b = jnp.where(diff <= 0, jnp.exp(jnp.maximum(-diff, 0.0)[None, None] * lg), 0.0)[1]
    dm = jnp.stack([dm_f, dm_b])
    dm = dm.reshape(2, RET_HEADS // 2, 2, CHUNK, CHUNK).transpose(0, 1, 3, 2, 4)
    dm = dm.reshape(2, RET_HEADS // 2, CHUNK, 2 * CHUNK)
    lane = lambda t: jnp.repeat(t, HEAD_DIM, axis=-1)
    te = jnp.stack([jnp.exp((CHUNK - 1 - pos)[:, None] * log_g[0][None, :]),
                    jnp.exp(pos[:, None] * log_g[1][None, :])])
    fs = jnp.stack([jnp.exp((pos + 1.0)[:, None] * log_g[0][None, :]),
                    jnp.exp((CHUNK - pos)[:, None] * log_g[1][None, :])])
    cd = jnp.broadcast_to(lane(jnp.exp(CHUNK * log_g)).reshape(2, RET_HEADS // 2, 128, 1),
                          (2, RET_HEADS // 2, 128, 128))
    sp = jnp.arange(seed_len, dtype=F32)
    ws = jnp.stack([jnp.exp((seed_len - 1 - sp)[:, None] * log_g[0][None, :]),
                    jnp.exp(sp[:, None] * log_g[1][None, :])])
    if not seeded:
        ws = jnp.zeros_like(ws)
    return dm, lane(te), lane(fs), cd, lane(ws)


def _retention(q, k, v, k_seed, v_seed, tables, tb):
    b, l, w = q.shape
    ls = k_seed.shape[1]
    nb = l // tb
    dm, te, fs, cd, ws = tables
    fwd = pl.BlockSpec((None, tb, w), lambda i, n: (i, n, 0))
    bwd = pl.BlockSpec((None, tb, w), lambda i, n: (i, nb - 1 - n, 0))
    seed = pl.BlockSpec((None, ls, w), lambda i, n: (i, 0, 0))
    const = lambda shape: _resident(shape, lambda i, n: (0,) * len(shape))
    out = jax.ShapeDtypeStruct((b, l, w), F32)
    return pl.pallas_call(
        functools.partial(_ret_kernel, n_chunks=tb // CHUNK),
        out_shape=(out, out),
        grid=(b, nb),
        in_specs=[
            fwd, fwd, fwd, bwd, bwd, bwd, seed, seed,
            const(ws.shape), const(dm.shape), const(te.shape), const(fs.shape), const(cd.shape),
        ],
        out_specs=(fwd, bwd),
        scratch_shapes=[pltpu.VMEM((2, RET_HEADS // 2, 128, 128), F32)],
        compiler_params=_params("parallel", "arbitrary"),
        name="retention",
    )(q, k, v, q, k, v, k_seed, v_seed, ws, dm, te, fs, cd)


def _head_group_norm(o, ones_blk):
    mu = _seg_sum(o, ones_blk) * (1.0 / HEAD_DIM)
    dlt = o - mu
    var = _seg_sum(dlt * dlt, ones_blk) * (1.0 / HEAD_DIM)
    return dlt * lax.rsqrt(var + LN_EPS)


def _mix_out_kernel(x_ref, mod_ref, na_ref, sg_ref, of_ref, ob_ref, gf_ref, gb_ref, gnw_ref,
                    w_ref, o_ref):
    seg_ones = _seg_ones()
    y = (_head_group_norm(of_ref[...], seg_ones) * gf_ref[...]
         + _head_group_norm(ob_ref[...], seg_ones) * gb_ref[...])
    ret = (y * gnw_ref[...]).astype(BF16)
    o_sg = NA_WIDTH + SG_WIDTH
    out = (_dot(na_ref[...], w_ref[0:NA_WIDTH, :])
           + _dot(sg_ref[...], w_ref[NA_WIDTH:o_sg, :])
           + _dot(ret, w_ref[o_sg:, :]))
    o_ref[...] = x_ref[...] + mod_ref[5:6, :] * out


def _mix_out(x, mod, na, sg, o_fwd, o_bwd, gf, gb, gnw, w_out, tm):
    b, l, d = x.shape
    mod_map = (lambda i, j: (i, 0, 0)) if mod.shape[0] == b else (lambda i, j: (0, 0, 0))
    tok = lambda w: pl.BlockSpec((None, tm, w), lambda i, j: (i, j, 0))
    const = lambda shape: _resident(shape, lambda i, j: (0,) * len(shape))
    return pl.pallas_call(
        _mix_out_kernel,
        out_shape=jax.ShapeDtypeStruct(x.shape, x.dtype),
        grid=(b, l // tm),
        in_specs=[
            tok(d),
            pl.BlockSpec((None, N_MOD, d), mod_map),
            tok(NA_WIDTH), tok(SG_WIDTH), tok(RET_WIDTH), tok(RET_WIDTH), tok(RET_WIDTH), tok(RET_WIDTH),
            const((1, RET_WIDTH)), const((d, d)),
        ],
        out_specs=tok(d),
        compiler_params=_params("parallel", "parallel"),
        name="mix_out",
    )(x, mod, na, sg, o_fwd, o_bwd, gf, gb, gnw, w_out)


def _rope_tables(n_lat):
    t = jnp.arange(n_lat)
    inv = ROPE_BASE ** (-jnp.arange(ROPE_FREQS, dtype=F32) / ROPE_FREQS)
    ang_r = (t // GRID_W).astype(F32)[:, None] * inv[None, :]
    ang_c = (t % GRID_W).astype(F32)[:, None] * inv[None, :]
    ang = jnp.concatenate([ang_r, ang_r, ang_c, ang_c] * 2, axis=1)
    first = (np.arange(128) // ROPE_FREQS) % 2 == 0
    sin = jnp.sin(ang)
    return jnp.cos(ang), jnp.where(first, -sin, 0.0), jnp.where(first, 0.0, sin)


def kernel(x, c, ctx, c_ctx, ada_w, ada_b, norm_w, ffn_w1, ffn_w2, mix_w_in, mix_w_out, na_q_norm,
           na_k_norm, na_rpb, sg_w, sg_b, sg_ln_w, sg_ln_b, ret_decay_logit, ret_gn_w):
    b, n_lat, d = x.shape
    lc = ctx.shape[1]
    depth = ada_w.shape[0]
    rows = n_lat // GRID_W
    assert d == D_MODEL and n_lat % (GRID_W * NA_BLOCK_ROWS) == 0 and rows >= 2 * NA_BAND_ROWS - NA_WIN_R
    assert lc % CHUNK == 0
    tm_x = min(512, n_lat)
    tm_c = min(256, lc)
    tb_x = min(1024, n_lat)
    assert n_lat % tb_x == 0
    assert n_lat % tm_x == 0 and lc % tm_c == 0 and tm_x % CHUNK == 0 and tm_c % CHUNK == 0

    cc = jnp.zeros((16, d), F32).at[:b].set(c).at[b].set(c_ctx)
    mod = _adaln(cc, ada_w, ada_b)

    rope_x = _rope_tables(n_lat)
    rope_c = (jnp.ones((lc, 128), F32), jnp.zeros((lc, 128), F32), jnp.zeros((lc, 128), F32))
    w1 = ffn_w1.astype(BF16)
    w2 = ffn_w2.astype(BF16)
    w_in = mix_w_in.astype(BF16)
    w_out = mix_w_out.astype(BF16)
    sgw = sg_w.astype(BF16).transpose(0, 2, 1, 3).reshape(depth, CHUNK, SG_GROUPS * CHUNK)
    log_g = jax.nn.log_sigmoid(ret_decay_logit.astype(F32))

    for layer in range(depth):
        last = layer == depth - 1
        mod_x = mod[layer, :b].reshape(b, N_MOD, d)
        mod_c = mod[layer, b:b + 1].reshape(1, N_MOD, d)
        nw = norm_w[layer]
        x = _ffn(x, mod_x, 0, nw[0], w1[layer, 0], w2[layer, 0], tm_x)
        ctx = _ffn(ctx, mod_c, 0, nw[0], w1[layer, 0], w2[layer, 0], tm_c)

        qn = jnp.tile(na_q_norm[layer], NA_HEADS).reshape(1, NA_WIDTH)
        kn = jnp.tile(na_k_norm[layer], NA_HEADS).reshape(1, NA_WIDTH)
        sgb_tab = jnp.repeat(sg_b[layer].T, HEAD_DIM, axis=1)
        lnw = sg_ln_w[layer].reshape(1, SG_WIDTH)
        lnb = sg_ln_b[layer].reshape(1, SG_WIDTH)
        gnw = ret_gn_w[layer].reshape(1, RET_WIDTH)
        px = _mix_in(x, mod_x, nw[1], w_in[layer], qn, kn, rope_x, sgw[layer], sgb_tab, lnw, lnb,
                     tm_x)
        pc = _mix_in(ctx, mod_c, nw[1], w_in[layer], qn, kn, rope_c, sgw[layer], sgb_tab, lnw, lnb,
                     tm_c)
        nq_x, nk_x, nv_x, sg_x, rq_x, rk_x, rv_x, gf_x, gb_x = px
        nq_c, nk_c, nv_c, sg_c, rq_c, rk_c, rv_c, gf_c, gb_c = pc

        na_x = _na(nq_x, nk_x, nv_x, nk_c, nv_c, _na_bias_table(na_rpb[layer], rows), rows)
        of_x, ob_x = _retention(rq_x, rk_x, rv_x, rk_c, rv_c, _ret_tables(log_g[layer], lc, True), tb_x)
        x = _mix_out(x, mod_x, na_x, sg_x, of_x, ob_x, gf_x, gb_x, gnw, w_out[layer], tm_x)
        x = _ffn(x, mod_x, 6, nw[2], w1[layer, 1], w2[layer, 1], tm_x)
        if not last:
            na_c = _ctx_attn(nq_c, nk_c, nv_c)
            of_c, ob_c = _retention(rq_c, rk_c, rv_c, rk_c, rv_c, _ret_tables(log_g[layer], lc, False),
                                    tm_c)
            ctx = _mix_out(ctx, mod_c, na_c, sg_c, of_c, ob_c, gf_c, gb_c, gnw, w_out[layer], tm_c)
            ctx = _ffn(ctx, mod_c, 6, nw[2], w1[layer, 1], w2[layer, 1], tm_c)
    return x
```

```python
import functools

import numpy as np
import jax
import jax.numpy as jnp
from jax import lax
from jax.experimental import pallas as pl
from jax.experimental.pallas import tpu as pltpu

D_MODEL = 1024
GRID_W = 64
N_MOD = 9
D_FF = 2816
NA_HEADS = 6
HEAD_DIM = 64
NA_WIDTH = NA_HEADS * HEAD_DIM
NA_WIN_R = 8
NA_WIN_C = 16
SG_GROUPS = 4
SG_WIDTH = SG_GROUPS * HEAD_DIM
CHUNK = 128
RET_HEADS = 6
RET_WIDTH = RET_HEADS * HEAD_DIM
IN_COLS = 3 * NA_WIDTH + 2 * SG_WIDTH + 5 * RET_WIDTH
ROPE_BASE = 10000.0
ROPE_FREQS = HEAD_DIM // 4
RMS_EPS = 1e-6
LN_EPS = 1e-5
NEG_INF = -1e30
LOG2E = 1.4426950408889634

OFF_NQ, OFF_NK, OFF_NV = 0, NA_WIDTH, 2 * NA_WIDTH
OFF_SU = 3 * NA_WIDTH
OFF_SV = OFF_SU + SG_WIDTH
OFF_RQ = OFF_SV + SG_WIDTH
OFF_RK = OFF_RQ + RET_WIDTH
OFF_RV = OFF_RK + RET_WIDTH
OFF_GF = OFF_RV + RET_WIDTH
OFF_GB = OFF_GF + RET_WIDTH

V7X_VMEM_LIMIT_BYTES = 56 * 1024 * 1024
NA_BLOCK_ROWS = 4
NA_BAND_ROWS = NA_BLOCK_ROWS + NA_WIN_R
NA_LOGITS_AHEAD = 2
NA_BLOCKS_PER_STEP = 2
FFN_CHUNKS = ((0, 1536), (1536, 1280))

BF16 = jnp.bfloat16
F32 = jnp.float32


def _params(*sem):
    return pltpu.CompilerParams(dimension_semantics=sem, vmem_limit_bytes=V7X_VMEM_LIMIT_BYTES)


def _resident(shape, index_map):
    return pl.BlockSpec(shape, index_map, pipeline_mode=pl.Buffered(1))


def _dot(a, b):
    return jnp.dot(a, b, preferred_element_type=F32)


def _dot_nt(a, b):
    return lax.dot_general(a, b, (((1,), (1,)), ((), ())), preferred_element_type=F32)


def _dot_tn(a, b):
    return lax.dot_general(a, b, (((0,), (0,)), ((), ())), preferred_element_type=F32)


def _seg_ones():
    r = lax.broadcasted_iota(jnp.int32, (256, 128), 0)
    c = lax.broadcasted_iota(jnp.int32, (256, 128), 1)
    return jnp.where(((r >> 6) & 1) == (c >> 6), 1.0, 0.0).astype(BF16)


def _seg_sum(t, seg_ones):
    hi = t.astype(BF16)
    lo = (t - hi.astype(F32)).astype(BF16)
    outs = []
    for s in range(t.shape[1] // 128):
        sl = slice(s * 128, (s + 1) * 128)
        outs.append(_dot(jnp.concatenate([hi[:, sl], lo[:, sl]], axis=1), seg_ones))
    return jnp.concatenate(outs, axis=1)


def _ada_rms(x, nw, shift, scale):
    ms = jnp.mean(x * x, axis=-1, keepdims=True)
    return (x * lax.rsqrt(ms + RMS_EPS)) * nw * (1.0 + scale) + shift


def _adaln_kernel(c_ref, w_ref, b_ref, o_ref):
    a = jax.nn.silu(c_ref[...])
    o_ref[...] = jnp.dot(a, w_ref[...], precision=lax.Precision.HIGHEST,
                         preferred_element_type=F32) + b_ref[...]


def _adaln(cc, ada_w, ada_b):
    depth = ada_w.shape[0]
    rows = cc.shape[0]
    tn = D_MODEL
    return pl.pallas_call(
        _adaln_kernel,
        out_shape=jax.ShapeDtypeStruct((depth, rows, N_MOD * D_MODEL), F32),
        grid=(depth, N_MOD * D_MODEL // tn),
        in_specs=[
            pl.BlockSpec((rows, D_MODEL), lambda l, j: (0, 0)),
            pl.BlockSpec((None, D_MODEL, tn), lambda l, j: (l, 0, j)),
            pl.BlockSpec((None, 1, tn), lambda l, j: (l, 0, j)),
        ],
        out_specs=pl.BlockSpec((None, rows, tn), lambda l, j: (l, 0, j)),
        compiler_params=_params("arbitrary", "arbitrary"),
        name="adaln",
    )(cc, ada_w, ada_b.reshape(depth, 1, N_MOD * D_MODEL))


def _ffn_half_step(x, mod_ref, nw_ref, w1_ref, w2_ref, mod_off):
    shift = mod_ref[mod_off:mod_off + 1, :]
    scale = mod_ref[mod_off + 1:mod_off + 2, :]
    gate = mod_ref[mod_off + 2:mod_off + 3, :]
    xb = _ada_rms(x, nw_ref[...], shift, scale).astype(BF16)
    acc = None
    for off, width in FFN_CHUNKS:
        g = _dot(xb, w1_ref[:, off:off + width])
        u = _dot(xb, w1_ref[:, D_FF + off:D_FF + off + width])
        a = (jax.nn.silu(g) * u).astype(BF16)
        part = _dot(a, w2_ref[off:off + width, :])
        acc = part if acc is None else acc + part
    return x + (0.5 * gate) * acc


def _ffn_kernel(x_ref, mod_ref, nw_ref, w1_ref, w2_ref, o_ref, *, mod_off):
    o_ref[...] = _ffn_half_step(x_ref[...], mod_ref, nw_ref, w1_ref, w2_ref, mod_off)


def _ffn(x, mod, mod_off, nw, w1, w2, tm):
    b, l, d = x.shape
    mod_map = (lambda i, j: (i, 0, 0)) if mod.shape[0] == b else (lambda i, j: (0, 0, 0))
    return pl.pallas_call(
        functools.partial(_ffn_kernel, mod_off=mod_off),
        out_shape=jax.ShapeDtypeStruct(x.shape, x.dtype),
        grid=(b, l // tm),
        in_specs=[
            pl.BlockSpec((None, tm, d), lambda i, j: (i, j, 0)),
            pl.BlockSpec((None, N_MOD, d), mod_map),
            _resident((1, d), lambda i, j: (0, 0)),
            _resident((d, 2 * D_FF), lambda i, j: (0, 0)),
            _resident((D_FF, d), lambda i, j: (0, 0)),
        ],
        out_specs=pl.BlockSpec((None, tm, d), lambda i, j: (i, j, 0)),
        compiler_params=_params("parallel", "parallel"),
        name="ffn",
    )(x, mod, nw.reshape(1, d), w1, w2)


def _head_rms(t, w, ones_blk):
    ms = _seg_sum(t * t, ones_blk) * (1.0 / HEAD_DIM)
    return (t * lax.rsqrt(ms + RMS_EPS)) * w


def _rope(t, cos, sin_next, sin_prev):
    outs = []
    for s in range(RET_WIDTH // 128):
        ts = t[:, s * 128:(s + 1) * 128]
        nxt = pltpu.roll(ts, 128 - ROPE_FREQS, axis=1)
        prv = pltpu.roll(ts, ROPE_FREQS, axis=1)
        outs.append(ts * cos + nxt * sin_next + prv * sin_prev)
    return jnp.concatenate(outs, axis=1)


def _mix_in_kernel(x_ref, mod_ref, nw_ref, w_ref, qn_ref, kn_ref, cos_ref, sn_ref, sp_ref,
                   sgw_ref, sgb_ref, lnw_ref, lnb_ref,
                   nq_ref, nk_ref, nv_ref, sg_ref, rq_ref, rk_ref, rv_ref, gf_ref, gb_ref):
    tm = x_ref.shape[0]
    x = x_ref[...]
    hb = _ada_rms(x, nw_ref[...], mod_ref[3:4, :], mod_ref[4:5, :]).astype(BF16)
    p = _dot(hb, w_ref[...])
    seg_ones = _seg_ones()
    scale = HEAD_DIM ** -0.5

    nq_ref[...] = (_head_rms(p[:, OFF_NQ:OFF_NQ + NA_WIDTH], qn_ref[...], seg_ones)
                   * (scale * LOG2E)).astype(BF16)
    nk_ref[...] = _head_rms(p[:, OFF_NK:OFF_NK + NA_WIDTH], kn_ref[...], seg_ones).astype(BF16)
    nv_ref[...] = p[:, OFF_NV:OFF_NV + NA_WIDTH].astype(BF16)

    u = jax.nn.gelu(p[:, OFF_SU:OFF_SU + SG_WIDTH])
    v = jax.nn.gelu(p[:, OFF_SV:OFF_SV + SG_WIDTH])
    mu = _seg_sum(v, seg_ones) * (1.0 / HEAD_DIM)
    dv = v - mu
    var = _seg_sum(dv * dv, seg_ones) * (1.0 / HEAD_DIM)
    vn = ((dv * lax.rsqrt(var + LN_EPS)) * lnw_ref[...] + lnb_ref[...]).astype(BF16)
    lane_grp = lax.broadcasted_iota(jnp.int32, (CHUNK, SG_WIDTH), 1) >> 6
    for j in range(tm // CHUNK):
        vch = vn[j * CHUNK:(j + 1) * CHUNK, :]
        stacked = jnp.concatenate(
            [jnp.where(lane_grp == g, vch, jnp.zeros_like(vch)) for g in range(SG_GROUPS)], axis=0)
        mixed = _dot(sgw_ref[...], stacked) + sgb_ref[...]
        sg_ref[j * CHUNK:(j + 1) * CHUNK, :] = (u[j * CHUNK:(j + 1) * CHUNK, :] * mixed).astype(BF16)

    cos, sn, sp = cos_ref[...], sn_ref[...], sp_ref[...]
    rq_ref[...] = _rope(p[:, OFF_RQ:OFF_RQ + RET_WIDTH], cos, sn, sp).astype(BF16)
    rk_ref[...] = (_rope(p[:, OFF_RK:OFF_RK + RET_WIDTH], cos, sn, sp) * scale).astype(BF16)
    rv_ref[...] = p[:, OFF_RV:OFF_RV + RET_WIDTH].astype(BF16)
    gf_ref[...] = jax.nn.silu(p[:, OFF_GF:OFF_GF + RET_WIDTH])
    gb_ref[...] = jax.nn.silu(p[:, OFF_GB:OFF_GB + RET_WIDTH])


def _mix_in(x, mod, nw, w_in, qn, kn, rope_tabs, sgw, sgb_tab, lnw, lnb, tm):
    b, l, d = x.shape
    mod_map = (lambda i, j: (i, 0, 0)) if mod.shape[0] == b else (lambda i, j: (0, 0, 0))
    tok = lambda w: pl.BlockSpec((None, tm, w), lambda i, j: (i, j, 0))
    tab = pl.BlockSpec((tm, 128), lambda i, j: (j, 0))
    const = lambda shape: _resident(shape, lambda i, j: (0,) * len(shape))
    out_bf = lambda w: jax.ShapeDtypeStruct((b, l, w), BF16)
    out_f32 = lambda w: jax.ShapeDtypeStruct((b, l, w), F32)
    return pl.pallas_call(
        _mix_in_kernel,
        out_shape=(out_bf(NA_WIDTH), out_bf(NA_WIDTH), out_bf(NA_WIDTH), out_bf(SG_WIDTH),
                   out_bf(RET_WIDTH), out_bf(RET_WIDTH), out_bf(RET_WIDTH),
                   out_f32(RET_WIDTH), out_f32(RET_WIDTH)),
        grid=(b, l // tm),
        in_specs=[
            tok(d),
            pl.BlockSpec((None, N_MOD, d), mod_map),
            const((1, d)),
            const((d, IN_COLS)),
            const((1, NA_WIDTH)), const((1, NA_WIDTH)),
            tab, tab, tab,
            const((CHUNK, SG_GROUPS * CHUNK)), const((CHUNK, SG_WIDTH)),
            const((1, SG_WIDTH)), const((1, SG_WIDTH)),
        ],
        out_specs=(tok(NA_WIDTH), tok(NA_WIDTH), tok(NA_WIDTH), tok(SG_WIDTH),
                   tok(RET_WIDTH), tok(RET_WIDTH), tok(RET_WIDTH), tok(RET_WIDTH), tok(RET_WIDTH)),
        compiler_params=_params("parallel", "parallel"),
        name="mix_in",
    )(x, mod, nw.reshape(1, d), w_in, qn, kn, *rope_tabs, sgw, sgb_tab, lnw, lnb)


def _own_lanes(shape, parity):
    even = lax.broadcasted_iota(jnp.int32, shape, 1) < HEAD_DIM
    return even if parity == 0 else jnp.logical_not(even)


def _scores_t(q, parity, keys, biases):
    qp = jnp.where(_own_lanes(q.shape, parity), q, jnp.zeros_like(q))
    return [_dot_nt(kk, qp) if bias is None else _dot_nt(kk, qp) + bias for kk, bias in zip(keys, biases)]


def _weighted_values_t(scores, parity, vals):
    mx = functools.reduce(jnp.maximum, [jnp.max(s, axis=0, keepdims=True) for s in scores])
    num = None
    for s, vv in zip(scores, vals):
        vp = jnp.where(_own_lanes(vv.shape, parity), vv, jnp.ones_like(vv))
        part = _dot_tn(vp, jnp.exp2((s - mx).astype(BF16)))
        num = part if num is None else num + part
    return num


def _normalised_pair(num_even, num_odd):
    even_row = lax.broadcasted_iota(jnp.int32, num_even.shape, 0) < HEAD_DIM
    out_t = jnp.where(even_row, num_even / num_even[HEAD_DIM:HEAD_DIM + 1, :], num_odd / num_odd[0:1, :])
    return out_t.T


def _attend_all_heads(q_ref, o_ref, n_sub, operands):
    m = q_ref.shape[0] // n_sub
    units = [(t, s, parity) for t in range(n_sub) for s in range(NA_WIDTH // 128) for parity in range(2)]

    def logits(u):
        t, s, parity = units[u]
        keys, vals, biases = operands(t, s, parity)
        return _scores_t(q_ref[t * m:(t + 1) * m, s * 128:(s + 1) * 128], parity, keys, biases), vals

    pending = [logits(u) for u in range(NA_LOGITS_AHEAD)]
    nums = []
    for u, (t, s, parity) in enumerate(units):
        scores, vals = pending.pop(0)
        if u + NA_LOGITS_AHEAD < len(units):
            pending.append(logits(u + NA_LOGITS_AHEAD))
        nums.append(_weighted_values_t(scores, parity, vals))
        if parity == 1:
            o_ref[t * m:(t + 1) * m, s * 128:(s + 1) * 128] = (
                _normalised_pair(nums[-2], nums[-1]).astype(BF16))


def _na_kernel(q_ref, k_ref, v_ref, kc_ref, vc_ref, bias_ref, o_ref, *, rows):
    n_blocks = rows // NA_BLOCK_ROWS

    def operands(t, s, parity):
        blk = pl.program_id(1) * NA_BLOCKS_PER_STEP + t
        band_row = jnp.clip(blk * NA_BLOCK_ROWS - NA_WIN_R // 2, 0, rows - NA_BAND_ROWS)
        band = pl.ds(pl.multiple_of(band_row * GRID_W, GRID_W), NA_BAND_ROWS * GRID_W)
        variant = jnp.where(blk == 0, 0, jnp.where(blk == n_blocks - 1, 2, 1))
        sl = slice(s * 128, (s + 1) * 128)
        return ([k_ref[band, sl], kc_ref[:, sl]], [v_ref[band, sl], vc_ref[:, sl]],
                [bias_ref[variant, 2 * s + parity], None])

    _attend_all_heads(q_ref, o_ref, NA_BLOCKS_PER_STEP, operands)


def _na_bias_table(rpb, rows):
    r, band = NA_BLOCK_ROWS, NA_BAND_ROWS
    n_blocks = rows // r
    qc = np.arange(GRID_W)[:, None]
    kcol = np.arange(GRID_W)[None, :]
    c_start = np.clip(qc - NA_WIN_C // 2, 0, GRID_W - NA_WIN_C)
    col_ok = (kcol >= c_start) & (kcol < c_start + NA_WIN_C)
    col_sel = ((kcol - qc + NA_WIN_C - 1)[:, :, None] == np.arange(2 * NA_WIN_C - 1)) & col_ok[:, :, None]
    row_sel = np.zeros((3, r, band, 2 * NA_WIN_R - 1), np.float32)
    for vi, blk in enumerate((0, 1, n_blocks - 1)):
        r0 = blk * r
        b0 = int(np.clip(r0 - NA_WIN_R // 2, 0, rows - band))
        for qi in range(r):
            r_start = int(np.clip(r0 + qi - NA_WIN_R // 2, 0, rows - NA_WIN_R))
            for ki in range(band):
                if r_start <= b0 + ki < r_start + NA_WIN_R:
                    row_sel[vi, qi, ki, b0 + ki - (r0 + qi) + NA_WIN_R - 1] = 1.0
    ok = ((row_sel.sum(-1) > 0).transpose(0, 2, 1)[:, None, :, None, :, None]
          & col_ok.T[None, None, None, :, None, :])
    hp = lax.Precision.HIGHEST
    t_col = jnp.einsum('hij,qkj->hiqk', rpb, jnp.asarray(col_sel, F32), precision=hp)
    tab = jnp.einsum('hiqk,vrsi->vhskrq', t_col, jnp.asarray(row_sel), precision=hp)
    tab = jnp.where(ok, tab * LOG2E, NEG_INF)
    return tab.reshape(3, NA_HEADS, band * GRID_W, r * GRID_W)


def _na(q, k, v, kc, vc, bias_tab, rows):
    b, s, w = q.shape
    lc = kc.shape[1]
    n_steps = rows // (NA_BLOCK_ROWS * NA_BLOCKS_PER_STEP)
    tq = NA_BLOCK_ROWS * NA_BLOCKS_PER_STEP * GRID_W
    return pl.pallas_call(
        functools.partial(_na_kernel, rows=rows),
        out_shape=jax.ShapeDtypeStruct((b, s, w), BF16),
        grid=(b, n_steps),
        in_specs=[
            pl.BlockSpec((None, tq, w), lambda i, j: (i, j, 0)),
            pl.BlockSpec((None, s, w), lambda i, j: (i, 0, 0)),
            pl.BlockSpec((None, s, w), lambda i, j: (i, 0, 0)),
            pl.BlockSpec((None, lc, w), lambda i, j: (i, 0, 0)),
            pl.BlockSpec((None, lc, w), lambda i, j: (i, 0, 0)),
            _resident(bias_tab.shape, lambda i, j: (0, 0, 0, 0)),
        ],
        out_specs=pl.BlockSpec((None, tq, w), lambda i, j: (i, j, 0)),
        compiler_params=_params("parallel", "arbitrary"),
        name="na",
    )(q, k, v, kc, vc, bias_tab)


def _ctx_attn_kernel(q_ref, k_ref, v_ref, o_ref):
    def operands(t, s, parity):
        sl = slice(s * 128, (s + 1) * 128)
        return [k_ref[:, sl]], [v_ref[:, sl]], [None]

    _attend_all_heads(q_ref, o_ref, 1, operands)


def _ctx_attn(q, k, v):
    b, lc, w = q.shape
    spec = pl.BlockSpec((None, lc, w), lambda i: (i, 0, 0))
    return pl.pallas_call(
        _ctx_attn_kernel,
        out_shape=jax.ShapeDtypeStruct((b, lc, w), BF16),
        grid=(b,),
        in_specs=[spec, spec, spec],
        out_specs=spec,
        compiler_params=_params("parallel"),
        name="ctx_attn",
    )(q, k, v)


def _ret_kernel(qf_ref, kf_ref, vf_ref, qb_ref, kb_ref, vb_ref, ks_ref, vs_ref, ws_ref, dm_ref, te_ref,
                fs_ref, cd_ref, of_ref, ob_ref, state_ref, *, n_chunks):
    n = pl.program_id(1)
    same_head = ((lax.broadcasted_iota(jnp.int32, (128, 128), 0) >> 6)
                 == (lax.broadcasted_iota(jnp.int32, (128, 128), 1) >> 6))
    even = lax.broadcasted_iota(jnp.int32, (CHUNK, 128), 1) < HEAD_DIM

    @pl.when(n == 0)
    def _seed():
        for d in range(2):
            for s in range(RET_WIDTH // 128):
                sl = slice(s * 128, (s + 1) * 128)
                kw = (ks_ref[:, sl].astype(F32) * ws_ref[d, :, sl]).astype(BF16)
                state_ref[d, s] = jnp.where(same_head, _dot_tn(kw, vs_ref[:, sl]), 0.0)

    streams = ((qf_ref, kf_ref, vf_ref, of_ref), (qb_ref, kb_ref, vb_ref, ob_ref))
    for s in range(RET_WIDTH // 128):
        sl = slice(s * 128, (s + 1) * 128)
        for d, (q_ref, k_ref, v_ref, o_ref) in enumerate(streams):
            order = [j if d == 0 else n_chunks - 1 - j for j in range(n_chunks)]
            chunk = lambda c: slice(c * CHUNK, (c + 1) * CHUNK)
            kvs = []
            for c in order:
                kt = (k_ref[chunk(c), sl].astype(F32) * te_ref[d, :, sl]).astype(BF16)
                kvs.append(jnp.where(same_head, _dot_tn(kt, v_ref[chunk(c), sl]), 0.0))
            states = [state_ref[d, s]]
            for kv in kvs:
                states.append(cd_ref[d, s] * states[-1] + kv)
            state_ref[d, s] = states[-1]
            for c, st in zip(order, states):
                q = q_ref[chunk(c), sl]
                k = k_ref[chunk(c), sl]
                v = v_ref[chunk(c), sl]
                zero = jnp.zeros_like(k)
                k2 = jnp.concatenate([jnp.where(even, k, zero), jnp.where(even, zero, k)], axis=0)
                v2 = jnp.concatenate([jnp.where(even, v, zero), jnp.where(even, zero, v)], axis=0)
                scores = (_dot_nt(q, k2) * dm_ref[d, s]).astype(BF16)
                qf = (q.astype(F32) * fs_ref[d, :, sl]).astype(BF16)
                o_ref[chunk(c), sl] = _dot(scores, v2) + _dot(qf, st.astype(BF16))


def _ret_tables(log_g, seed_len, seeded):
    pos = jnp.arange(CHUNK, dtype=F32)
    diff = pos[:, None] - pos[None, :]
    lg = log_g[:, :, None, None]
    dm_f = jnp.where(diff >= 0, jnp.exp(jnp.maximum(diff, 0.0)[None, None] * lg), 0.0)[0]
    dm_b = jnp.where(diff <= 0, jnp.exp(jnp.maximum(-diff, 0.0)[None, None] * lg), 0.0)[1]
    dm = jnp.stack([dm_f, dm_b])
    dm = dm.reshape(2, RET_HEADS // 2, 2, CHUNK, CHUNK).transpose(0, 1, 3, 2, 4)
    dm = dm.reshape(2, RET_HEADS // 2, CHUNK, 2 * CHUNK)
    lane = lambda t: jnp.repeat(t, HEAD_DIM, axis=-1)
    te = jnp.stack([jnp.exp((CHUNK - 1 - pos)[:, None] * log_g[0][None, :]),
                    jnp.exp(pos[:, None] * log_g[1][None, :])])
    fs = jnp.stack([jnp.exp((pos + 1.0)[:, None] * log_g[0][None, :]),
                    jnp.exp((CHUNK - pos)[:, None] * log_g[1][None, :])])
    cd = jnp.broadcast_to(lane(jnp.exp(CHUNK * log_g)).reshape(2, RET_HEADS // 2, 128, 1),
                          (2, RET_HEADS // 2, 128, 128))
    sp = jnp.arange(seed_len, dtype=F32)
    ws = jnp.stack([jnp.exp((seed_len - 1 - sp)[:, None] * log_g[0][None, :]),
                    jnp.exp(sp[:, None] * log_g[1][None, :])])
    if not seeded:
        ws = jnp.zeros_like(ws)
    return dm, lane(te), lane(fs), cd, lane(ws)


def _retention(q, k, v, k_seed, v_seed, tables, tb):
    b, l, w = q.shape
    ls = k_seed.shape[1]
    nb = l // tb
    dm, te, fs, cd, ws = tables
    fwd = pl.BlockSpec((None, tb, w), lambda i, n: (i, n, 0))
    bwd = pl.BlockSpec((None, tb, w), lambda i, n: (i, nb - 1 - n, 0))
    seed = pl.BlockSpec((None, ls, w), lambda i, n: (i, 0, 0))
    const = lambda shape: _resident(shape, lambda i, n: (0,) * len(shape))
    out = jax.ShapeDtypeStruct((b, l, w), F32)
    return pl.pallas_call(
        functools.partial(_ret_kernel, n_chunks=tb // CHUNK),
        out_shape=(out, out),
        grid=(b, nb),
        in_specs=[
            fwd, fwd, fwd, bwd, bwd, bwd, seed, seed,
            const(ws.shape), const(dm.shape), const(te.shape), const(fs.shape), const(cd.shape),
        ],
        out_specs=(fwd, bwd),
        scratch_shapes=[pltpu.VMEM((2, RET_HEADS // 2, 128, 128), F32)],
        compiler_params=_params("parallel", "arbitrary"),
        name="retention",
    )(q, k, v, q, k, v, k_seed, v_seed, ws, dm, te, fs, cd)


def _head_group_norm(o, ones_blk):
    mu = _seg_sum(o, ones_blk) * (1.0 / HEAD_DIM)
    dlt = o - mu
    var = _seg_sum(dlt * dlt, ones_blk) * (1.0 / HEAD_DIM)
    return dlt * lax.rsqrt(var + LN_EPS)


def _mix_out_ffn_kernel(x_ref, mod_ref, na_ref, sg_ref, of_ref, ob_ref, gf_ref, gb_ref, gnw_ref,
                        w_ref, nw_ref, w1_ref, w2_ref, o_ref):
    seg_ones = _seg_ones()
    y = (_head_group_norm(of_ref[...], seg_ones) * gf_ref[...]
         + _head_group_norm(ob_ref[...], seg_ones) * gb_ref[...])
    ret = (y * gnw_ref[...]).astype(BF16)
    o_sg = NA_WIDTH + SG_WIDTH
    out = (_dot(na_ref[...], w_ref[0:NA_WIDTH, :])
           + _dot(sg_ref[...], w_ref[NA_WIDTH:o_sg, :])
           + _dot(ret, w_ref[o_sg:, :]))
    x = x_ref[...] + mod_ref[5:6, :] * out
    o_ref[...] = _ffn_half_step(x, mod_ref, nw_ref, w1_ref, w2_ref, 6)


def _mix_out_ffn(x, mod, na, sg, o_fwd, o_bwd, gf, gb, gnw, w_out, nw, w1, w2, tm):
    b, l, d = x.shape
    mod_map = (lambda i, j: (i, 0, 0)) if mod.shape[0] == b else (lambda i, j: (0, 0, 0))
    tok = lambda w: pl.BlockSpec((None, tm, w), lambda i, j: (i, j, 0))
    const = lambda shape: _resident(shape, lambda i, j: (0,) * len(shape))
    return pl.pallas_call(
        _mix_out_ffn_kernel,
        out_shape=jax.ShapeDtypeStruct(x.shape, x.dtype),
        grid=(b, l // tm),
        in_specs=[
            tok(d),
            pl.BlockSpec((None, N_MOD, d), mod_map),
            tok(NA_WIDTH), tok(SG_WIDTH), tok(RET_WIDTH), tok(RET_WIDTH), tok(RET_WIDTH), tok(RET_WIDTH),
            const((1, RET_WIDTH)), const((d, d)),
            const((1, d)), const((d, 2 * D_FF)), const((D_FF, d)),
        ],
        out_specs=tok(d),
        compiler_params=_params("parallel", "parallel"),
        name="mix_out_ffn",
    )(x, mod, na, sg, o_fwd, o_bwd, gf, gb, gnw, w_out, nw.reshape(1, d), w1, w2)


def _rope_tables(n_lat):
    t = jnp.arange(n_lat)
    inv = ROPE_BASE ** (-jnp.arange(ROPE_FREQS, dtype=F32) / ROPE_FREQS)
    ang_r = (t // GRID_W).astype(F32)[:, None] * inv[None, :]
    ang_c = (t % GRID_W).astype(F32)[:, None] * inv[None, :]
    ang = jnp.concatenate([ang_r, ang_r, ang_c, ang_c] * 2, axis=1)
    first = (np.arange(128) // ROPE_FREQS) % 2 == 0
    sin = jnp.sin(ang)
    return jnp.cos(ang), jnp.where(first, -sin, 0.0), jnp.where(first, 0.0, sin)


def kernel(x, c, ctx, c_ctx, ada_w, ada_b, norm_w, ffn_w1, ffn_w2, mix_w_in, mix_w_out, na_q_norm,
           na_k_norm, na_rpb, sg_w, sg_b, sg_ln_w, sg_ln_b, ret_decay_logit, ret_gn_w):
    b, n_lat, d = x.shape
    lc = ctx.shape[1]
    depth = ada_w.shape[0]
    rows = n_lat // GRID_W
    assert d == D_MODEL and n_lat % (GRID_W * NA_BLOCK_ROWS * NA_BLOCKS_PER_STEP) == 0
    assert rows >= 2 * NA_BAND_ROWS - NA_WIN_R
    assert lc % CHUNK == 0
    tm_x = min(512, n_lat)
    tm_c = min(256, lc)
    tb_x = min(1024, n_lat)
    assert n_lat % tb_x == 0
    assert n_lat % tm_x == 0 and lc % tm_c == 0 and tm_x % CHUNK == 0 and tm_c % CHUNK == 0

    cc = jnp.zeros((16, d), F32).at[:b].set(c).at[b].set(c_ctx)
    mod = _adaln(cc, ada_w, ada_b)

    rope_x = _rope_tables(n_lat)
    rope_c = (jnp.ones((lc, 128), F32), jnp.zeros((lc, 128), F32), jnp.zeros((lc, 128), F32))
    w1 = ffn_w1.astype(BF16)
    w2 = ffn_w2.astype(BF16)
    w_in = mix_w_in.astype(BF16)
    w_out = mix_w_out.astype(BF16)
    sgw = sg_w.astype(BF16).transpose(0, 2, 1, 3).reshape(depth, CHUNK, SG_GROUPS * CHUNK)
    log_g = jax.nn.log_sigmoid(ret_decay_logit.astype(F32))

    for layer in range(depth):
        last = layer == depth - 1
        mod_x = mod[layer, :b].reshape(b, N_MOD, d)
        mod_c = mod[layer, b:b + 1].reshape(1, N_MOD, d)
        nw = norm_w[layer]
        x = _ffn(x, mod_x, 0, nw[0], w1[layer, 0], w2[layer, 0], tm_x)
        ctx = _ffn(ctx, mod_c, 0, nw[0], w1[layer, 0], w2[layer, 0], tm_c)

        qn = jnp.tile(na_q_norm[layer], NA_HEADS).reshape(1, NA_WIDTH)
        kn = jnp.tile(na_k_norm[layer], NA_HEADS).reshape(1, NA_WIDTH)
        sgb_tab = jnp.repeat(sg_b[layer].T, HEAD_DIM, axis=1)
        lnw = sg_ln_w[layer].reshape(1, SG_WIDTH)
        lnb = sg_ln_b[layer].reshape(1, SG_WIDTH)
        gnw = ret_gn_w[layer].reshape(1, RET_WIDTH)
        px = _mix_in(x, mod_x, nw[1], w_in[layer], qn, kn, rope_x, sgw[layer], sgb_tab, lnw, lnb,
                     tm_x)
        pc = _mix_in(ctx, mod_c, nw[1], w_in[layer], qn, kn, rope_c, sgw[layer], sgb_tab, lnw, lnb,
                     tm_c)
        nq_x, nk_x, nv_x, sg_x, rq_x, rk_x, rv_x, gf_x, gb_x = px
        nq_c, nk_c, nv_c, sg_c, rq_c, rk_c, rv_c, gf_c, gb_c = pc

        na_x = _na(nq_x, nk_x, nv_x, nk_c, nv_c, _na_bias_table(na_rpb[layer], rows), rows)
        of_x, ob_x = _retention(rq_x, rk_x, rv_x, rk_c, rv_c, _ret_tables(log_g[layer], lc, True), tb_x)
        x = _mix_out_ffn(x, mod_x, na_x, sg_x, of_x, ob_x, gf_x, gb_x, gnw, w_out[layer],
                         nw[2], w1[layer, 1], w2[layer, 1], tm_x)
        if not last:
            na_c = _ctx_attn(nq_c, nk_c, nv_c)
            of_c, ob_c = _retention(rq_c, rk_c, rv_c, rk_c, rv_c, _ret_tables(log_g[layer], lc, False),
                                    tm_c)
            ctx = _mix_out_ffn(ctx, mod_c, na_c, sg_c, of_c, ob_c, gf_c, gb_c, gnw, w_out[layer],
                               nw[2], w1[layer, 1], w2[layer, 1], tm_c)
    return x
```

```python
import functools

import numpy as np
import jax
import jax.numpy as jnp
from jax import lax
from jax.experimental import pallas as pl
from jax.experimental.pallas import tpu as pltpu

D_MODEL = 1024
GRID_W = 64
N_MOD = 9
D_FF = 2816
NA_HEADS = 6
HEAD_DIM = 64
NA_WIDTH = NA_HEADS * HEAD_DIM
NA_WIN_R = 8
NA_WIN_C = 16
SG_GROUPS = 4
SG_WIDTH = SG_GROUPS * HEAD_DIM
CHUNK = 128
RET_HEADS = 6
RET_WIDTH = RET_HEADS * HEAD_DIM
IN_COLS = 3 * NA_WIDTH + 2 * SG_WIDTH + 5 * RET_WIDTH
ROPE_BASE = 10000.0
ROPE_FREQS = HEAD_DIM // 4
RMS_EPS = 1e-6
LN_EPS = 1e-5
NEG_INF = -1e30
LOG2E = 1.4426950408889634

OFF_NQ, OFF_NK, OFF_NV = 0, NA_WIDTH, 2 * NA_WIDTH
OFF_SU = 3 * NA_WIDTH
OFF_SV = OFF_SU + SG_WIDTH
OFF_RQ = OFF_SV + SG_WIDTH
OFF_RK = OFF_RQ + RET_WIDTH
OFF_RV = OFF_RK + RET_WIDTH
OFF_GF = OFF_RV + RET_WIDTH
OFF_GB = OFF_GF + RET_WIDTH

V7X_VMEM_LIMIT_BYTES = 56 * 1024 * 1024
NA_BLOCK_ROWS = 4
NA_BAND_ROWS = NA_BLOCK_ROWS + NA_WIN_R
NA_LOGITS_AHEAD = 2
NA_BLOCKS_PER_STEP = 2
RET_SCORES_AHEAD = 3
FFN_CHUNKS = ((0, 1536), (1536, 1280))

BF16 = jnp.bfloat16
F32 = jnp.float32


def _params(*sem):
    return pltpu.CompilerParams(dimension_semantics=sem, vmem_limit_bytes=V7X_VMEM_LIMIT_BYTES)


def _resident(shape, index_map):
    return pl.BlockSpec(shape, index_map, pipeline_mode=pl.Buffered(1))


def _stacked(w, lead):
    n = len(lead)
    return _resident((None,) * n + w.shape[n:], lambda i, j: tuple(lead) + (0,) * (w.ndim - n))


def _dot(a, b):
    return jnp.dot(a, b, preferred_element_type=F32)


def _dot_nt(a, b):
    return lax.dot_general(a, b, (((1,), (1,)), ((), ())), preferred_element_type=F32)


def _dot_tn(a, b):
    return lax.dot_general(a, b, (((0,), (0,)), ((), ())), preferred_element_type=F32)


def _seg_ones():
    r = lax.broadcasted_iota(jnp.int32, (256, 128), 0)
    c = lax.broadcasted_iota(jnp.int32, (256, 128), 1)
    return jnp.where(((r >> 6) & 1) == (c >> 6), 1.0, 0.0).astype(BF16)


def _seg_sum(t, seg_ones):
    hi = t.astype(BF16)
    lo = (t - hi.astype(F32)).astype(BF16)
    outs = []
    for s in range(t.shape[1] // 128):
        sl = slice(s * 128, (s + 1) * 128)
        outs.append(_dot(jnp.concatenate([hi[:, sl], lo[:, sl]], axis=1), seg_ones))
    return jnp.concatenate(outs, axis=1)


def _ada_rms(x, nw, shift, scale):
    ms = jnp.mean(x * x, axis=-1, keepdims=True)
    return (x * lax.rsqrt(ms + RMS_EPS)) * nw * (1.0 + scale) + shift


def _adaln_kernel(c_ref, w_ref, b_ref, o_ref):
    a = jax.nn.silu(c_ref[...])
    o_ref[...] = jnp.dot(a, w_ref[...], precision=lax.Precision.HIGHEST,
                         preferred_element_type=F32) + b_ref[...]


def _adaln(cc, ada_w, ada_b):
    depth = ada_w.shape[0]
    rows = cc.shape[0]
    tn = D_MODEL
    return pl.pallas_call(
        _adaln_kernel,
        out_shape=jax.ShapeDtypeStruct((depth, rows, N_MOD * D_MODEL), F32),
        grid=(depth, N_MOD * D_MODEL // tn),
        in_specs=[
            pl.BlockSpec((rows, D_MODEL), lambda l, j: (0, 0)),
            pl.BlockSpec((None, D_MODEL, tn), lambda l, j: (l, 0, j)),
            pl.BlockSpec((None, 1, tn), lambda l, j: (l, 0, j)),
        ],
        out_specs=pl.BlockSpec((None, rows, tn), lambda l, j: (l, 0, j)),
        compiler_params=_params("arbitrary", "arbitrary"),
        name="adaln",
    )(cc, ada_w, ada_b.reshape(depth, 1, N_MOD * D_MODEL))


def _ffn_half_step(x, mod_ref, nw_ref, w1_ref, w2_ref, mod_off):
    shift = mod_ref[mod_off:mod_off + 1, :]
    scale = mod_ref[mod_off + 1:mod_off + 2, :]
    gate = mod_ref[mod_off + 2:mod_off + 3, :]
    xb = _ada_rms(x, nw_ref[...], shift, scale).astype(BF16)
    acc = None
    for off, width in FFN_CHUNKS:
        g = _dot(xb, w1_ref[:, off:off + width])
        u = _dot(xb, w1_ref[:, D_FF + off:D_FF + off + width])
        a = (jax.nn.silu(g) * u).astype(BF16)
        part = _dot(a, w2_ref[off:off + width, :])
        acc = part if acc is None else acc + part
    return x + (0.5 * gate) * acc


def _ffn_kernel(x_ref, mod_ref, nw_ref, w1_ref, w2_ref, o_ref, *, mod_off):
    o_ref[...] = _ffn_half_step(x_ref[...], mod_ref, nw_ref, w1_ref, w2_ref, mod_off)


def _ffn(x, mod, mod_off, nw, w1, w2, lead, tm):
    b, l, d = x.shape
    mod_map = (lambda i, j: (i, 0, 0)) if mod.shape[0] == b else (lambda i, j: (0, 0, 0))
    return pl.pallas_call(
        functools.partial(_ffn_kernel, mod_off=mod_off),
        out_shape=jax.ShapeDtypeStruct(x.shape, x.dtype),
        grid=(b, l // tm),
        in_specs=[
            pl.BlockSpec((None, tm, d), lambda i, j: (i, j, 0)),
            pl.BlockSpec((None, N_MOD, d), mod_map),
            _resident((1, d), lambda i, j: (0, 0)),
            _stacked(w1, lead),
            _stacked(w2, lead),
        ],
        out_specs=pl.BlockSpec((None, tm, d), lambda i, j: (i, j, 0)),
        compiler_params=_params("parallel", "parallel"),
        name="ffn",
    )(x, mod, nw.reshape(1, d), w1, w2)


def _head_rms(t, w, ones_blk):
    ms = _seg_sum(t * t, ones_blk) * (1.0 / HEAD_DIM)
    return (t * lax.rsqrt(ms + RMS_EPS)) * w


def _rope(t, cos, sin_next, sin_prev):
    outs = []
    for s in range(RET_WIDTH // 128):
        ts = t[:, s * 128:(s + 1) * 128]
        nxt = pltpu.roll(ts, 128 - ROPE_FREQS, axis=1)
        prv = pltpu.roll(ts, ROPE_FREQS, axis=1)
        outs.append(ts * cos + nxt * sin_next + prv * sin_prev)
    return jnp.concatenate(outs, axis=1)


def _mix_in_kernel(x_ref, mod_ref, nw_ref, w_ref, qn_ref, kn_ref, cos_ref, sn_ref, sp_ref,
                   sgw_ref, sgb_ref, lnw_ref, lnb_ref,
                   nq_ref, nk_ref, nv_ref, sg_ref, rq_ref, rk_ref, rv_ref, gf_ref, gb_ref):
    tm = x_ref.shape[0]
    x = x_ref[...]
    hb = _ada_rms(x, nw_ref[...], mod_ref[3:4, :], mod_ref[4:5, :]).astype(BF16)
    p = _dot(hb, w_ref[...])
    seg_ones = _seg_ones()
    scale = HEAD_DIM ** -0.5

    nq_ref[...] = (_head_rms(p[:, OFF_NQ:OFF_NQ + NA_WIDTH], qn_ref[...], seg_ones)
                   * (scale * LOG2E)).astype(BF16)
    nk_ref[...] = _head_rms(p[:, OFF_NK:OFF_NK + NA_WIDTH], kn_ref[...], seg_ones).astype(BF16)
    nv_ref[...] = p[:, OFF_NV:OFF_NV + NA_WIDTH].astype(BF16)

    u = jax.nn.gelu(p[:, OFF_SU:OFF_SU + SG_WIDTH])
    v = jax.nn.gelu(p[:, OFF_SV:OFF_SV + SG_WIDTH])
    mu = _seg_sum(v, seg_ones) * (1.0 / HEAD_DIM)
    dv = v - mu
    var = _seg_sum(dv * dv, seg_ones) * (1.0 / HEAD_DIM)
    vn = ((dv * lax.rsqrt(var + LN_EPS)) * lnw_ref[...] + lnb_ref[...]).astype(BF16)
    lane_grp = lax.broadcasted_iota(jnp.int32, (CHUNK, SG_WIDTH), 1) >> 6
    for j in range(tm // CHUNK):
        vch = vn[j * CHUNK:(j + 1) * CHUNK, :]
        stacked = jnp.concatenate(
            [jnp.where(lane_grp == g, vch, jnp.zeros_like(vch)) for g in range(SG_GROUPS)], axis=0)
        mixed = _dot(sgw_ref[...], stacked) + sgb_ref[...]
        sg_ref[j * CHUNK:(j + 1) * CHUNK, :] = (u[j * CHUNK:(j + 1) * CHUNK, :] * mixed).astype(BF16)

    cos, sn, sp = cos_ref[...], sn_ref[...], sp_ref[...]
    rq_ref[...] = _rope(p[:, OFF_RQ:OFF_RQ + RET_WIDTH], cos, sn, sp).astype(BF16)
    rk_ref[...] = (_rope(p[:, OFF_RK:OFF_RK + RET_WIDTH], cos, sn, sp) * scale).astype(BF16)
    rv_ref[...] = p[:, OFF_RV:OFF_RV + RET_WIDTH].astype(BF16)
    gf_ref[...] = jax.nn.silu(p[:, OFF_GF:OFF_GF + RET_WIDTH])
    gb_ref[...] = jax.nn.silu(p[:, OFF_GB:OFF_GB + RET_WIDTH])


def _mix_in(x, mod, nw, w_in, layer, qn, kn, rope_tabs, sgw, sgb_tab, lnw, lnb, tm):
    b, l, d = x.shape
    mod_map = (lambda i, j: (i, 0, 0)) if mod.shape[0] == b else (lambda i, j: (0, 0, 0))
    tok = lambda w: pl.BlockSpec((None, tm, w), lambda i, j: (i, j, 0))
    tab = pl.BlockSpec((tm, 128), lambda i, j: (j, 0))
    const = lambda shape: _resident(shape, lambda i, j: (0,) * len(shape))
    out_bf = lambda w: jax.ShapeDtypeStruct((b, l, w), BF16)
    out_f32 = lambda w: jax.ShapeDtypeStruct((b, l, w), F32)
    return pl.pallas_call(
        _mix_in_kernel,
        out_shape=(out_bf(NA_WIDTH), out_bf(NA_WIDTH), out_bf(NA_WIDTH), out_bf(SG_WIDTH),
                   out_bf(RET_WIDTH), out_bf(RET_WIDTH), out_bf(RET_WIDTH),
                   out_f32(RET_WIDTH), out_f32(RET_WIDTH)),
        grid=(b, l // tm),
        in_specs=[
            tok(d),
            pl.BlockSpec((None, N_MOD, d), mod_map),
            const((1, d)),
            _stacked(w_in, (layer,)),
            const((1, NA_WIDTH)), const((1, NA_WIDTH)),
            tab, tab, tab,
            const((CHUNK, SG_GROUPS * CHUNK)), const((CHUNK, SG_WIDTH)),
            const((1, SG_WIDTH)), const((1, SG_WIDTH)),
        ],
        out_specs=(tok(NA_WIDTH), tok(NA_WIDTH), tok(NA_WIDTH), tok(SG_WIDTH),
                   tok(RET_WIDTH), tok(RET_WIDTH), tok(RET_WIDTH), tok(RET_WIDTH), tok(RET_WIDTH)),
        compiler_params=_params("parallel", "parallel"),
        name="mix_in",
    )(x, mod, nw.reshape(1, d), w_in, qn, kn, *rope_tabs, sgw, sgb_tab, lnw, lnb)


def _own_lanes(shape, parity):
    even = lax.broadcasted_iota(jnp.int32, shape, 1) < HEAD_DIM
    return even if parity == 0 else jnp.logical_not(even)


def _scores_t(q, parity, keys, biases):
    qp = jnp.where(_own_lanes(q.shape, parity), q, jnp.zeros_like(q))
    return [_dot_nt(kk, qp) if bias is None else _dot_nt(kk, qp) + bias for kk, bias in zip(keys, biases)]


def _weighted_values_t(scores, parity, vals):
    mx = functools.reduce(jnp.maximum, [jnp.max(s, axis=0, keepdims=True) for s in scores])
    num = None
    for s, vv in zip(scores, vals):
        vp = jnp.where(_own_lanes(vv.shape, parity), vv, jnp.ones_like(vv))
        part = _dot_tn(vp, jnp.exp2((s - mx).astype(BF16)))
        num = part if num is None else num + part
    return num


def _normalised_pair(num_even, num_odd):
    even_row = lax.broadcasted_iota(jnp.int32, num_even.shape, 0) < HEAD_DIM
    out_t = jnp.where(even_row, num_even / num_even[HEAD_DIM:HEAD_DIM + 1, :], num_odd / num_odd[0:1, :])
    return out_t.T


def _attend_all_heads(q_ref, o_ref, n_sub, operands):
    m = q_ref.shape[0] // n_sub
    units = [(t, s, parity) for t in range(n_sub) for s in range(NA_WIDTH // 128) for parity in range(2)]

    def logits(u):
        t, s, parity = units[u]
        keys, vals, biases = operands(t, s, parity)
        return _scores_t(q_ref[t * m:(t + 1) * m, s * 128:(s + 1) * 128], parity, keys, biases), vals

    pending = [logits(u) for u in range(NA_LOGITS_AHEAD)]
    nums = []
    for u, (t, s, parity) in enumerate(units):
        scores, vals = pending.pop(0)
        if u + NA_LOGITS_AHEAD < len(units):
            pending.append(logits(u + NA_LOGITS_AHEAD))
        nums.append(_weighted_values_t(scores, parity, vals))
        if parity == 1:
            o_ref[t * m:(t + 1) * m, s * 128:(s + 1) * 128] = (
                _normalised_pair(nums[-2], nums[-1]).astype(BF16))


def _na_kernel(q_ref, k_ref, v_ref, kc_ref, vc_ref, bias_ref, o_ref, *, rows):
    n_blocks = rows // NA_BLOCK_ROWS

    def operands(t, s, parity):
        blk = pl.program_id(1) * NA_BLOCKS_PER_STEP + t
        band_row = jnp.clip(blk * NA_BLOCK_ROWS - NA_WIN_R // 2, 0, rows - NA_BAND_ROWS)
        band = pl.ds(pl.multiple_of(band_row * GRID_W, GRID_W), NA_BAND_ROWS * GRID_W)
        variant = jnp.where(blk == 0, 0, jnp.where(blk == n_blocks - 1, 2, 1))
        sl = slice(s * 128, (s + 1) * 128)
        return ([k_ref[band, sl], kc_ref[:, sl]], [v_ref[band, sl], vc_ref[:, sl]],
                [bias_ref[variant, 2 * s + parity], None])

    _attend_all_heads(q_ref, o_ref, NA_BLOCKS_PER_STEP, operands)


def _na_bias_table(rpb, rows):
    r, band = NA_BLOCK_ROWS, NA_BAND_ROWS
    n_blocks = rows // r
    kcol = np.arange(GRID_W)[:, None]
    qc = np.arange(GRID_W)[None, :]
    c_start = np.clip(qc - NA_WIN_C // 2, 0, GRID_W - NA_WIN_C)
    col_ok = (kcol >= c_start) & (kcol < c_start + NA_WIN_C)
    col_sel = (kcol - qc + NA_WIN_C - 1)[:, :, None] == np.arange(2 * NA_WIN_C - 1)
    tiles = jnp.einsum('hij,kqj->ihkq', rpb, jnp.asarray(col_sel, F32), precision=lax.Precision.HIGHEST)
    tiles = jnp.where(col_ok, tiles * LOG2E, NEG_INF)
    masked = jnp.full((NA_HEADS, GRID_W, GRID_W), NEG_INF, F32)
    variants = []
    for blk in (0, 1, n_blocks - 1):
        r0 = blk * r
        b0 = int(np.clip(r0 - NA_WIN_R // 2, 0, rows - band))
        key_rows = []
        for ki in range(band):
            per_query_row = []
            for qi in range(r):
                r_start = int(np.clip(r0 + qi - NA_WIN_R // 2, 0, rows - NA_WIN_R))
                inside = r_start <= b0 + ki < r_start + NA_WIN_R
                per_query_row.append(tiles[b0 + ki - (r0 + qi) + NA_WIN_R - 1] if inside else masked)
            key_rows.append(jnp.stack(per_query_row, axis=2))
        variants.append(jnp.stack(key_rows, axis=1))
    return jnp.stack(variants).reshape(3, NA_HEADS, band * GRID_W, r * GRID_W)


def _na(q, k, v, kc, vc, bias_tab, rows):
    b, s, w = q.shape
    lc = kc.shape[1]
    n_steps = rows // (NA_BLOCK_ROWS * NA_BLOCKS_PER_STEP)
    tq = NA_BLOCK_ROWS * NA_BLOCKS_PER_STEP * GRID_W
    return pl.pallas_call(
        functools.partial(_na_kernel, rows=rows),
        out_shape=jax.ShapeDtypeStruct((b, s, w), BF16),
        grid=(b, n_steps),
        in_specs=[
            pl.BlockSpec((None, tq, w), lambda i, j: (i, j, 0)),
            pl.BlockSpec((None, s, w), lambda i, j: (i, 0, 0)),
            pl.BlockSpec((None, s, w), lambda i, j: (i, 0, 0)),
            pl.BlockSpec((None, lc, w), lambda i, j: (i, 0, 0)),
            pl.BlockSpec((None, lc, w), lambda i, j: (i, 0, 0)),
            _resident(bias_tab.shape, lambda i, j: (0, 0, 0, 0)),
        ],
        out_specs=pl.BlockSpec((None, tq, w), lambda i, j: (i, j, 0)),
        compiler_params=_params("parallel", "arbitrary"),
        name="na",
    )(q, k, v, kc, vc, bias_tab)


def _ctx_attn_kernel(q_ref, k_ref, v_ref, o_ref):
    def operands(t, s, parity):
        sl = slice(s * 128, (s + 1) * 128)
        return [k_ref[:, sl]], [v_ref[:, sl]], [None]

    _attend_all_heads(q_ref, o_ref, 1, operands)


def _ctx_attn(q, k, v):
    b, lc, w = q.shape
    spec = pl.BlockSpec((None, lc, w), lambda i: (i, 0, 0))
    return pl.pallas_call(
        _ctx_attn_kernel,
        out_shape=jax.ShapeDtypeStruct((b, lc, w), BF16),
        grid=(b,),
        in_specs=[spec, spec, spec],
        out_specs=spec,
        compiler_params=_params("parallel"),
        name="ctx_attn",
    )(q, k, v)


def _ret_kernel(qf_ref, kf_ref, vf_ref, qb_ref, kb_ref, vb_ref, ks_ref, vs_ref, ws_ref, dm_ref, te_ref,
                fs_ref, cd_ref, of_ref, ob_ref, state_ref, *, n_chunks):
    n = pl.program_id(1)
    same_head = ((lax.broadcasted_iota(jnp.int32, (128, 128), 0) >> 6)
                 == (lax.broadcasted_iota(jnp.int32, (128, 128), 1) >> 6))
    even = lax.broadcasted_iota(jnp.int32, (CHUNK, 128), 1) < HEAD_DIM

    @pl.when(n == 0)
    def _seed():
        for d in range(2):
            for s in range(RET_WIDTH // 128):
                sl = slice(s * 128, (s + 1) * 128)
                kw = (ks_ref[:, sl].astype(F32) * ws_ref[d, :, sl]).astype(BF16)
                state_ref[d, s] = jnp.where(same_head, _dot_tn(kw, vs_ref[:, sl]), 0.0)

    streams = ((qf_ref, kf_ref, vf_ref, of_ref), (qb_ref, kb_ref, vb_ref, ob_ref))
    chunk = lambda c: slice(c * CHUNK, (c + 1) * CHUNK)
    units = []
    for s in range(RET_WIDTH // 128):
        sl = slice(s * 128, (s + 1) * 128)
        for d, (_, k_ref, v_ref, _) in enumerate(streams):
            order = [j if d == 0 else n_chunks - 1 - j for j in range(n_chunks)]
            kvs = []
            for c in order:
                kt = (k_ref[chunk(c), sl].astype(F32) * te_ref[d, :, sl]).astype(BF16)
                kvs.append(jnp.where(same_head, _dot_tn(kt, v_ref[chunk(c), sl]), 0.0))
            st = state_ref[d, s]
            for c, kv in zip(order, kvs):
                units.append((s, d, c, st))
                st = cd_ref[d, s] * st + kv
            state_ref[d, s] = st

    def decayed_scores(u):
        s, d, c, _ = units[u]
        sl = slice(s * 128, (s + 1) * 128)
        q = streams[d][0][chunk(c), sl]
        k = streams[d][1][chunk(c), sl]
        zero = jnp.zeros_like(k)
        k2 = jnp.concatenate([jnp.where(even, k, zero), jnp.where(even, zero, k)], axis=0)
        return (_dot_nt(q, k2) * dm_ref[d, s]).astype(BF16)

    pending = [decayed_scores(u) for u in range(min(RET_SCORES_AHEAD, len(units)))]
    for u, (s, d, c, st) in enumerate(units):
        sl = slice(s * 128, (s + 1) * 128)
        scores = pending.pop(0)
        if u + RET_SCORES_AHEAD < len(units):
            pending.append(decayed_scores(u + RET_SCORES_AHEAD))
        q_ref, _, v_ref, o_ref = streams[d]
        v = v_ref[chunk(c), sl]
        zero = jnp.zeros_like(v)
        v2 = jnp.concatenate([jnp.where(even, v, zero), jnp.where(even, zero, v)], axis=0)
        qf = (q_ref[chunk(c), sl].astype(F32) * fs_ref[d, :, sl]).astype(BF16)
        o_ref[chunk(c), sl] = _dot(scores, v2) + _dot(qf, st.astype(BF16))


def _ret_tables(log_g, seed_len, seeded):
    pos = jnp.arange(CHUNK, dtype=F32)
    diff = pos[:, None] - pos[None, :]
    lg = log_g[:, :, None, None]
    dm_f = jnp.where(diff >= 0, jnp.exp(jnp.maximum(diff, 0.0)[None, None] * lg), 0.0)[0]
    dm_b = jnp.where(diff <= 0, jnp.exp(jnp.maximum(-diff, 0.0)[None, None] * lg), 0.0)[1]
    dm = jnp.stack([dm_f, dm_b])
    dm = dm.reshape(2, RET_HEADS // 2, 2, CHUNK, CHUNK).transpose(0, 1, 3, 2, 4)
    dm = dm.reshape(2, RET_HEADS // 2, CHUNK, 2 * CHUNK)
    lane = lambda t: jnp.repeat(t, HEAD_DIM, axis=-1)
    te = jnp.stack([jnp.exp((CHUNK - 1 - pos)[:, None] * log_g[0][None, :]),
                    jnp.exp(pos[:, None] * log_g[1][None, :])])
    fs = jnp.stack([jnp.exp((pos + 1.0)[:, None] * log_g[0][None, :]),
                    jnp.exp((CHUNK - pos)[:, None] * log_g[1][None, :])])
    cd = jnp.broadcast_to(lane(jnp.exp(CHUNK * log_g)).reshape(2, RET_HEADS // 2, 128, 1),
                          (2, RET_HEADS // 2, 128, 128))
    sp = jnp.arange(seed_len, dtype=F32)
    ws = jnp.stack([jnp.exp((seed_len - 1 - sp)[:, None] * log_g[0][None, :]),
                    jnp.exp(sp[:, None] * log_g[1][None, :])])
    if not seeded:
        ws = jnp.zeros_like(ws)
    return dm, lane(te), lane(fs), cd, lane(ws)


def _retention(q, k, v, k_seed, v_seed, tables, tb):
    b, l, w = q.shape
    ls = k_seed.shape[1]
    nb = l // tb
    dm, te, fs, cd, ws = tables
    fwd = pl.BlockSpec((None, tb, w), lambda i, n: (i, n, 0))
    bwd = pl.BlockSpec((None, tb, w), lambda i, n: (i, nb - 1 - n, 0))
    seed = pl.BlockSpec((None, ls, w), lambda i, n: (i, 0, 0))
    const = lambda shape: _resident(shape, lambda i, n: (0,) * len(shape))
    out = jax.ShapeDtypeStruct((b, l, w), F32)
    return pl.pallas_call(
        functools.partial(_ret_kernel, n_chunks=tb // CHUNK),
        out_shape=(out, out),
        grid=(b, nb),
        in_specs=[
            fwd, fwd, fwd, bwd, bwd, bwd, seed, seed,
            const(ws.shape), const(dm.shape), const(te.shape), const(fs.shape), const(cd.shape),
        ],
        out_specs=(fwd, bwd),
        scratch_shapes=[pltpu.VMEM((2, RET_HEADS // 2, 128, 128), F32)],
        compiler_params=_params("parallel", "arbitrary"),
        name="retention",
    )(q, k, v, q, k, v, k_seed, v_seed, ws, dm, te, fs, cd)


def _head_group_norm(o, ones_blk):
    mu = _seg_sum(o, ones_blk) * (1.0 / HEAD_DIM)
    dlt = o - mu
    var = _seg_sum(dlt * dlt, ones_blk) * (1.0 / HEAD_DIM)
    return dlt * lax.rsqrt(var + LN_EPS)


def _mix_out_ffn_kernel(x_ref, mod_ref, na_ref, sg_ref, of_ref, ob_ref, gf_ref, gb_ref, gnw_ref,
                        w_ref, nw_ref, w1_ref, w2_ref, o_ref):
    seg_ones = _seg_ones()
    y = (_head_group_norm(of_ref[...], seg_ones) * gf_ref[...]
         + _head_group_norm(ob_ref[...], seg_ones) * gb_ref[...])
    ret = (y * gnw_ref[...]).astype(BF16)
    o_sg = NA_WIDTH + SG_WIDTH
    out = (_dot(na_ref[...], w_ref[0:NA_WIDTH, :])
           + _dot(sg_ref[...], w_ref[NA_WIDTH:o_sg, :])
           + _dot(ret, w_ref[o_sg:, :]))
    x = x_ref[...] + mod_ref[5:6, :] * out
    o_ref[...] = _ffn_half_step(x, mod_ref, nw_ref, w1_ref, w2_ref, 6)


def _mix_out_ffn(x, mod, na, sg, o_fwd, o_bwd, gf, gb, gnw, w_out, nw, w1, w2, layer, tm):
    b, l, d = x.shape
    mod_map = (lambda i, j: (i, 0, 0)) if mod.shape[0] == b else (lambda i, j: (0, 0, 0))
    tok = lambda w: pl.BlockSpec((None, tm, w), lambda i, j: (i, j, 0))
    const = lambda shape: _resident(shape, lambda i, j: (0,) * len(shape))
    return pl.pallas_call(
        _mix_out_ffn_kernel,
        out_shape=jax.ShapeDtypeStruct(x.shape, x.dtype),
        grid=(b, l // tm),
        in_specs=[
            tok(d),
            pl.BlockSpec((None, N_MOD, d), mod_map),
            tok(NA_WIDTH), tok(SG_WIDTH), tok(RET_WIDTH), tok(RET_WIDTH), tok(RET_WIDTH), tok(RET_WIDTH),
            const((1, RET_WIDTH)), _stacked(w_out, (layer,)),
            const((1, d)), _stacked(w1, (layer, 1)), _stacked(w2, (layer, 1)),
        ],
        out_specs=tok(d),
        compiler_params=_params("parallel", "parallel"),
        name="mix_out_ffn",
    )(x, mod, na, sg, o_fwd, o_bwd, gf, gb, gnw, w_out, nw.reshape(1, d), w1, w2)


def _rope_tables(n_lat):
    t = jnp.arange(n_lat)
    inv = ROPE_BASE ** (-jnp.arange(ROPE_FREQS, dtype=F32) / ROPE_FREQS)
    ang_r = (t // GRID_W).astype(F32)[:, None] * inv[None, :]
    ang_c = (t % GRID_W).astype(F32)[:, None] * inv[None, :]
    ang = jnp.concatenate([ang_r, ang_r, ang_c, ang_c] * 2, axis=1)
    first = (np.arange(128) // ROPE_FREQS) % 2 == 0
    sin = jnp.sin(ang)
    return jnp.cos(ang), jnp.where(first, -sin, 0.0), jnp.where(first, 0.0, sin)


def kernel(x, c, ctx, c_ctx, ada_w, ada_b, norm_w, ffn_w1, ffn_w2, mix_w_in, mix_w_out, na_q_norm,
           na_k_norm, na_rpb, sg_w, sg_b, sg_ln_w, sg_ln_b, ret_decay_logit, ret_gn_w):
    b, n_lat, d = x.shape
    lc = ctx.shape[1]
    depth = ada_w.shape[0]
    rows = n_lat // GRID_W
    assert d == D_MODEL and n_lat % (GRID_W * NA_BLOCK_ROWS * NA_BLOCKS_PER_STEP) == 0
    assert rows >= 2 * NA_BAND_ROWS - NA_WIN_R
    assert lc % CHUNK == 0
    tm_x = min(512, n_lat)
    tm_c = min(256, lc)
    tb_x = min(1024, n_lat)
    assert n_lat % tb_x == 0
    assert n_lat % tm_x == 0 and lc % tm_c == 0 and tm_x % CHUNK == 0 and tm_c % CHUNK == 0

    cc = jnp.zeros((16, d), F32).at[:b].set(c).at[b].set(c_ctx)
    mod = _adaln(cc, ada_w, ada_b)

    rope_x = _rope_tables(n_lat)
    rope_c = (jnp.ones((lc, 128), F32), jnp.zeros((lc, 128), F32), jnp.zeros((lc, 128), F32))
    w1 = ffn_w1.astype(BF16)
    w2 = ffn_w2.astype(BF16)
    w_in = mix_w_in.astype(BF16)
    w_out = mix_w_out.astype(BF16)
    sgw = sg_w.astype(BF16).transpose(0, 2, 1, 3).reshape(depth, CHUNK, SG_GROUPS * CHUNK)
    log_g = jax.nn.log_sigmoid(ret_decay_logit.astype(F32))

    for layer in range(depth):
        last = layer == depth - 1
        mod_x = mod[layer, :b].reshape(b, N_MOD, d)
        mod_c = mod[layer, b:b + 1].reshape(1, N_MOD, d)
        nw = norm_w[layer]
        x = _ffn(x, mod_x, 0, nw[0], w1, w2, (layer, 0), tm_x)
        ctx = _ffn(ctx, mod_c, 0, nw[0], w1, w2, (layer, 0), tm_c)

        qn = jnp.tile(na_q_norm[layer], NA_HEADS).reshape(1, NA_WIDTH)
        kn = jnp.tile(na_k_norm[layer], NA_HEADS).reshape(1, NA_WIDTH)
        sgb_tab = jnp.repeat(sg_b[layer].T, HEAD_DIM, axis=1)
        lnw = sg_ln_w[layer].reshape(1, SG_WIDTH)
        lnb = sg_ln_b[layer].reshape(1, SG_WIDTH)
        gnw = ret_gn_w[layer].reshape(1, RET_WIDTH)
        px = _mix_in(x, mod_x, nw[1], w_in, layer, qn, kn, rope_x, sgw[layer], sgb_tab, lnw, lnb, tm_x)
        pc = _mix_in(ctx, mod_c, nw[1], w_in, layer, qn, kn, rope_c, sgw[layer], sgb_tab, lnw, lnb, tm_c)
        nq_x, nk_x, nv_x, sg_x, rq_x, rk_x, rv_x, gf_x, gb_x = px
        nq_c, nk_c, nv_c, sg_c, rq_c, rk_c, rv_c, gf_c, gb_c = pc

        na_x = _na(nq_x, nk_x, nv_x, nk_c, nv_c, _na_bias_table(na_rpb[layer], rows), rows)
        of_x, ob_x = _retention(rq_x, rk_x, rv_x, rk_c, rv_c, _ret_tables(log_g[layer], lc, True), tb_x)
        x = _mix_out_ffn(x, mod_x, na_x, sg_x, of_x, ob_x, gf_x, gb_x, gnw, w_out, nw[2], w1, w2,
                         layer, tm_x)
        if not last:
            na_c = _ctx_attn(nq_c, nk_c, nv_c)
            of_c, ob_c = _retention(rq_c, rk_c, rv_c, rk_c, rv_c, _ret_tables(log_g[layer], lc, False),
                                    tm_c)
            ctx = _mix_out_ffn(ctx, mod_c, na_c, sg_c, of_c, ob_c, gf_c, gb_c, gnw, w_out, nw[2], w1, w2,
                               layer, tm_c)
    return x
```

```python
import functools

import numpy as np
import jax
import jax.numpy as jnp
from jax import lax
from jax.experimental import pallas as pl
from jax.experimental.pallas import tpu as pltpu

D_MODEL = 1024
GRID_W = 64
N_MOD = 9
D_FF = 2816
NA_HEADS = 6
HEAD_DIM = 64
NA_WIDTH = NA_HEADS * HEAD_DIM
NA_WIN_R = 8
NA_WIN_C = 16
SG_GROUPS = 4
SG_WIDTH = SG_GROUPS * HEAD_DIM
CHUNK = 128
RET_HEADS = 6
RET_WIDTH = RET_HEADS * HEAD_DIM
IN_COLS = 3 * NA_WIDTH + 2 * SG_WIDTH + 5 * RET_WIDTH
ROPE_BASE = 10000.0
ROPE_FREQS = HEAD_DIM // 4
RMS_EPS = 1e-6
LN_EPS = 1e-5
NEG_INF = -1e30
LOG2E = 1.4426950408889634

OFF_NQ, OFF_NK, OFF_NV = 0, NA_WIDTH, 2 * NA_WIDTH
OFF_SU = 3 * NA_WIDTH
OFF_SV = OFF_SU + SG_WIDTH
OFF_RQ = OFF_SV + SG_WIDTH
OFF_RK = OFF_RQ + RET_WIDTH
OFF_RV = OFF_RK + RET_WIDTH
OFF_GF = OFF_RV + RET_WIDTH
OFF_GB = OFF_GF + RET_WIDTH

V7X_VMEM_LIMIT_BYTES = 56 * 1024 * 1024
NA_BLOCK_ROWS = 4
NA_BAND_ROWS = NA_BLOCK_ROWS + NA_WIN_R
NA_LOGITS_AHEAD = 2
NA_BLOCKS_PER_STEP = 4
RET_SCORES_AHEAD = 3
FFN_CHUNKS = ((0, 1536), (1536, 1280))

BF16 = jnp.bfloat16
F32 = jnp.float32


def _params(*sem):
    return pltpu.CompilerParams(dimension_semantics=sem, vmem_limit_bytes=V7X_VMEM_LIMIT_BYTES)


def _resident(shape, index_map):
    return pl.BlockSpec(shape, index_map, pipeline_mode=pl.Buffered(1))


def _stacked(w, lead):
    n = len(lead)
    return _resident((None,) * n + w.shape[n:], lambda i, j: tuple(lead) + (0,) * (w.ndim - n))


def _dot(a, b):
    return jnp.dot(a, b, preferred_element_type=F32)


def _dot_nt(a, b):
    return lax.dot_general(a, b, (((1,), (1,)), ((), ())), preferred_element_type=F32)


def _dot_tn(a, b):
    return lax.dot_general(a, b, (((0,), (0,)), ((), ())), preferred_element_type=F32)


def _seg_ones():
    r = lax.broadcasted_iota(jnp.int32, (256, 128), 0)
    c = lax.broadcasted_iota(jnp.int32, (256, 128), 1)
    return jnp.where(((r >> 6) & 1) == (c >> 6), 1.0, 0.0).astype(BF16)


def _seg_sum(t, seg_ones):
    hi = t.astype(BF16)
    lo = (t - hi.astype(F32)).astype(BF16)
    outs = []
    for s in range(t.shape[1] // 128):
        sl = slice(s * 128, (s + 1) * 128)
        outs.append(_dot(jnp.concatenate([hi[:, sl], lo[:, sl]], axis=1), seg_ones))
    return jnp.concatenate(outs, axis=1)


def _ada_rms(x, nw, shift, scale):
    ms = jnp.mean(x * x, axis=-1, keepdims=True)
    return (x * lax.rsqrt(ms + RMS_EPS)) * nw * (1.0 + scale) + shift


def _adaln_kernel(c_ref, w_ref, b_ref, o_ref):
    a = jax.nn.silu(c_ref[...])
    o_ref[...] = jnp.dot(a, w_ref[...], precision=lax.Precision.HIGHEST,
                         preferred_element_type=F32) + b_ref[...]


def _adaln(cc, ada_w, ada_b):
    depth = ada_w.shape[0]
    rows = cc.shape[0]
    tn = D_MODEL
    return pl.pallas_call(
        _adaln_kernel,
        out_shape=jax.ShapeDtypeStruct((depth, rows, N_MOD * D_MODEL), F32),
        grid=(depth, N_MOD * D_MODEL // tn),
        in_specs=[
            pl.BlockSpec((rows, D_MODEL), lambda l, j: (0, 0)),
            pl.BlockSpec((None, D_MODEL, tn), lambda l, j: (l, 0, j)),
            pl.BlockSpec((None, 1, tn), lambda l, j: (l, 0, j)),
        ],
        out_specs=pl.BlockSpec((None, rows, tn), lambda l, j: (l, 0, j)),
        compiler_params=_params("arbitrary", "arbitrary"),
        name="adaln",
    )(cc, ada_w, ada_b.reshape(depth, 1, N_MOD * D_MODEL))


def _ffn_half_step(x, mod_ref, nw_ref, w1_ref, w2_ref, mod_off):
    shift = mod_ref[mod_off:mod_off + 1, :]
    scale = mod_ref[mod_off + 1:mod_off + 2, :]
    gate = mod_ref[mod_off + 2:mod_off + 3, :]
    xb = _ada_rms(x, nw_ref[...], shift, scale).astype(BF16)
    acc = None
    for off, width in FFN_CHUNKS:
        g = _dot(xb, w1_ref[:, off:off + width])
        u = _dot(xb, w1_ref[:, D_FF + off:D_FF + off + width])
        a = (jax.nn.silu(g) * u).astype(BF16)
        part = _dot(a, w2_ref[off:off + width, :])
        acc = part if acc is None else acc + part
    return x + (0.5 * gate) * acc


def _ffn_kernel(x_ref, mod_ref, nw_ref, w1_ref, w2_ref, o_ref, *, mod_off):
    o_ref[...] = _ffn_half_step(x_ref[...], mod_ref, nw_ref, w1_ref, w2_ref, mod_off)


def _ffn(x, mod, mod_off, nw, w1, w2, lead, tm):
    b, l, d = x.shape
    mod_map = (lambda i, j: (i, 0, 0)) if mod.shape[0] == b else (lambda i, j: (0, 0, 0))
    return pl.pallas_call(
        functools.partial(_ffn_kernel, mod_off=mod_off),
        out_shape=jax.ShapeDtypeStruct(x.shape, x.dtype),
        grid=(b, l // tm),
        in_specs=[
            pl.BlockSpec((None, tm, d), lambda i, j: (i, j, 0)),
            pl.BlockSpec((None, N_MOD, d), mod_map),
            _resident((1, d), lambda i, j: (0, 0)),
            _stacked(w1, lead),
            _stacked(w2, lead),
        ],
        out_specs=pl.BlockSpec((None, tm, d), lambda i, j: (i, j, 0)),
        compiler_params=_params("parallel", "parallel"),
        name="ffn",
    )(x, mod, nw.reshape(1, d), w1, w2)


def _head_rms(t, w, ones_blk):
    ms = _seg_sum(t * t, ones_blk) * (1.0 / HEAD_DIM)
    return (t * lax.rsqrt(ms + RMS_EPS)) * w


def _rope(t, cos, sin_next, sin_prev):
    outs = []
    for s in range(RET_WIDTH // 128):
        ts = t[:, s * 128:(s + 1) * 128]
        nxt = pltpu.roll(ts, 128 - ROPE_FREQS, axis=1)
        prv = pltpu.roll(ts, ROPE_FREQS, axis=1)
        outs.append(ts * cos + nxt * sin_next + prv * sin_prev)
    return jnp.concatenate(outs, axis=1)


def _mix_in_kernel(x_ref, mod_ref, nw_ref, w_ref, qn_ref, kn_ref, cos_ref, sn_ref, sp_ref,
                   sgw_ref, sgb_ref, lnw_ref, lnb_ref,
                   nq_ref, nk_ref, nv_ref, sg_ref, rq_ref, rk_ref, rv_ref, gf_ref, gb_ref):
    tm = x_ref.shape[0]
    x = x_ref[...]
    hb = _ada_rms(x, nw_ref[...], mod_ref[3:4, :], mod_ref[4:5, :]).astype(BF16)
    p = _dot(hb, w_ref[...])
    seg_ones = _seg_ones()
    scale = HEAD_DIM ** -0.5

    nq_ref[...] = (_head_rms(p[:, OFF_NQ:OFF_NQ + NA_WIDTH], qn_ref[...], seg_ones)
                   * (scale * LOG2E)).astype(BF16)
    nk_ref[...] = _head_rms(p[:, OFF_NK:OFF_NK + NA_WIDTH], kn_ref[...], seg_ones).astype(BF16)
    nv_ref[...] = p[:, OFF_NV:OFF_NV + NA_WIDTH].astype(BF16)

    u = jax.nn.gelu(p[:, OFF_SU:OFF_SU + SG_WIDTH])
    v = jax.nn.gelu(p[:, OFF_SV:OFF_SV + SG_WIDTH])
    mu = _seg_sum(v, seg_ones) * (1.0 / HEAD_DIM)
    dv = v - mu
    var = _seg_sum(dv * dv, seg_ones) * (1.0 / HEAD_DIM)
    vn = ((dv * lax.rsqrt(var + LN_EPS)) * lnw_ref[...] + lnb_ref[...]).astype(BF16)
    lane_grp = lax.broadcasted_iota(jnp.int32, (CHUNK, SG_WIDTH), 1) >> 6
    for j in range(tm // CHUNK):
        vch = vn[j * CHUNK:(j + 1) * CHUNK, :]
        stacked = jnp.concatenate(
            [jnp.where(lane_grp == g, vch, jnp.zeros_like(vch)) for g in range(SG_GROUPS)], axis=0)
        mixed = _dot(sgw_ref[...], stacked) + sgb_ref[...]
        sg_ref[j * CHUNK:(j + 1) * CHUNK, :] = (u[j * CHUNK:(j + 1) * CHUNK, :] * mixed).astype(BF16)

    cos, sn, sp = cos_ref[...], sn_ref[...], sp_ref[...]
    rq_ref[...] = _rope(p[:, OFF_RQ:OFF_RQ + RET_WIDTH], cos, sn, sp).astype(BF16)
    rk_ref[...] = (_rope(p[:, OFF_RK:OFF_RK + RET_WIDTH], cos, sn, sp) * scale).astype(BF16)
    rv_ref[...] = p[:, OFF_RV:OFF_RV + RET_WIDTH].astype(BF16)
    gf_ref[...] = jax.nn.silu(p[:, OFF_GF:OFF_GF + RET_WIDTH])
    gb_ref[...] = jax.nn.silu(p[:, OFF_GB:OFF_GB + RET_WIDTH])


def _mix_in(x, mod, nw, w_in, layer, qn, kn, rope_tabs, sgw, sgb_tab, lnw, lnb, tm):
    b, l, d = x.shape
    mod_map = (lambda i, j: (i, 0, 0)) if mod.shape[0] == b else (lambda i, j: (0, 0, 0))
    tok = lambda w: pl.BlockSpec((None, tm, w), lambda i, j: (i, j, 0))
    tab = pl.BlockSpec((tm, 128), lambda i, j: (j, 0))
    const = lambda shape: _resident(shape, lambda i, j: (0,) * len(shape))
    out_bf = lambda w: jax.ShapeDtypeStruct((b, l, w), BF16)
    out_f32 = lambda w: jax.ShapeDtypeStruct((b, l, w), F32)
    return pl.pallas_call(
        _mix_in_kernel,
        out_shape=(out_bf(NA_WIDTH), out_bf(NA_WIDTH), out_bf(NA_WIDTH), out_bf(SG_WIDTH),
                   out_bf(RET_WIDTH), out_bf(RET_WIDTH), out_bf(RET_WIDTH),
                   out_f32(RET_WIDTH), out_f32(RET_WIDTH)),
        grid=(b, l // tm),
        in_specs=[
            tok(d),
            pl.BlockSpec((None, N_MOD, d), mod_map),
            const((1, d)),
            _stacked(w_in, (layer,)),
            const((1, NA_WIDTH)), const((1, NA_WIDTH)),
            tab, tab, tab,
            const((CHUNK, SG_GROUPS * CHUNK)), const((CHUNK, SG_WIDTH)),
            const((1, SG_WIDTH)), const((1, SG_WIDTH)),
        ],
        out_specs=(tok(NA_WIDTH), tok(NA_WIDTH), tok(NA_WIDTH), tok(SG_WIDTH),
                   tok(RET_WIDTH), tok(RET_WIDTH), tok(RET_WIDTH), tok(RET_WIDTH), tok(RET_WIDTH)),
        compiler_params=_params("parallel", "parallel"),
        name="mix_in",
    )(x, mod, nw.reshape(1, d), w_in, qn, kn, *rope_tabs, sgw, sgb_tab, lnw, lnb)


def _own_lanes(shape, parity):
    even = lax.broadcasted_iota(jnp.int32, shape, 1) < HEAD_DIM
    return even if parity == 0 else jnp.logical_not(even)


def _scores_t(q, parity, keys, biases):
    qp = jnp.where(_own_lanes(q.shape, parity), q, jnp.zeros_like(q))
    scores = [_dot_nt(kk, qp) if bias is None else _dot_nt(kk, qp) + bias for kk, bias in zip(keys, biases)]
    mx = functools.reduce(jnp.maximum, [jnp.max(s, axis=0, keepdims=True) for s in scores])
    return scores, mx


def _weighted_values_t(scores, mx, parity, vals):
    num = None
    for s, vv in zip(scores, vals):
        vp = jnp.where(_own_lanes(vv.shape, parity), vv, jnp.ones_like(vv))
        part = _dot_tn(vp, jnp.exp2((s - mx).astype(BF16)))
        num = part if num is None else num + part
    return num


def _normalised_pair(num_even, num_odd):
    even_row = lax.broadcasted_iota(jnp.int32, num_even.shape, 0) < HEAD_DIM
    out_t = jnp.where(even_row, num_even / num_even[HEAD_DIM:HEAD_DIM + 1, :], num_odd / num_odd[0:1, :])
    return out_t.T


def _attend_all_heads(q_ref, o_ref, n_sub, operands):
    m = q_ref.shape[0] // n_sub
    units = [(t, s, parity) for t in range(n_sub) for s in range(NA_WIDTH // 128) for parity in range(2)]

    def logits(u):
        t, s, parity = units[u]
        keys, vals, biases = operands(t, s, parity)
        return _scores_t(q_ref[t * m:(t + 1) * m, s * 128:(s + 1) * 128], parity, keys, biases), vals

    pending = [logits(u) for u in range(NA_LOGITS_AHEAD)]
    nums = []
    for u, (t, s, parity) in enumerate(units):
        (scores, mx), vals = pending.pop(0)
        if u + NA_LOGITS_AHEAD < len(units):
            pending.append(logits(u + NA_LOGITS_AHEAD))
        nums.append(_weighted_values_t(scores, mx, parity, vals))
        if parity == 1:
            o_ref[t * m:(t + 1) * m, s * 128:(s + 1) * 128] = (
                _normalised_pair(nums[-2], nums[-1]).astype(BF16))


def _na_kernel(q_ref, k_ref, v_ref, kc_ref, vc_ref, bias_ref, o_ref, *, rows):
    n_blocks = rows // NA_BLOCK_ROWS

    def operands(t, s, parity):
        blk = pl.program_id(1) * NA_BLOCKS_PER_STEP + t
        band_row = jnp.clip(blk * NA_BLOCK_ROWS - NA_WIN_R // 2, 0, rows - NA_BAND_ROWS)
        band = pl.ds(pl.multiple_of(band_row * GRID_W, GRID_W), NA_BAND_ROWS * GRID_W)
        variant = jnp.where(blk == 0, 0, jnp.where(blk == n_blocks - 1, 2, 1))
        sl = slice(s * 128, (s + 1) * 128)
        return ([k_ref[band, sl], kc_ref[:, sl]], [v_ref[band, sl], vc_ref[:, sl]],
                [bias_ref[variant, 2 * s + parity], None])

    _attend_all_heads(q_ref, o_ref, NA_BLOCKS_PER_STEP, operands)


def _na_bias_table(rpb, rows):
    r, band = NA_BLOCK_ROWS, NA_BAND_ROWS
    n_blocks = rows // r
    kcol = np.arange(GRID_W)[:, None]
    qc = np.arange(GRID_W)[None, :]
    c_start = np.clip(qc - NA_WIN_C // 2, 0, GRID_W - NA_WIN_C)
    col_ok = (kcol >= c_start) & (kcol < c_start + NA_WIN_C)
    col_sel = (kcol - qc + NA_WIN_C - 1)[:, :, None] == np.arange(2 * NA_WIN_C - 1)
    tiles = jnp.einsum('hij,kqj->ihkq', rpb, jnp.asarray(col_sel, F32), precision=lax.Precision.HIGHEST)
    tiles = jnp.where(col_ok, tiles * LOG2E, NEG_INF)
    masked = jnp.full((NA_HEADS, GRID_W, GRID_W), NEG_INF, F32)
    variants = []
    for blk in (0, 1, n_blocks - 1):
        r0 = blk * r
        b0 = int(np.clip(r0 - NA_WIN_R // 2, 0, rows - band))
        key_rows = []
        for ki in range(band):
            per_query_row = []
            for qi in range(r):
                r_start = int(np.clip(r0 + qi - NA_WIN_R // 2, 0, rows - NA_WIN_R))
                inside = r_start <= b0 + ki < r_start + NA_WIN_R
                per_query_row.append(tiles[b0 + ki - (r0 + qi) + NA_WIN_R - 1] if inside else masked)
            key_rows.append(jnp.stack(per_query_row, axis=2))
        variants.append(jnp.stack(key_rows, axis=1))
    return jnp.stack(variants).reshape(3, NA_HEADS, band * GRID_W, r * GRID_W)


def _na(q, k, v, kc, vc, bias_tab, rows):
    b, s, w = q.shape
    lc = kc.shape[1]
    n_steps = rows // (NA_BLOCK_ROWS * NA_BLOCKS_PER_STEP)
    tq = NA_BLOCK_ROWS * NA_BLOCKS_PER_STEP * GRID_W
    return pl.pallas_call(
        functools.partial(_na_kernel, rows=rows),
        out_shape=jax.ShapeDtypeStruct((b, s, w), BF16),
        grid=(b, n_steps),
        in_specs=[
            pl.BlockSpec((None, tq, w), lambda i, j: (i, j, 0)),
            pl.BlockSpec((None, s, w), lambda i, j: (i, 0, 0)),
            pl.BlockSpec((None, s, w), lambda i, j: (i, 0, 0)),
            pl.BlockSpec((None, lc, w), lambda i, j: (i, 0, 0)),
            pl.BlockSpec((None, lc, w), lambda i, j: (i, 0, 0)),
            _resident(bias_tab.shape, lambda i, j: (0, 0, 0, 0)),
        ],
        out_specs=pl.BlockSpec((None, tq, w), lambda i, j: (i, j, 0)),
        compiler_params=_params("parallel", "arbitrary"),
        name="na",
    )(q, k, v, kc, vc, bias_tab)


def _ctx_attn_kernel(q_ref, k_ref, v_ref, o_ref):
    def operands(t, s, parity):
        sl = slice(s * 128, (s + 1) * 128)
        return [k_ref[:, sl]], [v_ref[:, sl]], [None]

    _attend_all_heads(q_ref, o_ref, 1, operands)


def _ctx_attn(q, k, v):
    b, lc, w = q.shape
    spec = pl.BlockSpec((None, lc, w), lambda i: (i, 0, 0))
    return pl.pallas_call(
        _ctx_attn_kernel,
        out_shape=jax.ShapeDtypeStruct((b, lc, w), BF16),
        grid=(b,),
        in_specs=[spec, spec, spec],
        out_specs=spec,
        compiler_params=_params("parallel"),
        name="ctx_attn",
    )(q, k, v)


def _ret_kernel(qf_ref, kf_ref, vf_ref, qb_ref, kb_ref, vb_ref, ks_ref, vs_ref, ws_ref, dm_ref, te_ref,
                fs_ref, cd_ref, of_ref, ob_ref, state_ref, *, n_chunks):
    n = pl.program_id(1)
    same_head = ((lax.broadcasted_iota(jnp.int32, (128, 128), 0) >> 6)
                 == (lax.broadcasted_iota(jnp.int32, (128, 128), 1) >> 6))
    even = lax.broadcasted_iota(jnp.int32, (CHUNK, 128), 1) < HEAD_DIM

    @pl.when(n == 0)
    def _seed():
        for d in range(2):
            for s in range(RET_WIDTH // 128):
                sl = slice(s * 128, (s + 1) * 128)
                kw = (ks_ref[:, sl].astype(F32) * ws_ref[d, :, sl]).astype(BF16)
                state_ref[d, s] = jnp.where(same_head, _dot_tn(kw, vs_ref[:, sl]), 0.0)

    streams = ((qf_ref, kf_ref, vf_ref, of_ref), (qb_ref, kb_ref, vb_ref, ob_ref))
    chunk = lambda c: slice(c * CHUNK, (c + 1) * CHUNK)
    units = []
    for s in range(RET_WIDTH // 128):
        sl = slice(s * 128, (s + 1) * 128)
        for d, (_, k_ref, v_ref, _) in enumerate(streams):
            order = [j if d == 0 else n_chunks - 1 - j for j in range(n_chunks)]
            kvs = []
            for c in order:
                kt = (k_ref[chunk(c), sl].astype(F32) * te_ref[d, :, sl]).astype(BF16)
                kvs.append(jnp.where(same_head, _dot_tn(kt, v_ref[chunk(c), sl]), 0.0))
            st = state_ref[d, s]
            for c, kv in zip(order, kvs):
                units.append((s, d, c, st))
                st = cd_ref[d, s] * st + kv
            state_ref[d, s] = st

    def decayed_scores(u):
        s, d, c, _ = units[u]
        sl = slice(s * 128, (s + 1) * 128)
        q = streams[d][0][chunk(c), sl]
        k = streams[d][1][chunk(c), sl]
        zero = jnp.zeros_like(k)
        k2 = jnp.concatenate([jnp.where(even, k, zero), jnp.where(even, zero, k)], axis=0)
        return (_dot_nt(q, k2) * dm_ref[d, s]).astype(BF16)

    pending = [decayed_scores(u) for u in range(min(RET_SCORES_AHEAD, len(units)))]
    for u, (s, d, c, st) in enumerate(units):
        sl = slice(s * 128, (s + 1) * 128)
        scores = pending.pop(0)
        if u + RET_SCORES_AHEAD < len(units):
            pending.append(decayed_scores(u + RET_SCORES_AHEAD))
        q_ref, _, v_ref, o_ref = streams[d]
        v = v_ref[chunk(c), sl]
        zero = jnp.zeros_like(v)
        v2 = jnp.concatenate([jnp.where(even, v, zero), jnp.where(even, zero, v)], axis=0)
        qf = (q_ref[chunk(c), sl].astype(F32) * fs_ref[d, :, sl]).astype(BF16)
        o_ref[chunk(c), sl] = _dot(scores, v2) + _dot(qf, st.astype(BF16))


def _ret_tables(log_g, seed_len, seeded):
    pos = jnp.arange(CHUNK, dtype=F32)
    diff = pos[:, None] - pos[None, :]
    lg = log_g[:, :, None, None]
    dm_f = jnp.where(diff >= 0, jnp.exp(jnp.maximum(diff, 0.0)[None, None] * lg), 0.0)[0]
    dm_b = jnp.where(diff <= 0, jnp.exp(jnp.maximum(-diff, 0.0)[None, None] * lg), 0.0)[1]
    dm = jnp.stack([dm_f, dm_b])
    dm = dm.reshape(2, RET_HEADS // 2, 2, CHUNK, CHUNK).transpose(0, 1, 3, 2, 4)
    dm = dm.reshape(2, RET_HEADS // 2, CHUNK, 2 * CHUNK)
    lane = lambda t: jnp.repeat(t, HEAD_DIM, axis=-1)
    te = jnp.stack([jnp.exp((CHUNK - 1 - pos)[:, None] * log_g[0][None, :]),
                    jnp.exp(pos[:, None] * log_g[1][None, :])])
    fs = jnp.stack([jnp.exp((pos + 1.0)[:, None] * log_g[0][None, :]),
                    jnp.exp((CHUNK - pos)[:, None] * log_g[1][None, :])])
    cd = jnp.broadcast_to(lane(jnp.exp(CHUNK * log_g)).reshape(2, RET_HEADS // 2, 128, 1),
                          (2, RET_HEADS // 2, 128, 128))
    sp = jnp.arange(seed_len, dtype=F32)
    ws = jnp.stack([jnp.exp((seed_len - 1 - sp)[:, None] * log_g[0][None, :]),
                    jnp.exp(sp[:, None] * log_g[1][None, :])])
    if not seeded:
        ws = jnp.zeros_like(ws)
    return dm, lane(te), lane(fs), cd, lane(ws)


def _retention(q, k, v, k_seed, v_seed, tables, tb):
    b, l, w = q.shape
    ls = k_seed.shape[1]
    nb = l // tb
    dm, te, fs, cd, ws = tables
    fwd = pl.BlockSpec((None, tb, w), lambda i, n: (i, n, 0))
    bwd = pl.BlockSpec((None, tb, w), lambda i, n: (i, nb - 1 - n, 0))
    seed = pl.BlockSpec((None, ls, w), lambda i, n: (i, 0, 0))
    const = lambda shape: _resident(shape, lambda i, n: (0,) * len(shape))
    out = jax.ShapeDtypeStruct((b, l, w), F32)
    return pl.pallas_call(
        functools.partial(_ret_kernel, n_chunks=tb // CHUNK),
        out_shape=(out, out),
        grid=(b, nb),
        in_specs=[
            fwd, fwd, fwd, bwd, bwd, bwd, seed, seed,
            const(ws.shape), const(dm.shape), const(te.shape), const(fs.shape), const(cd.shape),
        ],
        out_specs=(fwd, bwd),
        scratch_shapes=[pltpu.VMEM((2, RET_HEADS // 2, 128, 128), F32)],
        compiler_params=_params("parallel", "arbitrary"),
        name="retention",
    )(q, k, v, q, k, v, k_seed, v_seed, ws, dm, te, fs, cd)


def _head_group_norm(o, ones_blk):
    mu = _seg_sum(o, ones_blk) * (1.0 / HEAD_DIM)
    dlt = o - mu
    var = _seg_sum(dlt * dlt, ones_blk) * (1.0 / HEAD_DIM)
    return dlt * lax.rsqrt(var + LN_EPS)


def _mix_out_ffn_kernel(x_ref, mod_ref, na_ref, sg_ref, of_ref, ob_ref, gf_ref, gb_ref, gnw_ref,
                        w_ref, nw_ref, w1_ref, w2_ref, o_ref):
    seg_ones = _seg_ones()
    y = (_head_group_norm(of_ref[...], seg_ones) * gf_ref[...]
         + _head_group_norm(ob_ref[...], seg_ones) * gb_ref[...])
    ret = (y * gnw_ref[...]).astype(BF16)
    o_sg = NA_WIDTH + SG_WIDTH
    out = (_dot(na_ref[...], w_ref[0:NA_WIDTH, :])
           + _dot(sg_ref[...], w_ref[NA_WIDTH:o_sg, :])
           + _dot(ret, w_ref[o_sg:, :]))
    x = x_ref[...] + mod_ref[5:6, :] * out
    o_ref[...] = _ffn_half_step(x, mod_ref, nw_ref, w1_ref, w2_ref, 6)


def _mix_out_ffn(x, mod, na, sg, o_fwd, o_bwd, gf, gb, gnw, w_out, nw, w1, w2, layer, tm):
    b, l, d = x.shape
    mod_map = (lambda i, j: (i, 0, 0)) if mod.shape[0] == b else (lambda i, j: (0, 0, 0))
    tok = lambda w: pl.BlockSpec((None, tm, w), lambda i, j: (i, j, 0))
    const = lambda shape: _resident(shape, lambda i, j: (0,) * len(shape))
    return pl.pallas_call(
        _mix_out_ffn_kernel,
        out_shape=jax.ShapeDtypeStruct(x.shape, x.dtype),
        grid=(b, l // tm),
        in_specs=[
            tok(d),
            pl.BlockSpec((None, N_MOD, d), mod_map),
            tok(NA_WIDTH), tok(SG_WIDTH), tok(RET_WIDTH), tok(RET_WIDTH), tok(RET_WIDTH), tok(RET_WIDTH),
            const((1, RET_WIDTH)), _stacked(w_out, (layer,)),
            const((1, d)), _stacked(w1, (layer, 1)), _stacked(w2, (layer, 1)),
        ],
        out_specs=tok(d),
        compiler_params=_params("parallel", "parallel"),
        name="mix_out_ffn",
    )(x, mod, na, sg, o_fwd, o_bwd, gf, gb, gnw, w_out, nw.reshape(1, d), w1, w2)


def _rope_tables(n_lat):
    t = jnp.arange(n_lat)[:, None]
    lane = np.arange(128)
    inv = ROPE_BASE ** (-jnp.asarray(lane % ROPE_FREQS, F32) / ROPE_FREQS)
    pos = jnp.where(lane % HEAD_DIM < HEAD_DIM // 2, t // GRID_W, t % GRID_W).astype(F32)
    ang = pos * inv[None, :]
    first = (lane // ROPE_FREQS) % 2 == 0
    sin = jnp.sin(ang)
    return jnp.cos(ang), jnp.where(first, -sin, 0.0), jnp.where(first, 0.0, sin)


def kernel(x, c, ctx, c_ctx, ada_w, ada_b, norm_w, ffn_w1, ffn_w2, mix_w_in, mix_w_out, na_q_norm,
           na_k_norm, na_rpb, sg_w, sg_b, sg_ln_w, sg_ln_b, ret_decay_logit, ret_gn_w):
    b, n_lat, d = x.shape
    lc = ctx.shape[1]
    depth = ada_w.shape[0]
    rows = n_lat // GRID_W
    assert d == D_MODEL and n_lat % (GRID_W * NA_BLOCK_ROWS * NA_BLOCKS_PER_STEP) == 0
    assert rows >= 2 * NA_BAND_ROWS - NA_WIN_R
    assert lc % CHUNK == 0
    tm_x = min(512, n_lat)
    tm_c = min(256, lc)
    tb_x = min(1024, n_lat)
    assert n_lat % tb_x == 0
    assert n_lat % tm_x == 0 and lc % tm_c == 0 and tm_x % CHUNK == 0 and tm_c % CHUNK == 0

    cc = jnp.zeros((16, d), F32).at[:b].set(c).at[b].set(c_ctx)
    mod = _adaln(cc, ada_w, ada_b)

    rope_x = _rope_tables(n_lat)
    rope_c = (jnp.ones((lc, 128), F32), jnp.zeros((lc, 128), F32), jnp.zeros((lc, 128), F32))
    w1 = ffn_w1.astype(BF16)
    w2 = ffn_w2.astype(BF16)
    w_in = mix_w_in.astype(BF16)
    w_out = mix_w_out.astype(BF16)
    sgw = sg_w.astype(BF16).transpose(0, 2, 1, 3).reshape(depth, CHUNK, SG_GROUPS * CHUNK)
    log_g = jax.nn.log_sigmoid(ret_decay_logit.astype(F32))

    for layer in range(depth):
        last = layer == depth - 1
        mod_x = mod[layer, :b].reshape(b, N_MOD, d)
        mod_c = mod[layer, b:b + 1].reshape(1, N_MOD, d)
        nw = norm_w[layer]
        x = _ffn(x, mod_x, 0, nw[0], w1, w2, (layer, 0), tm_x)
        ctx = _ffn(ctx, mod_c, 0, nw[0], w1, w2, (layer, 0), tm_c)

        qn = jnp.tile(na_q_norm[layer], NA_HEADS).reshape(1, NA_WIDTH)
        kn = jnp.tile(na_k_norm[layer], NA_HEADS).reshape(1, NA_WIDTH)
        sgb_tab = jnp.repeat(sg_b[layer].T, HEAD_DIM, axis=1)
        lnw = sg_ln_w[layer].reshape(1, SG_WIDTH)
        lnb = sg_ln_b[layer].reshape(1, SG_WIDTH)
        gnw = ret_gn_w[layer].reshape(1, RET_WIDTH)
        px = _mix_in(x, mod_x, nw[1], w_in, layer, qn, kn, rope_x, sgw[layer], sgb_tab, lnw, lnb, tm_x)
        pc = _mix_in(ctx, mod_c, nw[1], w_in, layer, qn, kn, rope_c, sgw[layer], sgb_tab, lnw, lnb, tm_c)
        nq_x, nk_x, nv_x, sg_x, rq_x, rk_x, rv_x, gf_x, gb_x = px
        nq_c, nk_c, nv_c, sg_c, rq_c, rk_c, rv_c, gf_c, gb_c = pc

        na_x = _na(nq_x, nk_x, nv_x, nk_c, nv_c, _na_bias_table(na_rpb[layer], rows), rows)
        of_x, ob_x = _retention(rq_x, rk_x, rv_x, rk_c, rv_c, _ret_tables(log_g[layer], lc, True), tb_x)
        x = _mix_out_ffn(x, mod_x, na_x, sg_x, of_x, ob_x, gf_x, gb_x, gnw, w_out, nw[2], w1, w2,
                         layer, tm_x)
        if not last:
            na_c = _ctx_attn(nq_c, nk_c, nv_c)
            of_c, ob_c = _retention(rq_c, rk_c, rv_c, rk_c, rv_c, _ret_tables(log_g[layer], lc, False),
                                    tm_c)
            ctx = _mix_out_ffn(ctx, mod_c, na_c, sg_c, of_c, ob_c, gf_c, gb_c, gnw, w_out, nw[2], w1, w2,
                               layer, tm_c)
    return x
```

```python
import functools

import numpy as np
import jax
import jax.numpy as jnp
from jax import lax
from jax.experimental import pallas as pl
from jax.experimental.pallas import tpu as pltpu

D_MODEL = 1024
GRID_W = 64
N_MOD = 9
D_FF = 2816
NA_HEADS = 6
HEAD_DIM = 64
NA_WIDTH = NA_HEADS * HEAD_DIM
NA_WIN_R = 8
NA_WIN_C = 16
SG_GROUPS = 4
SG_WIDTH = SG_GROUPS * HEAD_DIM
CHUNK = 128
RET_HEADS = 6
RET_WIDTH = RET_HEADS * HEAD_DIM
IN_COLS = 3 * NA_WIDTH + 2 * SG_WIDTH + 5 * RET_WIDTH
ROPE_BASE = 10000.0
ROPE_FREQS = HEAD_DIM // 4
RMS_EPS = 1e-6
LN_EPS = 1e-5
NEG_INF = -1e30
LOG2E = 1.4426950408889634

OFF_NQ, OFF_NK, OFF_NV = 0, NA_WIDTH, 2 * NA_WIDTH
OFF_SU = 3 * NA_WIDTH
OFF_SV = OFF_SU + SG_WIDTH
OFF_RQ = OFF_SV + SG_WIDTH
OFF_RK = OFF_RQ + RET_WIDTH
OFF_RV = OFF_RK + RET_WIDTH
OFF_GF = OFF_RV + RET_WIDTH
OFF_GB = OFF_GF + RET_WIDTH

V7X_VMEM_LIMIT_BYTES = 56 * 1024 * 1024
NA_BLOCK_ROWS = 4
NA_BAND_ROWS = NA_BLOCK_ROWS + NA_WIN_R
NA_LOGITS_AHEAD = 2
NA_BLOCKS_PER_STEP = 4
RET_SCORES_AHEAD = 3
FFN_CHUNKS = ((0, 1536), (1536, 1280))

BF16 = jnp.bfloat16
F32 = jnp.float32


def _params(*sem):
    return pltpu.CompilerParams(dimension_semantics=sem, vmem_limit_bytes=V7X_VMEM_LIMIT_BYTES)


def _resident(shape, index_map):
    return pl.BlockSpec(shape, index_map, pipeline_mode=pl.Buffered(1))


def _stacked(w, lead):
    n = len(lead)
    return _resident((None,) * n + w.shape[n:], lambda i, j: tuple(lead) + (0,) * (w.ndim - n))


def _dot(a, b):
    return jnp.dot(a, b, preferred_element_type=F32)


def _dot_nt(a, b):
    return lax.dot_general(a, b, (((1,), (1,)), ((), ())), preferred_element_type=F32)


def _dot_tn(a, b):
    return lax.dot_general(a, b, (((0,), (0,)), ((), ())), preferred_element_type=F32)


def _seg_ones():
    r = lax.broadcasted_iota(jnp.int32, (256, 128), 0)
    c = lax.broadcasted_iota(jnp.int32, (256, 128), 1)
    return jnp.where(((r >> 6) & 1) == (c >> 6), 1.0, 0.0).astype(BF16)


def _seg_sum(t, seg_ones):
    hi = t.astype(BF16)
    lo = (t - hi.astype(F32)).astype(BF16)
    outs = []
    for s in range(t.shape[1] // 128):
        sl = slice(s * 128, (s + 1) * 128)
        outs.append(_dot(jnp.concatenate([hi[:, sl], lo[:, sl]], axis=1), seg_ones))
    return jnp.concatenate(outs, axis=1)


def _ada_rms(x, nw, shift, scale):
    ms = jnp.mean(x * x, axis=-1, keepdims=True)
    return (x * lax.rsqrt(ms + RMS_EPS)) * nw * (1.0 + scale) + shift


def _adaln_kernel(c_ref, w_ref, b_ref, o_ref):
    a = jax.nn.silu(c_ref[...])
    o_ref[...] = jnp.dot(a, w_ref[...], precision=lax.Precision.HIGHEST,
                         preferred_element_type=F32) + b_ref[...]


def _adaln(cc, ada_w, ada_b):
    depth = ada_w.shape[0]
    rows = cc.shape[0]
    tn = D_MODEL
    return pl.pallas_call(
        _adaln_kernel,
        out_shape=jax.ShapeDtypeStruct((depth, rows, N_MOD * D_MODEL), F32),
        grid=(depth, N_MOD * D_MODEL // tn),
        in_specs=[
            pl.BlockSpec((rows, D_MODEL), lambda l, j: (0, 0)),
            pl.BlockSpec((None, D_MODEL, tn), lambda l, j: (l, 0, j)),
            pl.BlockSpec((None, 1, tn), lambda l, j: (l, 0, j)),
        ],
        out_specs=pl.BlockSpec((None, rows, tn), lambda l, j: (l, 0, j)),
        compiler_params=_params("arbitrary", "arbitrary"),
        name="adaln",
    )(cc, ada_w, ada_b.reshape(depth, 1, N_MOD * D_MODEL))


def _ffn_half_step(x, mod_ref, nw_ref, w1_ref, w2_ref, mod_off):
    shift = mod_ref[mod_off:mod_off + 1, :]
    scale = mod_ref[mod_off + 1:mod_off + 2, :]
    gate = mod_ref[mod_off + 2:mod_off + 3, :]
    xb = _ada_rms(x, nw_ref[...], shift, scale).astype(BF16)
    acc = None
    for off, width in FFN_CHUNKS:
        g = _dot(xb, w1_ref[:, off:off + width])
        u = _dot(xb, w1_ref[:, D_FF + off:D_FF + off + width])
        a = (jax.nn.silu(g) * u).astype(BF16)
        part = _dot(a, w2_ref[off:off + width, :])
        acc = part if acc is None else acc + part
    return x + (0.5 * gate) * acc


def _ffn_kernel(x_ref, mod_ref, nw_ref, w1_ref, w2_ref, o_ref, *, mod_off):
    o_ref[...] = _ffn_half_step(x_ref[...], mod_ref, nw_ref, w1_ref, w2_ref, mod_off)


def _ffn(x, mod, mod_off, nw, w1, w2, lead, tm):
    b, l, d = x.shape
    mod_map = (lambda i, j: (i, 0, 0)) if mod.shape[0] == b else (lambda i, j: (0, 0, 0))
    return pl.pallas_call(
        functools.partial(_ffn_kernel, mod_off=mod_off),
        out_shape=jax.ShapeDtypeStruct(x.shape, x.dtype),
        grid=(b, l // tm),
        in_specs=[
            pl.BlockSpec((None, tm, d), lambda i, j: (i, j, 0)),
            pl.BlockSpec((None, N_MOD, d), mod_map),
            _resident((1, d), lambda i, j: (0, 0)),
            _stacked(w1, lead),
            _stacked(w2, lead),
        ],
        out_specs=pl.BlockSpec((None, tm, d), lambda i, j: (i, j, 0)),
        compiler_params=_params("parallel", "parallel"),
        name="ffn",
    )(x, mod, nw.reshape(1, d), w1, w2)


def _head_rms(t, w, ones_blk):
    ms = _seg_sum(t * t, ones_blk) * (1.0 / HEAD_DIM)
    return (t * lax.rsqrt(ms + RMS_EPS)) * w


def _rope(t, cos, sin_next, sin_prev):
    outs = []
    for s in range(RET_WIDTH // 128):
        ts = t[:, s * 128:(s + 1) * 128]
        nxt = pltpu.roll(ts, 128 - ROPE_FREQS, axis=1)
        prv = pltpu.roll(ts, ROPE_FREQS, axis=1)
        outs.append(ts * cos + nxt * sin_next + prv * sin_prev)
    return jnp.concatenate(outs, axis=1)


def _mix_in_kernel(x_ref, mod_ref, nw_ref, w_ref, qn_ref, kn_ref, cos_ref, sn_ref, sp_ref,
                   sgw_ref, sgb_ref, lnw_ref, lnb_ref,
                   nq_ref, nk_ref, nv_ref, sg_ref, rq_ref, rk_ref, rv_ref, gf_ref, gb_ref):
    tm = x_ref.shape[0]
    x = x_ref[...]
    hb = _ada_rms(x, nw_ref[...], mod_ref[3:4, :], mod_ref[4:5, :]).astype(BF16)
    p = _dot(hb, w_ref[...])
    seg_ones = _seg_ones()
    scale = HEAD_DIM ** -0.5

    nq_ref[...] = (_head_rms(p[:, OFF_NQ:OFF_NQ + NA_WIDTH], qn_ref[...], seg_ones)
                   * (scale * LOG2E)).astype(BF16)
    nk_ref[...] = _head_rms(p[:, OFF_NK:OFF_NK + NA_WIDTH], kn_ref[...], seg_ones).astype(BF16)
    nv_ref[...] = p[:, OFF_NV:OFF_NV + NA_WIDTH].astype(BF16)

    u = jax.nn.gelu(p[:, OFF_SU:OFF_SU + SG_WIDTH])
    v = jax.nn.gelu(p[:, OFF_SV:OFF_SV + SG_WIDTH])
    mu = _seg_sum(v, seg_ones) * (1.0 / HEAD_DIM)
    dv = v - mu
    var = _seg_sum(dv * dv, seg_ones) * (1.0 / HEAD_DIM)
    vn = ((dv * lax.rsqrt(var + LN_EPS)) * lnw_ref[...] + lnb_ref[...]).astype(BF16)
    lane_grp = lax.broadcasted_iota(jnp.int32, (CHUNK, SG_WIDTH), 1) >> 6
    for j in range(tm // CHUNK):
        vch = vn[j * CHUNK:(j + 1) * CHUNK, :]
        stacked = jnp.concatenate(
            [jnp.where(lane_grp == g, vch, jnp.zeros_like(vch)) for g in range(SG_GROUPS)], axis=0)
        mixed = _dot(sgw_ref[...], stacked) + sgb_ref[...]
        sg_ref[j * CHUNK:(j + 1) * CHUNK, :] = (u[j * CHUNK:(j + 1) * CHUNK, :] * mixed).astype(BF16)

    cos, sn, sp = cos_ref[...], sn_ref[...], sp_ref[...]
    rq_ref[...] = _rope(p[:, OFF_RQ:OFF_RQ + RET_WIDTH], cos, sn, sp).astype(BF16)
    rk_ref[...] = (_rope(p[:, OFF_RK:OFF_RK + RET_WIDTH], cos, sn, sp) * scale).astype(BF16)
    rv_ref[...] = p[:, OFF_RV:OFF_RV + RET_WIDTH].astype(BF16)
    gf_ref[...] = jax.nn.silu(p[:, OFF_GF:OFF_GF + RET_WIDTH])
    gb_ref[...] = jax.nn.silu(p[:, OFF_GB:OFF_GB + RET_WIDTH])


def _mix_in(x, mod, nw, w_in, layer, qn, kn, rope_tabs, sgw, sgb_tab, lnw, lnb, tm):
    b, l, d = x.shape
    mod_map = (lambda i, j: (i, 0, 0)) if mod.shape[0] == b else (lambda i, j: (0, 0, 0))
    tok = lambda w: pl.BlockSpec((None, tm, w), lambda i, j: (i, j, 0))
    tab = pl.BlockSpec((tm, 128), lambda i, j: (j, 0))
    const = lambda shape: _resident(shape, lambda i, j: (0,) * len(shape))
    out_bf = lambda w: jax.ShapeDtypeStruct((b, l, w), BF16)
    out_f32 = lambda w: jax.ShapeDtypeStruct((b, l, w), F32)
    return pl.pallas_call(
        _mix_in_kernel,
        out_shape=(out_bf(NA_WIDTH), out_bf(NA_WIDTH), out_bf(NA_WIDTH), out_bf(SG_WIDTH),
                   out_bf(RET_WIDTH), out_bf(RET_WIDTH), out_bf(RET_WIDTH),
                   out_f32(RET_WIDTH), out_f32(RET_WIDTH)),
        grid=(b, l // tm),
        in_specs=[
            tok(d),
            pl.BlockSpec((None, N_MOD, d), mod_map),
            const((1, d)),
            _stacked(w_in, (layer,)),
            const((1, NA_WIDTH)), const((1, NA_WIDTH)),
            tab, tab, tab,
            const((CHUNK, SG_GROUPS * CHUNK)), const((CHUNK, SG_WIDTH)),
            const((1, SG_WIDTH)), const((1, SG_WIDTH)),
        ],
        out_specs=(tok(NA_WIDTH), tok(NA_WIDTH), tok(NA_WIDTH), tok(SG_WIDTH),
                   tok(RET_WIDTH), tok(RET_WIDTH), tok(RET_WIDTH), tok(RET_WIDTH), tok(RET_WIDTH)),
        compiler_params=_params("parallel", "parallel"),
        name="mix_in",
    )(x, mod, nw.reshape(1, d), w_in, qn, kn, *rope_tabs, sgw, sgb_tab, lnw, lnb)


def _own_lanes(shape, parity):
    even = lax.broadcasted_iota(jnp.int32, shape, 1) < HEAD_DIM
    return even if parity == 0 else jnp.logical_not(even)


def _scores_t(q, parity, keys, biases):
    qp = jnp.where(_own_lanes(q.shape, parity), q, jnp.zeros_like(q))
    scores = [_dot_nt(kk, qp) if bias is None else _dot_nt(kk, qp) + bias for kk, bias in zip(keys, biases)]
    mx = functools.reduce(jnp.maximum, [jnp.max(s, axis=0, keepdims=True) for s in scores])
    return scores, mx


def _weighted_values_t(scores, mx, parity, vals):
    num = None
    for s, vv in zip(scores, vals):
        vp = jnp.where(_own_lanes(vv.shape, parity), vv, jnp.ones_like(vv))
        part = _dot_tn(vp, jnp.exp2((s - mx).astype(BF16)))
        num = part if num is None else num + part
    return num


def _normalised_pair(num_even, num_odd):
    even_row = lax.broadcasted_iota(jnp.int32, num_even.shape, 0) < HEAD_DIM
    out_t = jnp.where(even_row, num_even / num_even[HEAD_DIM:HEAD_DIM + 1, :], num_odd / num_odd[0:1, :])
    return out_t.T


def _attend_all_heads(q_ref, o_ref, n_sub, operands):
    m = q_ref.shape[0] // n_sub
    units = [(t, s, parity) for t in range(n_sub) for s in range(NA_WIDTH // 128) for parity in range(2)]

    def logits(u):
        t, s, parity = units[u]
        keys, vals, biases = operands(t, s, parity)
        return _scores_t(q_ref[t * m:(t + 1) * m, s * 128:(s + 1) * 128], parity, keys, biases), vals

    pending = [logits(u) for u in range(NA_LOGITS_AHEAD)]
    nums = []
    for u, (t, s, parity) in enumerate(units):
        (scores, mx), vals = pending.pop(0)
        if u + NA_LOGITS_AHEAD < len(units):
            pending.append(logits(u + NA_LOGITS_AHEAD))
        nums.append(_weighted_values_t(scores, mx, parity, vals))
        if parity == 1:
            o_ref[t * m:(t + 1) * m, s * 128:(s + 1) * 128] = (
                _normalised_pair(nums[-2], nums[-1]).astype(BF16))


def _na_kernel(q_ref, k_ref, v_ref, kc_ref, vc_ref, bias_ref, o_ref, *, rows):
    n_blocks = rows // NA_BLOCK_ROWS

    def operands(t, s, parity):
        blk = pl.program_id(1) * NA_BLOCKS_PER_STEP + t
        band_row = jnp.clip(blk * NA_BLOCK_ROWS - NA_WIN_R // 2, 0, rows - NA_BAND_ROWS)
        band = pl.ds(pl.multiple_of(band_row * GRID_W, GRID_W), NA_BAND_ROWS * GRID_W)
        variant = jnp.where(blk == 0, 0, jnp.where(blk == n_blocks - 1, 2, 1))
        sl = slice(s * 128, (s + 1) * 128)
        return ([k_ref[band, sl], kc_ref[:, sl]], [v_ref[band, sl], vc_ref[:, sl]],
                [bias_ref[variant, 2 * s + parity], None])

    _attend_all_heads(q_ref, o_ref, NA_BLOCKS_PER_STEP, operands)


def _na_bias_table(rpb, rows):
    depth = rpb.shape[0]
    r, band = NA_BLOCK_ROWS, NA_BAND_ROWS
    n_blocks = rows // r
    n_off = 2 * NA_WIN_R - 1
    kcol = np.arange(GRID_W)[:, None]
    qc = np.arange(GRID_W)[None, :]
    c_start = np.clip(qc - NA_WIN_C // 2, 0, GRID_W - NA_WIN_C)
    col_ok = (kcol >= c_start) & (kcol < c_start + NA_WIN_C)
    col_sel = (kcol - qc + NA_WIN_C - 1)[:, :, None] == np.arange(2 * NA_WIN_C - 1)
    tiles = jnp.einsum('lhij,kqj->lihkq', rpb, jnp.asarray(col_sel, F32), precision=lax.Precision.HIGHEST)
    tiles = jnp.where(col_ok, tiles * LOG2E, NEG_INF)
    tile_of = np.full((3, band, r), n_off, np.int64)
    for vi, blk in enumerate((0, 1, n_blocks - 1)):
        r0 = blk * r
        b0 = int(np.clip(r0 - NA_WIN_R // 2, 0, rows - band))
        for ki in range(band):
            for qi in range(r):
                r_start = int(np.clip(r0 + qi - NA_WIN_R // 2, 0, rows - NA_WIN_R))
                if r_start <= b0 + ki < r_start + NA_WIN_R:
                    tile_of[vi, ki, qi] = b0 + ki - (r0 + qi) + NA_WIN_R - 1

    def assemble(tiles_ref, o_ref):
        for vi in range(3):
            for ki in range(band):
                for qi in range(r):
                    dst = (vi, slice(None), slice(ki * GRID_W, (ki + 1) * GRID_W),
                           slice(qi * GRID_W, (qi + 1) * GRID_W))
                    idx = int(tile_of[vi, ki, qi])
                    if idx == n_off:
                        o_ref[dst] = jnp.full((NA_HEADS, GRID_W, GRID_W), NEG_INF, F32)
                    else:
                        o_ref[dst] = tiles_ref[idx]

    return pl.pallas_call(
        assemble,
        out_shape=jax.ShapeDtypeStruct((depth, 3, NA_HEADS, band * GRID_W, r * GRID_W), F32),
        grid=(depth,),
        in_specs=[pl.BlockSpec((None, n_off, NA_HEADS, GRID_W, GRID_W), lambda l: (l, 0, 0, 0, 0))],
        out_specs=pl.BlockSpec((None, 3, NA_HEADS, band * GRID_W, r * GRID_W), lambda l: (l, 0, 0, 0, 0)),
        compiler_params=_params("parallel"),
        name="na_bias",
    )(tiles)


def _na(q, k, v, kc, vc, bias_tabs, layer, rows):
    b, s, w = q.shape
    lc = kc.shape[1]
    n_steps = rows // (NA_BLOCK_ROWS * NA_BLOCKS_PER_STEP)
    tq = NA_BLOCK_ROWS * NA_BLOCKS_PER_STEP * GRID_W
    return pl.pallas_call(
        functools.partial(_na_kernel, rows=rows),
        out_shape=jax.ShapeDtypeStruct((b, s, w), BF16),
        grid=(b, n_steps),
        in_specs=[
            pl.BlockSpec((None, tq, w), lambda i, j: (i, j, 0)),
            pl.BlockSpec((None, s, w), lambda i, j: (i, 0, 0)),
            pl.BlockSpec((None, s, w), lambda i, j: (i, 0, 0)),
            pl.BlockSpec((None, lc, w), lambda i, j: (i, 0, 0)),
            pl.BlockSpec((None, lc, w), lambda i, j: (i, 0, 0)),
            _stacked(bias_tabs, (layer,)),
        ],
        out_specs=pl.BlockSpec((None, tq, w), lambda i, j: (i, j, 0)),
        compiler_params=_params("parallel", "arbitrary"),
        name="na",
    )(q, k, v, kc, vc, bias_tabs)


def _ctx_attn_kernel(q_ref, k_ref, v_ref, o_ref):
    def operands(t, s, parity):
        sl = slice(s * 128, (s + 1) * 128)
        return [k_ref[:, sl]], [v_ref[:, sl]], [None]

    _attend_all_heads(q_ref, o_ref, 1, operands)


def _ctx_attn(q, k, v):
    b, lc, w = q.shape
    spec = pl.BlockSpec((None, lc, w), lambda i: (i, 0, 0))
    return pl.pallas_call(
        _ctx_attn_kernel,
        out_shape=jax.ShapeDtypeStruct((b, lc, w), BF16),
        grid=(b,),
        in_specs=[spec, spec, spec],
        out_specs=spec,
        compiler_params=_params("parallel"),
        name="ctx_attn",
    )(q, k, v)


def _ret_kernel(qf_ref, kf_ref, vf_ref, qb_ref, kb_ref, vb_ref, ks_ref, vs_ref, ws_ref, dm_ref, te_ref,
                fs_ref, cd_ref, of_ref, ob_ref, state_ref, *, n_chunks):
    n = pl.program_id(1)
    same_head = ((lax.broadcasted_iota(jnp.int32, (128, 128), 0) >> 6)
                 == (lax.broadcasted_iota(jnp.int32, (128, 128), 1) >> 6))
    even = lax.broadcasted_iota(jnp.int32, (CHUNK, 128), 1) < HEAD_DIM

    @pl.when(n == 0)
    def _seed():
        for d in range(2):
            for s in range(RET_WIDTH // 128):
                sl = slice(s * 128, (s + 1) * 128)
                kw = (ks_ref[:, sl].astype(F32) * ws_ref[d, :, sl]).astype(BF16)
                state_ref[d, s] = jnp.where(same_head, _dot_tn(kw, vs_ref[:, sl]), 0.0)

    streams = ((qf_ref, kf_ref, vf_ref, of_ref), (qb_ref, kb_ref, vb_ref, ob_ref))
    chunk = lambda c: slice(c * CHUNK, (c + 1) * CHUNK)
    units = []
    for s in range(RET_WIDTH // 128):
        sl = slice(s * 128, (s + 1) * 128)
        for d, (_, k_ref, v_ref, _) in enumerate(streams):
            order = [j if d == 0 else n_chunks - 1 - j for j in range(n_chunks)]
            kvs = []
            for c in order:
                kt = (k_ref[chunk(c), sl].astype(F32) * te_ref[d, :, sl]).astype(BF16)
                kvs.append(jnp.where(same_head, _dot_tn(kt, v_ref[chunk(c), sl]), 0.0))
            st = state_ref[d, s]
            for c, kv in zip(order, kvs):
                units.append((s, d, c, st))
                st = cd_ref[d, s] * st + kv
            state_ref[d, s] = st

    def decayed_scores(u):
        s, d, c, _ = units[u]
        sl = slice(s * 128, (s + 1) * 128)
        q = streams[d][0][chunk(c), sl]
        k = streams[d][1][chunk(c), sl]
        zero = jnp.zeros_like(k)
        k2 = jnp.concatenate([jnp.where(even, k, zero), jnp.where(even, zero, k)], axis=0)
        return (_dot_nt(q, k2) * dm_ref[d, s]).astype(BF16)

    pending = [decayed_scores(u) for u in range(min(RET_SCORES_AHEAD, len(units)))]
    for u, (s, d, c, st) in enumerate(units):
        sl = slice(s * 128, (s + 1) * 128)
        scores = pending.pop(0)
        if u + RET_SCORES_AHEAD < len(units):
            pending.append(decayed_scores(u + RET_SCORES_AHEAD))
        q_ref, _, v_ref, o_ref = streams[d]
        v = v_ref[chunk(c), sl]
        zero = jnp.zeros_like(v)
        v2 = jnp.concatenate([jnp.where(even, v, zero), jnp.where(even, zero, v)], axis=0)
        qf = (q_ref[chunk(c), sl].astype(F32) * fs_ref[d, :, sl]).astype(BF16)
        o_ref[chunk(c), sl] = _dot(scores, v2) + _dot(qf, st.astype(BF16))


def _ret_tables(log_g, seed_len, seeded):
    pos = jnp.arange(CHUNK, dtype=F32)
    diff = pos[:, None] - pos[None, :]
    lg = log_g[:, :, None, None]
    dm_f = jnp.where(diff >= 0, jnp.exp(jnp.maximum(diff, 0.0)[None, None] * lg), 0.0)[0]
    dm_b = jnp.where(diff <= 0, jnp.exp(jnp.maximum(-diff, 0.0)[None, None] * lg), 0.0)[1]
    dm = jnp.stack([dm_f, dm_b])
    dm = dm.reshape(2, RET_HEADS // 2, 2, CHUNK, CHUNK).transpose(0, 1, 3, 2, 4)
    dm = dm.reshape(2, RET_HEADS // 2, CHUNK, 2 * CHUNK)
    lane = lambda t: jnp.repeat(t, HEAD_DIM, axis=-1)
    te = jnp.stack([jnp.exp((CHUNK - 1 - pos)[:, None] * log_g[0][None, :]),
                    jnp.exp(pos[:, None] * log_g[1][None, :])])
    fs = jnp.stack([jnp.exp((pos + 1.0)[:, None] * log_g[0][None, :]),
                    jnp.exp((CHUNK - pos)[:, None] * log_g[1][None, :])])
    cd = jnp.broadcast_to(lane(jnp.exp(CHUNK * log_g)).reshape(2, RET_HEADS // 2, 128, 1),
                          (2, RET_HEADS // 2, 128, 128))
    sp = jnp.arange(seed_len, dtype=F32)
    ws = jnp.stack([jnp.exp((seed_len - 1 - sp)[:, None] * log_g[0][None, :]),
                    jnp.exp(sp[:, None] * log_g[1][None, :])])
    if not seeded:
        ws = jnp.zeros_like(ws)
    return dm, lane(te), lane(fs), cd, lane(ws)


def _retention(q, k, v, k_seed, v_seed, tables, tb):
    b, l, w = q.shape
    ls = k_seed.shape[1]
    nb = l // tb
    dm, te, fs, cd, ws = tables
    fwd = pl.BlockSpec((None, tb, w), lambda i, n: (i, n, 0))
    bwd = pl.BlockSpec((None, tb, w), lambda i, n: (i, nb - 1 - n, 0))
    seed = pl.BlockSpec((None, ls, w), lambda i, n: (i, 0, 0))
    const = lambda shape: _resident(shape, lambda i, n: (0,) * len(shape))
    out = jax.ShapeDtypeStruct((b, l, w), F32)
    return pl.pallas_call(
        functools.partial(_ret_kernel, n_chunks=tb // CHUNK),
        out_shape=(out, out),
        grid=(b, nb),
        in_specs=[
            fwd, fwd, fwd, bwd, bwd, bwd, seed, seed,
            const(ws.shape), const(dm.shape), const(te.shape), const(fs.shape), const(cd.shape),
        ],
        out_specs=(fwd, bwd),
        scratch_shapes=[pltpu.VMEM((2, RET_HEADS // 2, 128, 128), F32)],
        compiler_params=_params("parallel", "arbitrary"),
        name="retention",
    )(q, k, v, q, k, v, k_seed, v_seed, ws, dm, te, fs, cd)


def _head_group_norm(o, ones_blk):
    mu = _seg_sum(o, ones_blk) * (1.0 / HEAD_DIM)
    dlt = o - mu
    var = _seg_sum(dlt * dlt, ones_blk) * (1.0 / HEAD_DIM)
    return dlt * lax.rsqrt(var + LN_EPS)


def _mix_out_ffn_kernel(x_ref, mod_ref, na_ref, sg_ref, of_ref, ob_ref, gf_ref, gb_ref, gnw_ref,
                        w_ref, nw_ref, w1_ref, w2_ref, o_ref):
    seg_ones = _seg_ones()
    y = (_head_group_norm(of_ref[...], seg_ones) * gf_ref[...]
         + _head_group_norm(ob_ref[...], seg_ones) * gb_ref[...])
    ret = (y * gnw_ref[...]).astype(BF16)
    o_sg = NA_WIDTH + SG_WIDTH
    out = (_dot(na_ref[...], w_ref[0:NA_WIDTH, :])
           + _dot(sg_ref[...], w_ref[NA_WIDTH:o_sg, :])
           + _dot(ret, w_ref[o_sg:, :]))
    x = x_ref[...] + mod_ref[5:6, :] * out
    o_ref[...] = _ffn_half_step(x, mod_ref, nw_ref, w1_ref, w2_ref, 6)


def _mix_out_ffn(x, mod, na, sg, o_fwd, o_bwd, gf, gb, gnw, w_out, nw, w1, w2, layer, tm):
    b, l, d = x.shape
    mod_map = (lambda i, j: (i, 0, 0)) if mod.shape[0] == b else (lambda i, j: (0, 0, 0))
    tok = lambda w: pl.BlockSpec((None, tm, w), lambda i, j: (i, j, 0))
    const = lambda shape: _resident(shape, lambda i, j: (0,) * len(shape))
    return pl.pallas_call(
        _mix_out_ffn_kernel,
        out_shape=jax.ShapeDtypeStruct(x.shape, x.dtype),
        grid=(b, l // tm),
        in_specs=[
            tok(d),
            pl.BlockSpec((None, N_MOD, d), mod_map),
            tok(NA_WIDTH), tok(SG_WIDTH), tok(RET_WIDTH), tok(RET_WIDTH), tok(RET_WIDTH), tok(RET_WIDTH),
            const((1, RET_WIDTH)), _stacked(w_out, (layer,)),
            const((1, d)), _stacked(w1, (layer, 1)), _stacked(w2, (layer, 1)),
        ],
        out_specs=tok(d),
        compiler_params=_params("parallel", "parallel"),
        name="mix_out_ffn",
    )(x, mod, na, sg, o_fwd, o_bwd, gf, gb, gnw, w_out, nw.reshape(1, d), w1, w2)


def _rope_tables(n_lat):
    t = jnp.arange(n_lat)[:, None]
    lane = np.arange(128)
    inv = ROPE_BASE ** (-jnp.asarray(lane % ROPE_FREQS, F32) / ROPE_FREQS)
    pos = jnp.where(lane % HEAD_DIM < HEAD_DIM // 2, t // GRID_W, t % GRID_W).astype(F32)
    ang = pos * inv[None, :]
    first = (lane // ROPE_FREQS) % 2 == 0
    sin = jnp.sin(ang)
    return jnp.cos(ang), jnp.where(first, -sin, 0.0), jnp.where(first, 0.0, sin)


def kernel(x, c, ctx, c_ctx, ada_w, ada_b, norm_w, ffn_w1, ffn_w2, mix_w_in, mix_w_out, na_q_norm,
           na_k_norm, na_rpb, sg_w, sg_b, sg_ln_w, sg_ln_b, ret_decay_logit, ret_gn_w):
    b, n_lat, d = x.shape
    lc = ctx.shape[1]
    depth = ada_w.shape[0]
    rows = n_lat // GRID_W
    assert d == D_MODEL and n_lat % (GRID_W * NA_BLOCK_ROWS * NA_BLOCKS_PER_STEP) == 0
    assert rows >= 2 * NA_BAND_ROWS - NA_WIN_R
    assert lc % CHUNK == 0
    tm_x = min(512, n_lat)
    tm_c = min(256, lc)
    tb_x = min(1024, n_lat)
    assert n_lat % tb_x == 0
    assert n_lat % tm_x == 0 and lc % tm_c == 0 and tm_x % CHUNK == 0 and tm_c % CHUNK == 0

    cc = jnp.zeros((16, d), F32).at[:b].set(c).at[b].set(c_ctx)
    mod = _adaln(cc, ada_w, ada_b)

    rope_x = _rope_tables(n_lat)
    rope_c = (jnp.ones((lc, 128), F32), jnp.zeros((lc, 128), F32), jnp.zeros((lc, 128), F32))
    w1 = ffn_w1.astype(BF16)
    w2 = ffn_w2.astype(BF16)
    w_in = mix_w_in.astype(BF16)
    w_out = mix_w_out.astype(BF16)
    sgw = sg_w.astype(BF16).transpose(0, 2, 1, 3).reshape(depth, CHUNK, SG_GROUPS * CHUNK)
    log_g = jax.nn.log_sigmoid(ret_decay_logit.astype(F32))
    bias_tabs = _na_bias_table(na_rpb, rows)

    for layer in range(depth):
        last = layer == depth - 1
        mod_x = mod[layer, :b].reshape(b, N_MOD, d)
        mod_c = mod[layer, b:b + 1].reshape(1, N_MOD, d)
        nw = norm_w[layer]
        x = _ffn(x, mod_x, 0, nw[0], w1, w2, (layer, 0), tm_x)
        ctx = _ffn(ctx, mod_c, 0, nw[0], w1, w2, (layer, 0), tm_c)

        qn = jnp.tile(na_q_norm[layer], NA_HEADS).reshape(1, NA_WIDTH)
        kn = jnp.tile(na_k_norm[layer], NA_HEADS).reshape(1, NA_WIDTH)
        sgb_tab = jnp.repeat(sg_b[layer].T, HEAD_DIM, axis=1)
        lnw = sg_ln_w[layer].reshape(1, SG_WIDTH)
        lnb = sg_ln_b[layer].reshape(1, SG_WIDTH)
        gnw = ret_gn_w[layer].reshape(1, RET_WIDTH)
        px = _mix_in(x, mod_x, nw[1], w_in, layer, qn, kn, rope_x, sgw[layer], sgb_tab, lnw, lnb, tm_x)
        pc = _mix_in(ctx, mod_c, nw[1], w_in, layer, qn, kn, rope_c, sgw[layer], sgb_tab, lnw, lnb, tm_c)
        nq_x, nk_x, nv_x, sg_x, rq_x, rk_x, rv_x, gf_x, gb_x = px
        nq_c, nk_c, nv_c, sg_c, rq_c, rk_c, rv_c, gf_c, gb_c = pc

        na_x = _na(nq_x, nk_x, nv_x, nk_c, nv_c, bias_tabs, layer, rows)
        of_x, ob_x = _retention(rq_x, rk_x, rv_x, rk_c, rv_c, _ret_tables(log_g[layer], lc, True), tb_x)
        x = _mix_out_ffn(x, mod_x, na_x, sg_x, of_x, ob_x, gf_x, gb_x, gnw, w_out, nw[2], w1, w2,
                         layer, tm_x)
        if not last:
            na_c = _ctx_attn(nq_c, nk_c, nv_c)
            of_c, ob_c = _retention(rq_c, rk_c, rv_c, rk_c, rv_c, _ret_tables(log_g[layer], lc, False),
                                    tm_c)
            ctx = _mix_out_ffn(ctx, mod_c, na_c, sg_c, of_c, ob_c, gf_c, gb_c, gnw, w_out, nw[2], w1, w2,
                               layer, tm_c)
    return x
```

```python
import functools

import numpy as np
import jax
import jax.numpy as jnp
from jax import lax
from jax.experimental import pallas as pl
from jax.experimental.pallas import tpu as pltpu

D_MODEL = 1024
GRID_W = 64
N_MOD = 9
D_FF = 2816
NA_HEADS = 6
HEAD_DIM = 64
NA_WIDTH = NA_HEADS * HEAD_DIM
NA_WIN_R = 8
NA_WIN_C = 16
SG_GROUPS = 4
SG_WIDTH = SG_GROUPS * HEAD_DIM
CHUNK = 128
RET_HEADS = 6
RET_WIDTH = RET_HEADS * HEAD_DIM
IN_COLS = 3 * NA_WIDTH + 2 * SG_WIDTH + 5 * RET_WIDTH
ROPE_BASE = 10000.0
ROPE_FREQS = HEAD_DIM // 4
RMS_EPS = 1e-6
LN_EPS = 1e-5
NEG_INF = -1e30
LOG2E = 1.4426950408889634

OFF_NQ, OFF_NK, OFF_NV = 0, NA_WIDTH, 2 * NA_WIDTH
OFF_SU = 3 * NA_WIDTH
OFF_SV = OFF_SU + SG_WIDTH
OFF_RQ = OFF_SV + SG_WIDTH
OFF_RK = OFF_RQ + RET_WIDTH
OFF_RV = OFF_RK + RET_WIDTH
OFF_GF = OFF_RV + RET_WIDTH
OFF_GB = OFF_GF + RET_WIDTH

V7X_VMEM_LIMIT_BYTES = 56 * 1024 * 1024
NA_BLOCK_ROWS = 4
NA_BAND_ROWS = NA_BLOCK_ROWS + NA_WIN_R
NA_LOGITS_AHEAD = 2
NA_BLOCKS_PER_STEP = 8
RET_SCORES_AHEAD = 5
FFN_CHUNKS = ((0, 1536), (1536, 1280))

BF16 = jnp.bfloat16
F32 = jnp.float32


def _params(*sem):
    return pltpu.CompilerParams(dimension_semantics=sem, vmem_limit_bytes=V7X_VMEM_LIMIT_BYTES)


def _resident(shape, index_map):
    return pl.BlockSpec(shape, index_map, pipeline_mode=pl.Buffered(1))


def _stacked(w, lead):
    n = len(lead)
    return _resident((None,) * n + w.shape[n:], lambda i, j: tuple(lead) + (0,) * (w.ndim - n))


def _dot(a, b):
    return jnp.dot(a, b, preferred_element_type=F32)


def _dot_nt(a, b):
    return lax.dot_general(a, b, (((1,), (1,)), ((), ())), preferred_element_type=F32)


def _dot_tn(a, b):
    return lax.dot_general(a, b, (((0,), (0,)), ((), ())), preferred_element_type=F32)


def _seg_ones():
    r = lax.broadcasted_iota(jnp.int32, (256, 128), 0)
    c = lax.broadcasted_iota(jnp.int32, (256, 128), 1)
    return jnp.where(((r >> 6) & 1) == (c >> 6), 1.0, 0.0).astype(BF16)


def _seg_sum(t, seg_ones):
    hi = t.astype(BF16)
    lo = (t - hi.astype(F32)).astype(BF16)
    outs = []
    for s in range(t.shape[1] // 128):
        sl = slice(s * 128, (s + 1) * 128)
        outs.append(_dot(jnp.concatenate([hi[:, sl], lo[:, sl]], axis=1), seg_ones))
    return jnp.concatenate(outs, axis=1)


def _ada_rms(x, nw, shift, scale):
    ms = jnp.mean(x * x, axis=-1, keepdims=True)
    return (x * lax.rsqrt(ms + RMS_EPS)) * nw * (1.0 + scale) + shift


def _adaln_kernel(c_ref, w_ref, b_ref, o_ref):
    a = jax.nn.silu(c_ref[...])
    o_ref[...] = jnp.dot(a, w_ref[...], precision=lax.Precision.HIGHEST,
                         preferred_element_type=F32) + b_ref[...]


def _adaln(cc, ada_w, ada_b):
    depth = ada_w.shape[0]
    rows = cc.shape[0]
    tn = D_MODEL
    return pl.pallas_call(
        _adaln_kernel,
        out_shape=jax.ShapeDtypeStruct((depth, rows, N_MOD * D_MODEL), F32),
        grid=(depth, N_MOD * D_MODEL // tn),
        in_specs=[
            pl.BlockSpec((rows, D_MODEL), lambda l, j: (0, 0)),
            pl.BlockSpec((None, D_MODEL, tn), lambda l, j: (l, 0, j)),
            pl.BlockSpec((None, 1, tn), lambda l, j: (l, 0, j)),
        ],
        out_specs=pl.BlockSpec((None, rows, tn), lambda l, j: (l, 0, j)),
        compiler_params=_params("arbitrary", "arbitrary"),
        name="adaln",
    )(cc, ada_w, ada_b.reshape(depth, 1, N_MOD * D_MODEL))


def _ffn_half_step(x, mod_ref, nw_ref, w1_ref, w2_ref, mod_off):
    shift = mod_ref[mod_off:mod_off + 1, :]
    scale = mod_ref[mod_off + 1:mod_off + 2, :]
    gate = mod_ref[mod_off + 2:mod_off + 3, :]
    xb = _ada_rms(x, nw_ref[...], shift, scale).astype(BF16)
    acc = None
    for off, width in FFN_CHUNKS:
        g = _dot(xb, w1_ref[:, off:off + width])
        u = _dot(xb, w1_ref[:, D_FF + off:D_FF + off + width])
        a = (jax.nn.silu(g) * u).astype(BF16)
        part = _dot(a, w2_ref[off:off + width, :])
        acc = part if acc is None else acc + part
    return x + (0.5 * gate) * acc


def _ffn_kernel(x_ref, mod_ref, nw_ref, w1_ref, w2_ref, o_ref, *, mod_off):
    o_ref[...] = _ffn_half_step(x_ref[...], mod_ref, nw_ref, w1_ref, w2_ref, mod_off)


def _ffn(x, mod, mod_off, nw, w1, w2, lead, tm):
    b, l, d = x.shape
    mod_map = (lambda i, j: (i, 0, 0)) if mod.shape[0] == b else (lambda i, j: (0, 0, 0))
    return pl.pallas_call(
        functools.partial(_ffn_kernel, mod_off=mod_off),
        out_shape=jax.ShapeDtypeStruct(x.shape, x.dtype),
        grid=(b, l // tm),
        in_specs=[
            pl.BlockSpec((None, tm, d), lambda i, j: (i, j, 0)),
            pl.BlockSpec((None, N_MOD, d), mod_map),
            _resident((1, d), lambda i, j: (0, 0)),
            _stacked(w1, lead),
            _stacked(w2, lead),
        ],
        out_specs=pl.BlockSpec((None, tm, d), lambda i, j: (i, j, 0)),
        compiler_params=_params("parallel", "parallel"),
        name="ffn",
    )(x, mod, nw.reshape(1, d), w1, w2)


def _head_rms(t, w, ones_blk):
    ms = _seg_sum(t * t, ones_blk) * (1.0 / HEAD_DIM)
    return (t * lax.rsqrt(ms + RMS_EPS)) * w


def _rope(t, cos, sin_next, sin_prev):
    outs = []
    for s in range(RET_WIDTH // 128):
        ts = t[:, s * 128:(s + 1) * 128]
        nxt = pltpu.roll(ts, 128 - ROPE_FREQS, axis=1)
        prv = pltpu.roll(ts, ROPE_FREQS, axis=1)
        outs.append(ts * cos + nxt * sin_next + prv * sin_prev)
    return jnp.concatenate(outs, axis=1)


def _mix_in_kernel(x_ref, mod_ref, nw_ref, w_ref, qn_ref, kn_ref, cos_ref, sn_ref, sp_ref,
                   sgw_ref, sgb_ref, lnw_ref, lnb_ref,
                   nq_ref, nk_ref, nv_ref, sg_ref, rq_ref, rk_ref, rv_ref, gf_ref, gb_ref):
    tm = x_ref.shape[0]
    x = x_ref[...]
    hb = _ada_rms(x, nw_ref[...], mod_ref[3:4, :], mod_ref[4:5, :]).astype(BF16)
    p = _dot(hb, w_ref[...])
    seg_ones = _seg_ones()
    scale = HEAD_DIM ** -0.5

    nq_ref[...] = (_head_rms(p[:, OFF_NQ:OFF_NQ + NA_WIDTH], qn_ref[...], seg_ones)
                   * (scale * LOG2E)).astype(BF16)
    nk_ref[...] = _head_rms(p[:, OFF_NK:OFF_NK + NA_WIDTH], kn_ref[...], seg_ones).astype(BF16)
    nv_ref[...] = p[:, OFF_NV:OFF_NV + NA_WIDTH].astype(BF16)

    u = jax.nn.gelu(p[:, OFF_SU:OFF_SU + SG_WIDTH])
    v = jax.nn.gelu(p[:, OFF_SV:OFF_SV + SG_WIDTH])
    mu = _seg_sum(v, seg_ones) * (1.0 / HEAD_DIM)
    dv = v - mu
    var = _seg_sum(dv * dv, seg_ones) * (1.0 / HEAD_DIM)
    vn = ((dv * lax.rsqrt(var + LN_EPS)) * lnw_ref[...] + lnb_ref[...]).astype(BF16)
    lane_grp = lax.broadcasted_iota(jnp.int32, (CHUNK, SG_WIDTH), 1) >> 6
    for j in range(tm // CHUNK):
        vch = vn[j * CHUNK:(j + 1) * CHUNK, :]
        stacked = jnp.concatenate(
            [jnp.where(lane_grp == g, vch, jnp.zeros_like(vch)) for g in range(SG_GROUPS)], axis=0)
        mixed = _dot(sgw_ref[...], stacked) + sgb_ref[...]
        sg_ref[j * CHUNK:(j + 1) * CHUNK, :] = (u[j * CHUNK:(j + 1) * CHUNK, :] * mixed).astype(BF16)

    cos, sn, sp = cos_ref[...], sn_ref[...], sp_ref[...]
    rq_ref[...] = _rope(p[:, OFF_RQ:OFF_RQ + RET_WIDTH], cos, sn, sp).astype(BF16)
    rk_ref[...] = (_rope(p[:, OFF_RK:OFF_RK + RET_WIDTH], cos, sn, sp) * scale).astype(BF16)
    rv_ref[...] = p[:, OFF_RV:OFF_RV + RET_WIDTH].astype(BF16)
    gf_ref[...] = jax.nn.silu(p[:, OFF_GF:OFF_GF + RET_WIDTH])
    gb_ref[...] = jax.nn.silu(p[:, OFF_GB:OFF_GB + RET_WIDTH])


def _mix_in(x, mod, nw, w_in, layer, qn, kn, rope_tabs, sgw, sgb_tab, lnw, lnb, tm):
    b, l, d = x.shape
    mod_map = (lambda i, j: (i, 0, 0)) if mod.shape[0] == b else (lambda i, j: (0, 0, 0))
    tok = lambda w: pl.BlockSpec((None, tm, w), lambda i, j: (i, j, 0))
    tab = pl.BlockSpec((tm, 128), lambda i, j: (j, 0))
    const = lambda shape: _resident(shape, lambda i, j: (0,) * len(shape))
    out_bf = lambda w: jax.ShapeDtypeStruct((b, l, w), BF16)
    out_f32 = lambda w: jax.ShapeDtypeStruct((b, l, w), F32)
    return pl.pallas_call(
        _mix_in_kernel,
        out_shape=(out_bf(NA_WIDTH), out_bf(NA_WIDTH), out_bf(NA_WIDTH), out_bf(SG_WIDTH),
                   out_bf(RET_WIDTH), out_bf(RET_WIDTH), out_bf(RET_WIDTH),
                   out_f32(RET_WIDTH), out_f32(RET_WIDTH)),
        grid=(b, l // tm),
        in_specs=[
            tok(d),
            pl.BlockSpec((None, N_MOD, d), mod_map),
            const((1, d)),
            _stacked(w_in, (layer,)),
            const((1, NA_WIDTH)), const((1, NA_WIDTH)),
            tab, tab, tab,
            const((CHUNK, SG_GROUPS * CHUNK)), const((CHUNK, SG_WIDTH)),
            const((1, SG_WIDTH)), const((1, SG_WIDTH)),
        ],
        out_specs=(tok(NA_WIDTH), tok(NA_WIDTH), tok(NA_WIDTH), tok(SG_WIDTH),
                   tok(RET_WIDTH), tok(RET_WIDTH), tok(RET_WIDTH), tok(RET_WIDTH), tok(RET_WIDTH)),
        compiler_params=_params("parallel", "parallel"),
        name="mix_in",
    )(x, mod, nw.reshape(1, d), w_in, qn, kn, *rope_tabs, sgw, sgb_tab, lnw, lnb)


def _own_lanes(shape, parity):
    even = lax.broadcasted_iota(jnp.int32, shape, 1) < HEAD_DIM
    return even if parity == 0 else jnp.logical_not(even)


def _scores_t(q, parity, keys, biases):
    qp = jnp.where(_own_lanes(q.shape, parity), q, jnp.zeros_like(q))
    scores = [_dot_nt(kk, qp) if bias is None else _dot_nt(kk, qp) + bias for kk, bias in zip(keys, biases)]
    mx = functools.reduce(jnp.maximum, [jnp.max(s, axis=0, keepdims=True) for s in scores])
    return scores, mx


def _weighted_values_t(scores, mx, parity, vals):
    num = None
    for s, vv in zip(scores, vals):
        vp = jnp.where(_own_lanes(vv.shape, parity), vv, jnp.ones_like(vv))
        part = _dot_tn(vp, jnp.exp2((s - mx).astype(BF16)))
        num = part if num is None else num + part
    return num


def _normalised_pair(num_even, num_odd):
    even_row = lax.broadcasted_iota(jnp.int32, num_even.shape, 0) < HEAD_DIM
    out_t = jnp.where(even_row, num_even / num_even[HEAD_DIM:HEAD_DIM + 1, :], num_odd / num_odd[0:1, :])
    return out_t.T


def _attend_all_heads(q_ref, o_ref, n_sub, operands):
    m = q_ref.shape[0] // n_sub
    units = [(t, s, parity) for t in range(n_sub) for s in range(NA_WIDTH // 128) for parity in range(2)]

    def logits(u):
        t, s, parity = units[u]
        keys, vals, biases = operands(t, s, parity)
        return _scores_t(q_ref[t * m:(t + 1) * m, s * 128:(s + 1) * 128], parity, keys, biases), vals

    pending = [logits(u) for u in range(NA_LOGITS_AHEAD)]
    nums = []
    for u, (t, s, parity) in enumerate(units):
        (scores, mx), vals = pending.pop(0)
        if u + NA_LOGITS_AHEAD < len(units):
            pending.append(logits(u + NA_LOGITS_AHEAD))
        nums.append(_weighted_values_t(scores, mx, parity, vals))
        if parity == 1:
            o_ref[t * m:(t + 1) * m, s * 128:(s + 1) * 128] = (
                _normalised_pair(nums[-2], nums[-1]).astype(BF16))


def _na_kernel(q_ref, k_ref, v_ref, kc_ref, vc_ref, bias_ref, o_ref, *, rows):
    n_blocks = rows // NA_BLOCK_ROWS

    def operands(t, s, parity):
        blk = pl.program_id(1) * NA_BLOCKS_PER_STEP + t
        band_row = jnp.clip(blk * NA_BLOCK_ROWS - NA_WIN_R // 2, 0, rows - NA_BAND_ROWS)
        band = pl.ds(pl.multiple_of(band_row * GRID_W, GRID_W), NA_BAND_ROWS * GRID_W)
        variant = jnp.where(blk == 0, 0, jnp.where(blk == n_blocks - 1, 2, 1))
        sl = slice(s * 128, (s + 1) * 128)
        return ([k_ref[band, sl], kc_ref[:, sl]], [v_ref[band, sl], vc_ref[:, sl]],
                [bias_ref[variant, 2 * s + parity], None])

    _attend_all_heads(q_ref, o_ref, NA_BLOCKS_PER_STEP, operands)


def _na_bias_table(rpb, rows):
    depth = rpb.shape[0]
    r, band = NA_BLOCK_ROWS, NA_BAND_ROWS
    n_blocks = rows // r
    n_off = 2 * NA_WIN_R - 1
    kcol = np.arange(GRID_W)[:, None]
    qc = np.arange(GRID_W)[None, :]
    c_start = np.clip(qc - NA_WIN_C // 2, 0, GRID_W - NA_WIN_C)
    col_ok = (kcol >= c_start) & (kcol < c_start + NA_WIN_C)
    col_sel = (kcol - qc + NA_WIN_C - 1)[:, :, None] == np.arange(2 * NA_WIN_C - 1)
    tiles = jnp.einsum('lhij,kqj->lihkq', rpb, jnp.asarray(col_sel, F32), precision=lax.Precision.HIGHEST)
    tiles = jnp.where(col_ok, tiles * LOG2E, NEG_INF)
    tile_of = np.full((3, band, r), n_off, np.int64)
    for vi, blk in enumerate((0, 1, n_blocks - 1)):
        r0 = blk * r
        b0 = int(np.clip(r0 - NA_WIN_R // 2, 0, rows - band))
        for ki in range(band):
            for qi in range(r):
                r_start = int(np.clip(r0 + qi - NA_WIN_R // 2, 0, rows - NA_WIN_R))
                if r_start <= b0 + ki < r_start + NA_WIN_R:
                    tile_of[vi, ki, qi] = b0 + ki - (r0 + qi) + NA_WIN_R - 1

    def assemble(tiles_ref, o_ref):
        for vi in range(3):
            for ki in range(band):
                for qi in range(r):
                    dst = (vi, slice(None), slice(ki * GRID_W, (ki + 1) * GRID_W),
                           slice(qi * GRID_W, (qi + 1) * GRID_W))
                    idx = int(tile_of[vi, ki, qi])
                    if idx == n_off:
                        o_ref[dst] = jnp.full((NA_HEADS, GRID_W, GRID_W), NEG_INF, F32)
                    else:
                        o_ref[dst] = tiles_ref[idx]

    return pl.pallas_call(
        assemble,
        out_shape=jax.ShapeDtypeStruct((depth, 3, NA_HEADS, band * GRID_W, r * GRID_W), F32),
        grid=(depth,),
        in_specs=[pl.BlockSpec((None, n_off, NA_HEADS, GRID_W, GRID_W), lambda l: (l, 0, 0, 0, 0))],
        out_specs=pl.BlockSpec((None, 3, NA_HEADS, band * GRID_W, r * GRID_W), lambda l: (l, 0, 0, 0, 0)),
        compiler_params=_params("parallel"),
        name="na_bias",
    )(tiles)


def _na(q, k, v, kc, vc, bias_tabs, layer, rows):
    b, s, w = q.shape
    lc = kc.shape[1]
    n_steps = rows // (NA_BLOCK_ROWS * NA_BLOCKS_PER_STEP)
    tq = NA_BLOCK_ROWS * NA_BLOCKS_PER_STEP * GRID_W
    return pl.pallas_call(
        functools.partial(_na_kernel, rows=rows),
        out_shape=jax.ShapeDtypeStruct((b, s, w), BF16),
        grid=(b, n_steps),
        in_specs=[
            pl.BlockSpec((None, tq, w), lambda i, j: (i, j, 0)),
            pl.BlockSpec((None, s, w), lambda i, j: (i, 0, 0)),
            pl.BlockSpec((None, s, w), lambda i, j: (i, 0, 0)),
            pl.BlockSpec((None, lc, w), lambda i, j: (i, 0, 0)),
            pl.BlockSpec((None, lc, w), lambda i, j: (i, 0, 0)),
            _stacked(bias_tabs, (layer,)),
        ],
        out_specs=pl.BlockSpec((None, tq, w), lambda i, j: (i, j, 0)),
        compiler_params=_params("parallel", "arbitrary"),
        name="na",
    )(q, k, v, kc, vc, bias_tabs)


def _ctx_attn_kernel(q_ref, k_ref, v_ref, o_ref):
    def operands(t, s, parity):
        sl = slice(s * 128, (s + 1) * 128)
        return [k_ref[:, sl]], [v_ref[:, sl]], [None]

    _attend_all_heads(q_ref, o_ref, 1, operands)


def _ctx_attn(q, k, v):
    b, lc, w = q.shape
    spec = pl.BlockSpec((None, lc, w), lambda i: (i, 0, 0))
    return pl.pallas_call(
        _ctx_attn_kernel,
        out_shape=jax.ShapeDtypeStruct((b, lc, w), BF16),
        grid=(b,),
        in_specs=[spec, spec, spec],
        out_specs=spec,
        compiler_params=_params("parallel"),
        name="ctx_attn",
    )(q, k, v)


def _ret_kernel(qf_ref, kf_ref, vf_ref, qb_ref, kb_ref, vb_ref, ks_ref, vs_ref, ws_ref, dm_ref, te_ref,
                fs_ref, cd_ref, of_ref, ob_ref, state_ref, *, n_chunks):
    n = pl.program_id(1)
    same_head = ((lax.broadcasted_iota(jnp.int32, (128, 128), 0) >> 6)
                 == (lax.broadcasted_iota(jnp.int32, (128, 128), 1) >> 6))
    even = lax.broadcasted_iota(jnp.int32, (CHUNK, 128), 1) < HEAD_DIM

    @pl.when(n == 0)
    def _seed():
        for d in range(2):
            for s in range(RET_WIDTH // 128):
                sl = slice(s * 128, (s + 1) * 128)
                kw = (ks_ref[:, sl].astype(F32) * ws_ref[d, :, sl]).astype(BF16)
                state_ref[d, s] = jnp.where(same_head, _dot_tn(kw, vs_ref[:, sl]), 0.0)

    streams = ((qf_ref, kf_ref, vf_ref, of_ref), (qb_ref, kb_ref, vb_ref, ob_ref))
    chunk = lambda c: slice(c * CHUNK, (c + 1) * CHUNK)
    units = []
    for s in range(RET_WIDTH // 128):
        sl = slice(s * 128, (s + 1) * 128)
        for d, (_, k_ref, v_ref, _) in enumerate(streams):
            order = [j if d == 0 else n_chunks - 1 - j for j in range(n_chunks)]
            kvs = []
            for c in order:
                kt = (k_ref[chunk(c), sl].astype(F32) * te_ref[d, :, sl]).astype(BF16)
                kvs.append(jnp.where(same_head, _dot_tn(kt, v_ref[chunk(c), sl]), 0.0))
            st = state_ref[d, s]
            for c, kv in zip(order, kvs):
                units.append((s, d, c, st))
                st = cd_ref[d, s] * st + kv
            state_ref[d, s] = st

    def decayed_scores(u):
        s, d, c, _ = units[u]
        sl = slice(s * 128, (s + 1) * 128)
        q = streams[d][0][chunk(c), sl]
        k = streams[d][1][chunk(c), sl]
        zero = jnp.zeros_like(k)
        k2 = jnp.concatenate([jnp.where(even, k, zero), jnp.where(even, zero, k)], axis=0)
        return (_dot_nt(q, k2) * dm_ref[d, s]).astype(BF16)

    pending = [decayed_scores(u) for u in range(min(RET_SCORES_AHEAD, len(units)))]
    for u, (s, d, c, st) in enumerate(units):
        sl = slice(s * 128, (s + 1) * 128)
        scores = pending.pop(0)
        if u + RET_SCORES_AHEAD < len(units):
            pending.append(decayed_scores(u + RET_SCORES_AHEAD))
        q_ref, _, v_ref, o_ref = streams[d]
        v = v_ref[chunk(c), sl]
        zero = jnp.zeros_like(v)
        v2 = jnp.concatenate([jnp.where(even, v, zero), jnp.where(even, zero, v)], axis=0)
        qf = (q_ref[chunk(c), sl].astype(F32) * fs_ref[d, :, sl]).astype(BF16)
        o_ref[chunk(c), sl] = _dot(scores, v2) + _dot(qf, st.astype(BF16))


def _ret_tables(log_g, seed_len, seeded):
    pos = jnp.arange(CHUNK, dtype=F32)
    diff = pos[:, None] - pos[None, :]
    lg = log_g[:, :, None, None]
    dm_f = jnp.where(diff >= 0, jnp.exp(jnp.maximum(diff, 0.0)[None, None] * lg), 0.0)[0]
    dm_b = jnp.where(diff <= 0, jnp.exp(jnp.maximum(-diff, 0.0)[None, None] * lg), 0.0)[1]
    dm = jnp.stack([dm_f, dm_b])
    dm = dm.reshape(2, RET_HEADS // 2, 2, CHUNK, CHUNK).transpose(0, 1, 3, 2, 4)
    dm = dm.reshape(2, RET_HEADS // 2, CHUNK, 2 * CHUNK)
    lane = lambda t: jnp.repeat(t, HEAD_DIM, axis=-1)
    te = jnp.stack([jnp.exp((CHUNK - 1 - pos)[:, None] * log_g[0][None, :]),
                    jnp.exp(pos[:, None] * log_g[1][None, :])])
    fs = jnp.stack([jnp.exp((pos + 1.0)[:, None] * log_g[0][None, :]),
                    jnp.exp((CHUNK - pos)[:, None] * log_g[1][None, :])])
    cd = jnp.broadcast_to(lane(jnp.exp(CHUNK * log_g)).reshape(2, RET_HEADS // 2, 128, 1),
                          (2, RET_HEADS // 2, 128, 128))
    sp = jnp.arange(seed_len, dtype=F32)
    ws = jnp.stack([jnp.exp((seed_len - 1 - sp)[:, None] * log_g[0][None, :]),
                    jnp.exp(sp[:, None] * log_g[1][None, :])])
    if not seeded:
        ws = jnp.zeros_like(ws)
    return dm, lane(te), lane(fs), cd, lane(ws)


def _retention(q, k, v, k_seed, v_seed, tables, tb):
    b, l, w = q.shape
    ls = k_seed.shape[1]
    nb = l // tb
    dm, te, fs, cd, ws = tables
    fwd = pl.BlockSpec((None, tb, w), lambda i, n: (i, n, 0))
    bwd = pl.BlockSpec((None, tb, w), lambda i, n: (i, nb - 1 - n, 0))
    seed = pl.BlockSpec((None, ls, w), lambda i, n: (i, 0, 0))
    const = lambda shape: _resident(shape, lambda i, n: (0,) * len(shape))
    out = jax.ShapeDtypeStruct((b, l, w), F32)
    return pl.pallas_call(
        functools.partial(_ret_kernel, n_chunks=tb // CHUNK),
        out_shape=(out, out),
        grid=(b, nb),
        in_specs=[
            fwd, fwd, fwd, bwd, bwd, bwd, seed, seed,
            const(ws.shape), const(dm.shape), const(te.shape), const(fs.shape), const(cd.shape),
        ],
        out_specs=(fwd, bwd),
        scratch_shapes=[pltpu.VMEM((2, RET_HEADS // 2, 128, 128), F32)],
        compiler_params=_params("parallel", "arbitrary"),
        name="retention",
    )(q, k, v, q, k, v, k_seed, v_seed, ws, dm, te, fs, cd)


def _head_group_norm(o, ones_blk):
    mu = _seg_sum(o, ones_blk) * (1.0 / HEAD_DIM)
    dlt = o - mu
    var = _seg_sum(dlt * dlt, ones_blk) * (1.0 / HEAD_DIM)
    return dlt * lax.rsqrt(var + LN_EPS)


def _mix_out_ffn_kernel(x_ref, mod_ref, na_ref, sg_ref, of_ref, ob_ref, gf_ref, gb_ref, gnw_ref,
                        w_ref, nw_ref, w1_ref, w2_ref, o_ref):
    seg_ones = _seg_ones()
    y = (_head_group_norm(of_ref[...], seg_ones) * gf_ref[...]
         + _head_group_norm(ob_ref[...], seg_ones) * gb_ref[...])
    ret = (y * gnw_ref[...]).astype(BF16)
    o_sg = NA_WIDTH + SG_WIDTH
    out = (_dot(na_ref[...], w_ref[0:NA_WIDTH, :])
           + _dot(sg_ref[...], w_ref[NA_WIDTH:o_sg, :])
           + _dot(ret, w_ref[o_sg:, :]))
    x = x_ref[...] + mod_ref[5:6, :] * out
    o_ref[...] = _ffn_half_step(x, mod_ref, nw_ref, w1_ref, w2_ref, 6)


def _mix_out_ffn(x, mod, na, sg, o_fwd, o_bwd, gf, gb, gnw, w_out, nw, w1, w2, layer, tm):
    b, l, d = x.shape
    mod_map = (lambda i, j: (i, 0, 0)) if mod.shape[0] == b else (lambda i, j: (0, 0, 0))
    tok = lambda w: pl.BlockSpec((None, tm, w), lambda i, j: (i, j, 0))
    const = lambda shape: _resident(shape, lambda i, j: (0,) * len(shape))
    return pl.pallas_call(
        _mix_out_ffn_kernel,
        out_shape=jax.ShapeDtypeStruct(x.shape, x.dtype),
        grid=(b, l // tm),
        in_specs=[
            tok(d),
            pl.BlockSpec((None, N_MOD, d), mod_map),
            tok(NA_WIDTH), tok(SG_WIDTH), tok(RET_WIDTH), tok(RET_WIDTH), tok(RET_WIDTH), tok(RET_WIDTH),
            const((1, RET_WIDTH)), _stacked(w_out, (layer,)),
            const((1, d)), _stacked(w1, (layer, 1)), _stacked(w2, (layer, 1)),
        ],
        out_specs=tok(d),
        compiler_params=_params("parallel", "parallel"),
        name="mix_out_ffn",
    )(x, mod, na, sg, o_fwd, o_bwd, gf, gb, gnw, w_out, nw.reshape(1, d), w1, w2)


def _rope_tables(n_lat):
    t = jnp.arange(n_lat)[:, None]
    lane = np.arange(128)
    inv = ROPE_BASE ** (-jnp.asarray(lane % ROPE_FREQS, F32) / ROPE_FREQS)
    pos = jnp.where(lane % HEAD_DIM < HEAD_DIM // 2, t // GRID_W, t % GRID_W).astype(F32)
    ang = pos * inv[None, :]
    first = (lane // ROPE_FREQS) % 2 == 0
    sin = jnp.sin(ang)
    return jnp.cos(ang), jnp.where(first, -sin, 0.0), jnp.where(first, 0.0, sin)


def kernel(x, c, ctx, c_ctx, ada_w, ada_b, norm_w, ffn_w1, ffn_w2, mix_w_in, mix_w_out, na_q_norm,
           na_k_norm, na_rpb, sg_w, sg_b, sg_ln_w, sg_ln_b, ret_decay_logit, ret_gn_w):
    b, n_lat, d = x.shape
    lc = ctx.shape[1]
    depth = ada_w.shape[0]
    rows = n_lat // GRID_W
    assert d == D_MODEL and n_lat % (GRID_W * NA_BLOCK_ROWS * NA_BLOCKS_PER_STEP) == 0
    assert rows >= 2 * NA_BAND_ROWS - NA_WIN_R
    assert lc % CHUNK == 0
    tm_x = min(512, n_lat)
    tm_c = min(256, lc)
    tb_x = min(1024, n_lat)
    assert n_lat % tb_x == 0
    assert n_lat % tm_x == 0 and lc % tm_c == 0 and tm_x % CHUNK == 0 and tm_c % CHUNK == 0

    cc = jnp.zeros((16, d), F32).at[:b].set(c).at[b].set(c_ctx)
    mod = _adaln(cc, ada_w, ada_b)

    rope_x = _rope_tables(n_lat)
    rope_c = (jnp.ones((lc, 128), F32), jnp.zeros((lc, 128), F32), jnp.zeros((lc, 128), F32))
    w1 = ffn_w1.astype(BF16)
    w2 = ffn_w2.astype(BF16)
    w_in = mix_w_in.astype(BF16)
    w_out = mix_w_out.astype(BF16)
    sgw = sg_w.astype(BF16).transpose(0, 2, 1, 3).reshape(depth, CHUNK, SG_GROUPS * CHUNK)
    log_g = jax.nn.log_sigmoid(ret_decay_logit.astype(F32))
    bias_tabs = _na_bias_table(na_rpb, rows)

    for layer in range(depth):
        last = layer == depth - 1
        mod_x = mod[layer, :b].reshape(b, N_MOD, d)
        mod_c = mod[layer, b:b + 1].reshape(1, N_MOD, d)
        nw = norm_w[layer]
        x = _ffn(x, mod_x, 0, nw[0], w1, w2, (layer, 0), tm_x)
        ctx = _ffn(ctx, mod_c, 0, nw[0], w1, w2, (layer, 0), tm_c)

        qn = jnp.tile(na_q_norm[layer], NA_HEADS).reshape(1, NA_WIDTH)
        kn = jnp.tile(na_k_norm[layer], NA_HEADS).reshape(1, NA_WIDTH)
        sgb_tab = jnp.repeat(sg_b[layer].T, HEAD_DIM, axis=1)
        lnw = sg_ln_w[layer].reshape(1, SG_WIDTH)
        lnb = sg_ln_b[layer].reshape(1, SG_WIDTH)
        gnw = ret_gn_w[layer].reshape(1, RET_WIDTH)
        px = _mix_in(x, mod_x, nw[1], w_in, layer, qn, kn, rope_x, sgw[layer], sgb_tab, lnw, lnb, tm_x)
        pc = _mix_in(ctx, mod_c, nw[1], w_in, layer, qn, kn, rope_c, sgw[layer], sgb_tab, lnw, lnb, tm_c)
        nq_x, nk_x, nv_x, sg_x, rq_x, rk_x, rv_x, gf_x, gb_x = px
        nq_c, nk_c, nv_c, sg_c, rq_c, rk_c, rv_c, gf_c, gb_c = pc

        na_x = _na(nq_x, nk_x, nv_x, nk_c, nv_c, bias_tabs, layer, rows)
        of_x, ob_x = _retention(rq_x, rk_x, rv_x, rk_c, rv_c, _ret_tables(log_g[layer], lc, True), tb_x)
        x = _mix_out_ffn(x, mod_x, na_x, sg_x, of_x, ob_x, gf_x, gb_x, gnw, w_out, nw[2], w1, w2,
                         layer, tm_x)
        if not last:
            na_c = _ctx_attn(nq_c, nk_c, nv_c)
            of_c, ob_c = _retention(rq_c, rk_c, rv_c, rk_c, rv_c, _ret_tables(log_g[layer], lc, False),
                                    tm_c)
            ctx = _mix_out_ffn(ctx, mod_c, na_c, sg_c, of_c, ob_c, gf_c, gb_c, gnw, w_out, nw[2], w1, w2,
                               layer, tm_c)
    return x
```

```python
import functools

import numpy as np
import jax
import jax.numpy as jnp
from jax import lax
from jax.experimental import pallas as pl
from jax.experimental.pallas import tpu as pltpu

D_MODEL = 1024
GRID_W = 64
N_MOD = 9
D_FF = 2816
NA_HEADS = 6
HEAD_DIM = 64
NA_WIDTH = NA_HEADS * HEAD_DIM
NA_WIN_R = 8
NA_WIN_C = 16
SG_GROUPS = 4
SG_WIDTH = SG_GROUPS * HEAD_DIM
CHUNK = 128
RET_HEADS = 6
RET_WIDTH = RET_HEADS * HEAD_DIM
IN_COLS = 3 * NA_WIDTH + 2 * SG_WIDTH + 5 * RET_WIDTH
ROPE_BASE = 10000.0
ROPE_FREQS = HEAD_DIM // 4
RMS_EPS = 1e-6
LN_EPS = 1e-5
NEG_INF = -1e30
LOG2E = 1.4426950408889634

OFF_NQ, OFF_NK, OFF_NV = 0, NA_WIDTH, 2 * NA_WIDTH
OFF_SU = 3 * NA_WIDTH
OFF_SV = OFF_SU + SG_WIDTH
OFF_RQ = OFF_SV + SG_WIDTH
OFF_RK = OFF_RQ + RET_WIDTH
OFF_RV = OFF_RK + RET_WIDTH
OFF_GF = OFF_RV + RET_WIDTH
OFF_GB = OFF_GF + RET_WIDTH

V7X_VMEM_LIMIT_BYTES = 56 * 1024 * 1024
SLAB = 128
HEAD_SHIFT = 6
MXU_DEPTH = 256
TILE_TOKENS = 512
CTX_TILE_TOKENS = 256
RET_BLOCK_TOKENS = 1024
COND_ROWS = 16
NA_BLOCK_ROWS = 4
NA_BAND_ROWS = NA_BLOCK_ROWS + NA_WIN_R
NA_LOGITS_AHEAD = 2
NA_BLOCKS_PER_STEP = 8
RET_SCORES_AHEAD = 5
FFN_CHUNKS = ((0, 1536), (1536, 1280))

BF16 = jnp.bfloat16
F32 = jnp.float32


def _params(*sem):
    return pltpu.CompilerParams(dimension_semantics=sem, vmem_limit_bytes=V7X_VMEM_LIMIT_BYTES)


def _resident(shape, index_map):
    return pl.BlockSpec(shape, index_map, pipeline_mode=pl.Buffered(1))


def _stacked(w, lead):
    n = len(lead)
    return _resident((None,) * n + w.shape[n:], lambda i, j: tuple(lead) + (0,) * (w.ndim - n))


def _dot(a, b):
    return jnp.dot(a, b, preferred_element_type=F32)


def _dot_nt(a, b):
    return lax.dot_general(a, b, (((1,), (1,)), ((), ())), preferred_element_type=F32)


def _dot_tn(a, b):
    return lax.dot_general(a, b, (((0,), (0,)), ((), ())), preferred_element_type=F32)


def _seg_ones():
    assert MXU_DEPTH == 2 * SLAB
    r = lax.broadcasted_iota(jnp.int32, (MXU_DEPTH, SLAB), 0)
    c = lax.broadcasted_iota(jnp.int32, (MXU_DEPTH, SLAB), 1)
    return jnp.where(((r >> HEAD_SHIFT) & 1) == (c >> HEAD_SHIFT), 1.0, 0.0).astype(BF16)


def _seg_sum(t, seg_ones):
    hi = t.astype(BF16)
    lo = (t - hi.astype(F32)).astype(BF16)
    outs = []
    for s in range(t.shape[1] // SLAB):
        sl = slice(s * SLAB, (s + 1) * SLAB)
        outs.append(_dot(jnp.concatenate([hi[:, sl], lo[:, sl]], axis=1), seg_ones))
    return jnp.concatenate(outs, axis=1)


def _ada_rms(x, nw, shift, scale):
    ms = jnp.mean(x * x, axis=-1, keepdims=True)
    return (x * lax.rsqrt(ms + RMS_EPS)) * nw * (1.0 + scale) + shift


def _adaln_kernel(c_ref, w_ref, b_ref, o_ref):
    a = jax.nn.silu(c_ref[...])
    o_ref[...] = jnp.dot(a, w_ref[...], precision=lax.Precision.HIGHEST,
                         preferred_element_type=F32) + b_ref[...]


def _adaln(cc, ada_w, ada_b):
    depth = ada_w.shape[0]
    rows = cc.shape[0]
    tn = D_MODEL
    return pl.pallas_call(
        _adaln_kernel,
        out_shape=jax.ShapeDtypeStruct((depth, rows, N_MOD * D_MODEL), F32),
        grid=(depth, N_MOD * D_MODEL // tn),
        in_specs=[
            pl.BlockSpec((rows, D_MODEL), lambda l, j: (0, 0)),
            pl.BlockSpec((None, D_MODEL, tn), lambda l, j: (l, 0, j)),
            pl.BlockSpec((None, 1, tn), lambda l, j: (l, 0, j)),
        ],
        out_specs=pl.BlockSpec((None, rows, tn), lambda l, j: (l, 0, j)),
        compiler_params=_params("arbitrary", "arbitrary"),
        name="adaln",
    )(cc, ada_w, ada_b.reshape(depth, 1, N_MOD * D_MODEL))


def _ffn_half_step(x, mod_ref, nw_ref, w1_ref, w2_ref, mod_off):
    shift = mod_ref[mod_off:mod_off + 1, :]
    scale = mod_ref[mod_off + 1:mod_off + 2, :]
    gate = mod_ref[mod_off + 2:mod_off + 3, :]
    xb = _ada_rms(x, nw_ref[...], shift, scale).astype(BF16)
    acc = None
    for off, width in FFN_CHUNKS:
        g = _dot(xb, w1_ref[:, off:off + width])
        u = _dot(xb, w1_ref[:, D_FF + off:D_FF + off + width])
        a = (jax.nn.silu(g) * u).astype(BF16)
        part = _dot(a, w2_ref[off:off + width, :])
        acc = part if acc is None else acc + part
    return x + (0.5 * gate) * acc


def _ffn_kernel(x_ref, mod_ref, nw_ref, w1_ref, w2_ref, o_ref, *, mod_off):
    o_ref[...] = _ffn_half_step(x_ref[...], mod_ref, nw_ref, w1_ref, w2_ref, mod_off)


def _ffn(x, mod, mod_off, nw, w1, w2, lead, tm):
    b, l, d = x.shape
    mod_map = (lambda i, j: (i, 0, 0)) if mod.shape[0] == b else (lambda i, j: (0, 0, 0))
    return pl.pallas_call(
        functools.partial(_ffn_kernel, mod_off=mod_off),
        out_shape=jax.ShapeDtypeStruct(x.shape, x.dtype),
        grid=(b, l // tm),
        in_specs=[
            pl.BlockSpec((None, tm, d), lambda i, j: (i, j, 0)),
            pl.BlockSpec((None, N_MOD, d), mod_map),
            _resident((1, d), lambda i, j: (0, 0)),
            _stacked(w1, lead),
            _stacked(w2, lead),
        ],
        out_specs=pl.BlockSpec((None, tm, d), lambda i, j: (i, j, 0)),
        compiler_params=_params("parallel", "parallel"),
        name="ffn",
    )(x, mod, nw.reshape(1, d), w1, w2)


def _head_rms(t, w, ones_blk):
    ms = _seg_sum(t * t, ones_blk) * (1.0 / HEAD_DIM)
    return (t * lax.rsqrt(ms + RMS_EPS)) * w


def _rope(t, cos, sin_next, sin_prev):
    outs = []
    for s in range(RET_WIDTH // SLAB):
        ts = t[:, s * SLAB:(s + 1) * SLAB]
        nxt = pltpu.roll(ts, SLAB - ROPE_FREQS, axis=1)
        prv = pltpu.roll(ts, ROPE_FREQS, axis=1)
        outs.append(ts * cos + nxt * sin_next + prv * sin_prev)
    return jnp.concatenate(outs, axis=1)


def _mix_in_kernel(x_ref, mod_ref, nw_ref, w_ref, qn_ref, kn_ref, cos_ref, sn_ref, sp_ref,
                   sgw_ref, sgb_ref, lnw_ref, lnb_ref,
                   nq_ref, nk_ref, nv_ref, sg_ref, rq_ref, rk_ref, rv_ref, gf_ref, gb_ref):
    tm = x_ref.shape[0]
    x = x_ref[...]
    hb = _ada_rms(x, nw_ref[...], mod_ref[3:4, :], mod_ref[4:5, :]).astype(BF16)
    p = _dot(hb, w_ref[...])
    seg_ones = _seg_ones()
    scale = HEAD_DIM ** -0.5

    nq_ref[...] = (_head_rms(p[:, OFF_NQ:OFF_NQ + NA_WIDTH], qn_ref[...], seg_ones)
                   * (scale * LOG2E)).astype(BF16)
    nk_ref[...] = _head_rms(p[:, OFF_NK:OFF_NK + NA_WIDTH], kn_ref[...], seg_ones).astype(BF16)
    nv_ref[...] = p[:, OFF_NV:OFF_NV + NA_WIDTH].astype(BF16)

    u = jax.nn.gelu(p[:, OFF_SU:OFF_SU + SG_WIDTH])
    v = jax.nn.gelu(p[:, OFF_SV:OFF_SV + SG_WIDTH])
    mu = _seg_sum(v, seg_ones) * (1.0 / HEAD_DIM)
    dv = v - mu
    var = _seg_sum(dv * dv, seg_ones) * (1.0 / HEAD_DIM)
    vn = ((dv * lax.rsqrt(var + LN_EPS)) * lnw_ref[...] + lnb_ref[...]).astype(BF16)
    lane_grp = lax.broadcasted_iota(jnp.int32, (CHUNK, SG_WIDTH), 1) >> HEAD_SHIFT
    for j in range(tm // CHUNK):
        vch = vn[j * CHUNK:(j + 1) * CHUNK, :]
        stacked = jnp.concatenate(
            [jnp.where(lane_grp == g, vch, jnp.zeros_like(vch)) for g in range(SG_GROUPS)], axis=0)
        mixed = _dot(sgw_ref[...], stacked) + sgb_ref[...]
        sg_ref[j * CHUNK:(j + 1) * CHUNK, :] = (u[j * CHUNK:(j + 1) * CHUNK, :] * mixed).astype(BF16)

    cos, sn, sp = cos_ref[...], sn_ref[...], sp_ref[...]
    rq_ref[...] = _rope(p[:, OFF_RQ:OFF_RQ + RET_WIDTH], cos, sn, sp).astype(BF16)
    rk_ref[...] = (_rope(p[:, OFF_RK:OFF_RK + RET_WIDTH], cos, sn, sp) * scale).astype(BF16)
    rv_ref[...] = p[:, OFF_RV:OFF_RV + RET_WIDTH].astype(BF16)
    gf_ref[...] = jax.nn.silu(p[:, OFF_GF:OFF_GF + RET_WIDTH])
    gb_ref[...] = jax.nn.silu(p[:, OFF_GB:OFF_GB + RET_WIDTH])


def _mix_in(x, mod, nw, w_in, layer, qn, kn, rope_tabs, sgw, sgb_tab, lnw, lnb, tm):
    b, l, d = x.shape
    mod_map = (lambda i, j: (i, 0, 0)) if mod.shape[0] == b else (lambda i, j: (0, 0, 0))
    tok = lambda w: pl.BlockSpec((None, tm, w), lambda i, j: (i, j, 0))
    tab = pl.BlockSpec((tm, SLAB), lambda i, j: (j, 0))
    const = lambda shape: _resident(shape, lambda i, j: (0,) * len(shape))
    out_bf = lambda w: jax.ShapeDtypeStruct((b, l, w), BF16)
    out_f32 = lambda w: jax.ShapeDtypeStruct((b, l, w), F32)
    return pl.pallas_call(
        _mix_in_kernel,
        out_shape=(out_bf(NA_WIDTH), out_bf(NA_WIDTH), out_bf(NA_WIDTH), out_bf(SG_WIDTH),
                   out_bf(RET_WIDTH), out_bf(RET_WIDTH), out_bf(RET_WIDTH),
                   out_f32(RET_WIDTH), out_f32(RET_WIDTH)),
        grid=(b, l // tm),
        in_specs=[
            tok(d),
            pl.BlockSpec((None, N_MOD, d), mod_map),
            const((1, d)),
            _stacked(w_in, (layer,)),
            const((1, NA_WIDTH)), const((1, NA_WIDTH)),
            tab, tab, tab,
            const((CHUNK, SG_GROUPS * CHUNK)), const((CHUNK, SG_WIDTH)),
            const((1, SG_WIDTH)), const((1, SG_WIDTH)),
        ],
        out_specs=(tok(NA_WIDTH), tok(NA_WIDTH), tok(NA_WIDTH), tok(SG_WIDTH),
                   tok(RET_WIDTH), tok(RET_WIDTH), tok(RET_WIDTH), tok(RET_WIDTH), tok(RET_WIDTH)),
        compiler_params=_params("parallel", "parallel"),
        name="mix_in",
    )(x, mod, nw.reshape(1, d), w_in, qn, kn, *rope_tabs, sgw, sgb_tab, lnw, lnb)


def _own_lanes(shape, parity):
    even = lax.broadcasted_iota(jnp.int32, shape, 1) < HEAD_DIM
    return even if parity == 0 else jnp.logical_not(even)


def _scores_t(q, parity, keys, biases):
    qp = jnp.where(_own_lanes(q.shape, parity), q, jnp.zeros_like(q))
    scores = [_dot_nt(kk, qp) if bias is None else _dot_nt(kk, qp) + bias for kk, bias in zip(keys, biases)]
    mx = functools.reduce(jnp.maximum, [jnp.max(s, axis=0, keepdims=True) for s in scores])
    return scores, mx


def _weighted_values_t(scores, mx, parity, vals):
    num = None
    for s, vv in zip(scores, vals):
        vp = jnp.where(_own_lanes(vv.shape, parity), vv, jnp.ones_like(vv))
        part = _dot_tn(vp, jnp.exp2((s - mx).astype(BF16)))
        num = part if num is None else num + part
    return num


def _normalised_pair(num_even, num_odd):
    even_row = lax.broadcasted_iota(jnp.int32, num_even.shape, 0) < HEAD_DIM
    out_t = jnp.where(even_row, num_even / num_even[HEAD_DIM:HEAD_DIM + 1, :], num_odd / num_odd[0:1, :])
    return out_t.T


def _attend_all_heads(q_ref, o_ref, n_sub, operands):
    m = q_ref.shape[0] // n_sub
    units = [(t, s, parity) for t in range(n_sub) for s in range(NA_WIDTH // SLAB) for parity in range(2)]

    def logits(u):
        t, s, parity = units[u]
        keys, vals, biases = operands(t, s, parity)
        return _scores_t(q_ref[t * m:(t + 1) * m, s * SLAB:(s + 1) * SLAB], parity, keys, biases), vals

    pending = [logits(u) for u in range(NA_LOGITS_AHEAD)]
    nums = []
    for u, (t, s, parity) in enumerate(units):
        (scores, mx), vals = pending.pop(0)
        if u + NA_LOGITS_AHEAD < len(units):
            pending.append(logits(u + NA_LOGITS_AHEAD))
        nums.append(_weighted_values_t(scores, mx, parity, vals))
        if parity == 1:
            o_ref[t * m:(t + 1) * m, s * SLAB:(s + 1) * SLAB] = (
                _normalised_pair(nums[-2], nums[-1]).astype(BF16))


def _na_kernel(q_ref, k_ref, v_ref, kc_ref, vc_ref, bias_ref, o_ref, *, rows):
    n_blocks = rows // NA_BLOCK_ROWS

    def operands(t, s, parity):
        blk = pl.program_id(1) * NA_BLOCKS_PER_STEP + t
        band_row = jnp.clip(blk * NA_BLOCK_ROWS - NA_WIN_R // 2, 0, rows - NA_BAND_ROWS)
        band = pl.ds(pl.multiple_of(band_row * GRID_W, GRID_W), NA_BAND_ROWS * GRID_W)
        variant = jnp.where(blk == 0, 0, jnp.where(blk == n_blocks - 1, 2, 1))
        sl = slice(s * SLAB, (s + 1) * SLAB)
        return ([k_ref[band, sl], kc_ref[:, sl]], [v_ref[band, sl], vc_ref[:, sl]],
                [bias_ref[variant, 2 * s + parity], None])

    _attend_all_heads(q_ref, o_ref, NA_BLOCKS_PER_STEP, operands)


def _na_bias_table(rpb, rows):
    depth = rpb.shape[0]
    r, band = NA_BLOCK_ROWS, NA_BAND_ROWS
    n_blocks = rows // r
    n_off = 2 * NA_WIN_R - 1
    kcol = np.arange(GRID_W)[:, None]
    qc = np.arange(GRID_W)[None, :]
    c_start = np.clip(qc - NA_WIN_C // 2, 0, GRID_W - NA_WIN_C)
    col_ok = (kcol >= c_start) & (kcol < c_start + NA_WIN_C)
    col_sel = (kcol - qc + NA_WIN_C - 1)[:, :, None] == np.arange(2 * NA_WIN_C - 1)
    tiles = jnp.einsum('lhij,kqj->lihkq', rpb, jnp.asarray(col_sel, F32), precision=lax.Precision.HIGHEST)
    tiles = jnp.where(col_ok, tiles * LOG2E, NEG_INF)
    tile_of = np.full((3, band, r), n_off, np.int64)
    for vi, blk in enumerate((0, 1, n_blocks - 1)):
        r0 = blk * r
        b0 = int(np.clip(r0 - NA_WIN_R // 2, 0, rows - band))
        for ki in range(band):
            for qi in range(r):
                r_start = int(np.clip(r0 + qi - NA_WIN_R // 2, 0, rows - NA_WIN_R))
                if r_start <= b0 + ki < r_start + NA_WIN_R:
                    tile_of[vi, ki, qi] = b0 + ki - (r0 + qi) + NA_WIN_R - 1

    def assemble(tiles_ref, o_ref):
        for vi in range(3):
            for ki in range(band):
                for qi in range(r):
                    dst = (vi, slice(None), slice(ki * GRID_W, (ki + 1) * GRID_W),
                           slice(qi * GRID_W, (qi + 1) * GRID_W))
                    idx = int(tile_of[vi, ki, qi])
                    if idx == n_off:
                        o_ref[dst] = jnp.full((NA_HEADS, GRID_W, GRID_W), NEG_INF, F32)
                    else:
                        o_ref[dst] = tiles_ref[idx]

    return pl.pallas_call(
        assemble,
        out_shape=jax.ShapeDtypeStruct((depth, 3, NA_HEADS, band * GRID_W, r * GRID_W), F32),
        grid=(depth,),
        in_specs=[pl.BlockSpec((None, n_off, NA_HEADS, GRID_W, GRID_W), lambda l: (l, 0, 0, 0, 0))],
        out_specs=pl.BlockSpec((None, 3, NA_HEADS, band * GRID_W, r * GRID_W), lambda l: (l, 0, 0, 0, 0)),
        compiler_params=_params("parallel"),
        name="na_bias",
    )(tiles)


def _na(q, k, v, kc, vc, bias_tabs, layer, rows):
    b, s, w = q.shape
    lc = kc.shape[1]
    n_steps = rows // (NA_BLOCK_ROWS * NA_BLOCKS_PER_STEP)
    tq = NA_BLOCK_ROWS * NA_BLOCKS_PER_STEP * GRID_W
    return pl.pallas_call(
        functools.partial(_na_kernel, rows=rows),
        out_shape=jax.ShapeDtypeStruct((b, s, w), BF16),
        grid=(b, n_steps),
        in_specs=[
            pl.BlockSpec((None, tq, w), lambda i, j: (i, j, 0)),
            pl.BlockSpec((None, s, w), lambda i, j: (i, 0, 0)),
            pl.BlockSpec((None, s, w), lambda i, j: (i, 0, 0)),
            pl.BlockSpec((None, lc, w), lambda i, j: (i, 0, 0)),
            pl.BlockSpec((None, lc, w), lambda i, j: (i, 0, 0)),
            _stacked(bias_tabs, (layer,)),
        ],
        out_specs=pl.BlockSpec((None, tq, w), lambda i, j: (i, j, 0)),
        compiler_params=_params("parallel", "arbitrary"),
        name="na",
    )(q, k, v, kc, vc, bias_tabs)


def _ctx_attn_kernel(q_ref, k_ref, v_ref, o_ref):
    def operands(t, s, parity):
        sl = slice(s * SLAB, (s + 1) * SLAB)
        return [k_ref[:, sl]], [v_ref[:, sl]], [None]

    _attend_all_heads(q_ref, o_ref, 1, operands)


def _ctx_attn(q, k, v):
    b, lc, w = q.shape
    spec = pl.BlockSpec((None, lc, w), lambda i: (i, 0, 0))
    return pl.pallas_call(
        _ctx_attn_kernel,
        out_shape=jax.ShapeDtypeStruct((b, lc, w), BF16),
        grid=(b,),
        in_specs=[spec, spec, spec],
        out_specs=spec,
        compiler_params=_params("parallel"),
        name="ctx_attn",
    )(q, k, v)


def _ret_kernel(qf_ref, kf_ref, vf_ref, qb_ref, kb_ref, vb_ref, ks_ref, vs_ref, ws_ref, dm_ref, te_ref,
                fs_ref, cd_ref, of_ref, ob_ref, state_ref, *, n_chunks):
    n = pl.program_id(1)
    same_head = ((lax.broadcasted_iota(jnp.int32, (SLAB, SLAB), 0) >> HEAD_SHIFT)
                 == (lax.broadcasted_iota(jnp.int32, (SLAB, SLAB), 1) >> HEAD_SHIFT))
    even = lax.broadcasted_iota(jnp.int32, (CHUNK, SLAB), 1) < HEAD_DIM

    @pl.when(n == 0)
    def _seed():
        for d in range(2):
            for s in range(RET_WIDTH // SLAB):
                sl = slice(s * SLAB, (s + 1) * SLAB)
                kw = (ks_ref[:, sl].astype(F32) * ws_ref[d, :, sl]).astype(BF16)
                state_ref[d, s] = jnp.where(same_head, _dot_tn(kw, vs_ref[:, sl]), 0.0)

    streams = ((qf_ref, kf_ref, vf_ref, of_ref), (qb_ref, kb_ref, vb_ref, ob_ref))
    chunk = lambda c: slice(c * CHUNK, (c + 1) * CHUNK)
    units = []
    for s in range(RET_WIDTH // SLAB):
        sl = slice(s * SLAB, (s + 1) * SLAB)
        for d, (_, k_ref, v_ref, _) in enumerate(streams):
            order = [j if d == 0 else n_chunks - 1 - j for j in range(n_chunks)]
            kvs = []
            for c in order:
                kt = (k_ref[chunk(c), sl].astype(F32) * te_ref[d, :, sl]).astype(BF16)
                kvs.append(jnp.where(same_head, _dot_tn(kt, v_ref[chunk(c), sl]), 0.0))
            st = state_ref[d, s]
            for c, kv in zip(order, kvs):
                units.append((s, d, c, st))
                st = cd_ref[d, s] * st + kv
            state_ref[d, s] = st

    def decayed_scores(u):
        s, d, c, _ = units[u]
        sl = slice(s * SLAB, (s + 1) * SLAB)
        q = streams[d][0][chunk(c), sl]
        k = streams[d][1][chunk(c), sl]
        zero = jnp.zeros_like(k)
        k2 = jnp.concatenate([jnp.where(even, k, zero), jnp.where(even, zero, k)], axis=0)
        return (_dot_nt(q, k2) * dm_ref[d, s]).astype(BF16)

    pending = [decayed_scores(u) for u in range(min(RET_SCORES_AHEAD, len(units)))]
    for u, (s, d, c, st) in enumerate(units):
        sl = slice(s * SLAB, (s + 1) * SLAB)
        scores = pending.pop(0)
        if u + RET_SCORES_AHEAD < len(units):
            pending.append(decayed_scores(u + RET_SCORES_AHEAD))
        q_ref, _, v_ref, o_ref = streams[d]
        v = v_ref[chunk(c), sl]
        zero = jnp.zeros_like(v)
        v2 = jnp.concatenate([jnp.where(even, v, zero), jnp.where(even, zero, v)], axis=0)
        qf = (q_ref[chunk(c), sl].astype(F32) * fs_ref[d, :, sl]).astype(BF16)
        o_ref[chunk(c), sl] = _dot(scores, v2) + _dot(qf, st.astype(BF16))


def _ret_tables(log_g, seed_len, seeded):
    pos = jnp.arange(CHUNK, dtype=F32)
    diff = pos[:, None] - pos[None, :]
    lg = log_g[:, :, None, None]
    dm_f = jnp.where(diff >= 0, jnp.exp(jnp.maximum(diff, 0.0)[None, None] * lg), 0.0)[0]
    dm_b = jnp.where(diff <= 0, jnp.exp(jnp.maximum(-diff, 0.0)[None, None] * lg), 0.0)[1]
    dm = jnp.stack([dm_f, dm_b])
    dm = dm.reshape(2, RET_HEADS // 2, 2, CHUNK, CHUNK).transpose(0, 1, 3, 2, 4)
    dm = dm.reshape(2, RET_HEADS // 2, CHUNK, 2 * CHUNK)
    lane = lambda t: jnp.repeat(t, HEAD_DIM, axis=-1)
    te = jnp.stack([jnp.exp((CHUNK - 1 - pos)[:, None] * log_g[0][None, :]),
                    jnp.exp(pos[:, None] * log_g[1][None, :])])
    fs = jnp.stack([jnp.exp((pos + 1.0)[:, None] * log_g[0][None, :]),
                    jnp.exp((CHUNK - pos)[:, None] * log_g[1][None, :])])
    cd = jnp.broadcast_to(lane(jnp.exp(CHUNK * log_g)).reshape(2, RET_HEADS // 2, SLAB, 1),
                          (2, RET_HEADS // 2, SLAB, SLAB))
    sp = jnp.arange(seed_len, dtype=F32)
    ws = jnp.stack([jnp.exp((seed_len - 1 - sp)[:, None] * log_g[0][None, :]),
                    jnp.exp(sp[:, None] * log_g[1][None, :])])
    if not seeded:
        ws = jnp.zeros_like(ws)
    return dm, lane(te), lane(fs), cd, lane(ws)


def _retention(q, k, v, k_seed, v_seed, tables, tb):
    b, l, w = q.shape
    ls = k_seed.shape[1]
    nb = l // tb
    dm, te, fs, cd, ws = tables
    fwd = pl.BlockSpec((None, tb, w), lambda i, n: (i, n, 0))
    bwd = pl.BlockSpec((None, tb, w), lambda i, n: (i, nb - 1 - n, 0))
    seed = pl.BlockSpec((None, ls, w), lambda i, n: (i, 0, 0))
    const = lambda shape: _resident(shape, lambda i, n: (0,) * len(shape))
    out = jax.ShapeDtypeStruct((b, l, w), F32)
    return pl.pallas_call(
        functools.partial(_ret_kernel, n_chunks=tb // CHUNK),
        out_shape=(out, out),
        grid=(b, nb),
        in_specs=[
            fwd, fwd, fwd, bwd, bwd, bwd, seed, seed,
            const(ws.shape), const(dm.shape), const(te.shape), const(fs.shape), const(cd.shape),
        ],
        out_specs=(fwd, bwd),
        scratch_shapes=[pltpu.VMEM((2, RET_HEADS // 2, SLAB, SLAB), F32)],
        compiler_params=_params("parallel", "arbitrary"),
        name="retention",
    )(q, k, v, q, k, v, k_seed, v_seed, ws, dm, te, fs, cd)


def _head_group_norm(o, ones_blk):
    mu = _seg_sum(o, ones_blk) * (1.0 / HEAD_DIM)
    dlt = o - mu
    var = _seg_sum(dlt * dlt, ones_blk) * (1.0 / HEAD_DIM)
    return dlt * lax.rsqrt(var + LN_EPS)


def _mix_out_ffn_kernel(x_ref, mod_ref, na_ref, sg_ref, of_ref, ob_ref, gf_ref, gb_ref, gnw_ref,
                        w_ref, nw_ref, w1_ref, w2_ref, o_ref):
    seg_ones = _seg_ones()
    y = (_head_group_norm(of_ref[...], seg_ones) * gf_ref[...]
         + _head_group_norm(ob_ref[...], seg_ones) * gb_ref[...])
    ret = (y * gnw_ref[...]).astype(BF16)
    o_sg = NA_WIDTH + SG_WIDTH
    out = (_dot(na_ref[...], w_ref[0:NA_WIDTH, :])
           + _dot(sg_ref[...], w_ref[NA_WIDTH:o_sg, :])
           + _dot(ret, w_ref[o_sg:, :]))
    x = x_ref[...] + mod_ref[5:6, :] * out
    o_ref[...] = _ffn_half_step(x, mod_ref, nw_ref, w1_ref, w2_ref, 6)


def _mix_out_ffn(x, mod, na, sg, o_fwd, o_bwd, gf, gb, gnw, w_out, nw, w1, w2, layer, tm):
    b, l, d = x.shape
    mod_map = (lambda i, j: (i, 0, 0)) if mod.shape[0] == b else (lambda i, j: (0, 0, 0))
    tok = lambda w: pl.BlockSpec((None, tm, w), lambda i, j: (i, j, 0))
    const = lambda shape: _resident(shape, lambda i, j: (0,) * len(shape))
    return pl.pallas_call(
        _mix_out_ffn_kernel,
        out_shape=jax.ShapeDtypeStruct(x.shape, x.dtype),
        grid=(b, l // tm),
        in_specs=[
            tok(d),
            pl.BlockSpec((None, N_MOD, d), mod_map),
            tok(NA_WIDTH), tok(SG_WIDTH), tok(RET_WIDTH), tok(RET_WIDTH), tok(RET_WIDTH), tok(RET_WIDTH),
            const((1, RET_WIDTH)), _stacked(w_out, (layer,)),
            const((1, d)), _stacked(w1, (layer, 1)), _stacked(w2, (layer, 1)),
        ],
        out_specs=tok(d),
        compiler_params=_params("parallel", "parallel"),
        name="mix_out_ffn",
    )(x, mod, na, sg, o_fwd, o_bwd, gf, gb, gnw, w_out, nw.reshape(1, d), w1, w2)


def _rope_tables(n_lat):
    t = jnp.arange(n_lat)[:, None]
    lane = np.arange(SLAB)
    inv = ROPE_BASE ** (-jnp.asarray(lane % ROPE_FREQS, F32) / ROPE_FREQS)
    pos = jnp.where(lane % HEAD_DIM < HEAD_DIM // 2, t // GRID_W, t % GRID_W).astype(F32)
    ang = pos * inv[None, :]
    first = (lane // ROPE_FREQS) % 2 == 0
    sin = jnp.sin(ang)
    return jnp.cos(ang), jnp.where(first, -sin, 0.0), jnp.where(first, 0.0, sin)


def kernel(x, c, ctx, c_ctx, ada_w, ada_b, norm_w, ffn_w1, ffn_w2, mix_w_in, mix_w_out, na_q_norm,
           na_k_norm, na_rpb, sg_w, sg_b, sg_ln_w, sg_ln_b, ret_decay_logit, ret_gn_w):
    b, n_lat, d = x.shape
    lc = ctx.shape[1]
    depth = ada_w.shape[0]
    rows = n_lat // GRID_W
    assert d == D_MODEL and n_lat % (GRID_W * NA_BLOCK_ROWS * NA_BLOCKS_PER_STEP) == 0
    assert rows >= 2 * NA_BAND_ROWS - NA_WIN_R
    assert lc % CHUNK == 0
    tm_x = min(TILE_TOKENS, n_lat)
    tm_c = min(CTX_TILE_TOKENS, lc)
    tb_x = min(RET_BLOCK_TOKENS, n_lat)
    assert n_lat % tb_x == 0
    assert n_lat % tm_x == 0 and lc % tm_c == 0 and tm_x % CHUNK == 0 and tm_c % CHUNK == 0

    assert b + 1 <= COND_ROWS
    cc = jnp.zeros((COND_ROWS, d), F32).at[:b].set(c).at[b].set(c_ctx)
    mod = _adaln(cc, ada_w, ada_b)

    rope_x = _rope_tables(n_lat)
    rope_c = (jnp.ones((lc, SLAB), F32), jnp.zeros((lc, SLAB), F32), jnp.zeros((lc, SLAB), F32))
    w1 = ffn_w1.astype(BF16)
    w2 = ffn_w2.astype(BF16)
    w_in = mix_w_in.astype(BF16)
    w_out = mix_w_out.astype(BF16)
    sgw = sg_w.astype(BF16).transpose(0, 2, 1, 3).reshape(depth, CHUNK, SG_GROUPS * CHUNK)
    log_g = jax.nn.log_sigmoid(ret_decay_logit.astype(F32))
    bias_tabs = _na_bias_table(na_rpb, rows)

    for layer in range(depth):
        last = layer == depth - 1
        mod_x = mod[layer, :b].reshape(b, N_MOD, d)
        mod_c = mod[layer, b:b + 1].reshape(1, N_MOD, d)
        nw = norm_w[layer]
        x = _ffn(x, mod_x, 0, nw[0], w1, w2, (layer, 0), tm_x)
        ctx = _ffn(ctx, mod_c, 0, nw[0], w1, w2, (layer, 0), tm_c)

        qn = jnp.tile(na_q_norm[layer], NA_HEADS).reshape(1, NA_WIDTH)
        kn = jnp.tile(na_k_norm[layer], NA_HEADS).reshape(1, NA_WIDTH)
        sgb_tab = jnp.repeat(sg_b[layer].T, HEAD_DIM, axis=1)
        lnw = sg_ln_w[layer].reshape(1, SG_WIDTH)
        lnb = sg_ln_b[layer].reshape(1, SG_WIDTH)
        gnw = ret_gn_w[layer].reshape(1, RET_WIDTH)
        px = _mix_in(x, mod_x, nw[1], w_in, layer, qn, kn, rope_x, sgw[layer], sgb_tab, lnw, lnb, tm_x)
        pc = _mix_in(ctx, mod_c, nw[1], w_in, layer, qn, kn, rope_c, sgw[layer], sgb_tab, lnw, lnb, tm_c)
        nq_x, nk_x, nv_x, sg_x, rq_x, rk_x, rv_x, gf_x, gb_x = px
        nq_c, nk_c, nv_c, sg_c, rq_c, rk_c, rv_c, gf_c, gb_c = pc

        na_x = _na(nq_x, nk_x, nv_x, nk_c, nv_c, bias_tabs, layer, rows)
        of_x, ob_x = _retention(rq_x, rk_x, rv_x, rk_c, rv_c, _ret_tables(log_g[layer], lc, True), tb_x)
        x = _mix_out_ffn(x, mod_x, na_x, sg_x, of_x, ob_x, gf_x, gb_x, gnw, w_out, nw[2], w1, w2,
                         layer, tm_x)
        if not last:
            na_c = _ctx_attn(nq_c, nk_c, nv_c)
            of_c, ob_c = _retention(rq_c, rk_c, rv_c, rk_c, rv_c, _ret_tables(log_g[layer], lc, False),
                                    tm_c)
            ctx = _mix_out_ffn(ctx, mod_c, na_c, sg_c, of_c, ob_c, gf_c, gb_c, gnw, w_out, nw[2], w1, w2,
                               layer, tm_c)
    return x
```

```python
import functools

import numpy as np
import jax
import jax.numpy as jnp
from jax import lax
from jax.experimental import pallas as pl
from jax.experimental.pallas import tpu as pltpu

D_MODEL = 1024
GRID_W = 64
N_MOD = 9
D_FF = 2816
NA_HEADS = 6
HEAD_DIM = 64
NA_WIDTH = NA_HEADS * HEAD_DIM
NA_WIN_R = 8
NA_WIN_C = 16
SG_GROUPS = 4
SG_WIDTH = SG_GROUPS * HEAD_DIM
CHUNK = 128
RET_HEADS = 6
RET_WIDTH = RET_HEADS * HEAD_DIM
IN_COLS = 3 * NA_WIDTH + 2 * SG_WIDTH + 5 * RET_WIDTH
ROPE_BASE = 10000.0
ROPE_FREQS = HEAD_DIM // 4
RMS_EPS = 1e-6
LN_EPS = 1e-5
NEG_INF = -1e30
LOG2E = 1.4426950408889634

OFF_NQ, OFF_NK, OFF_NV = 0, NA_WIDTH, 2 * NA_WIDTH
OFF_SU = 3 * NA_WIDTH
OFF_SV = OFF_SU + SG_WIDTH
OFF_RQ = OFF_SV + SG_WIDTH
OFF_RK = OFF_RQ + RET_WIDTH
OFF_RV = OFF_RK + RET_WIDTH
OFF_GF = OFF_RV + RET_WIDTH
OFF_GB = OFF_GF + RET_WIDTH

V7X_VMEM_LIMIT_BYTES = 56 * 1024 * 1024
SLAB = 128
HEAD_SHIFT = 6
MXU_DEPTH = 256
TILE_TOKENS = 512
CTX_TILE_TOKENS = 256
RET_BLOCK_TOKENS = 2048
COND_ROWS = 16
NA_BLOCK_ROWS = 4
NA_BAND_ROWS = NA_BLOCK_ROWS + NA_WIN_R
NA_LOGITS_AHEAD = 2
NA_BLOCKS_PER_STEP = 8
RET_SCORES_AHEAD = 5
FFN_CHUNKS = ((0, 1536), (1536, 1280))

BF16 = jnp.bfloat16
F32 = jnp.float32


def _params(*sem):
    return pltpu.CompilerParams(dimension_semantics=sem, vmem_limit_bytes=V7X_VMEM_LIMIT_BYTES)


def _resident(shape, index_map):
    return pl.BlockSpec(shape, index_map, pipeline_mode=pl.Buffered(1))


def _stacked(w, lead):
    n = len(lead)
    return _resident((None,) * n + w.shape[n:], lambda i, j: tuple(lead) + (0,) * (w.ndim - n))


def _dot(a, b):
    return jnp.dot(a, b, preferred_element_type=F32)


def _dot_nt(a, b):
    return lax.dot_general(a, b, (((1,), (1,)), ((), ())), preferred_element_type=F32)


def _dot_tn(a, b):
    return lax.dot_general(a, b, (((0,), (0,)), ((), ())), preferred_element_type=F32)


def _seg_ones():
    assert MXU_DEPTH == 2 * SLAB
    r = lax.broadcasted_iota(jnp.int32, (MXU_DEPTH, SLAB), 0)
    c = lax.broadcasted_iota(jnp.int32, (MXU_DEPTH, SLAB), 1)
    return jnp.where(((r >> HEAD_SHIFT) & 1) == (c >> HEAD_SHIFT), 1.0, 0.0).astype(BF16)


def _seg_sum(t, seg_ones):
    hi = t.astype(BF16)
    lo = (t - hi.astype(F32)).astype(BF16)
    outs = []
    for s in range(t.shape[1] // SLAB):
        sl = slice(s * SLAB, (s + 1) * SLAB)
        outs.append(_dot(jnp.concatenate([hi[:, sl], lo[:, sl]], axis=1), seg_ones))
    return jnp.concatenate(outs, axis=1)


def _ada_rms(x, nw, shift, scale):
    ms = jnp.mean(x * x, axis=-1, keepdims=True)
    return (x * lax.rsqrt(ms + RMS_EPS)) * nw * (1.0 + scale) + shift


def _adaln_kernel(c_ref, w_ref, b_ref, o_ref):
    a = jax.nn.silu(c_ref[...])
    o_ref[...] = jnp.dot(a, w_ref[...], precision=lax.Precision.HIGHEST,
                         preferred_element_type=F32) + b_ref[...]


def _adaln(cc, ada_w, ada_b):
    depth = ada_w.shape[0]
    rows = cc.shape[0]
    tn = D_MODEL
    return pl.pallas_call(
        _adaln_kernel,
        out_shape=jax.ShapeDtypeStruct((depth, rows, N_MOD * D_MODEL), F32),
        grid=(depth, N_MOD * D_MODEL // tn),
        in_specs=[
            pl.BlockSpec((rows, D_MODEL), lambda l, j: (0, 0)),
            pl.BlockSpec((None, D_MODEL, tn), lambda l, j: (l, 0, j)),
            pl.BlockSpec((None, 1, tn), lambda l, j: (l, 0, j)),
        ],
        out_specs=pl.BlockSpec((None, rows, tn), lambda l, j: (l, 0, j)),
        compiler_params=_params("arbitrary", "arbitrary"),
        name="adaln",
    )(cc, ada_w, ada_b.reshape(depth, 1, N_MOD * D_MODEL))


def _ffn_half_step(x, mod_ref, nw_ref, w1_ref, w2_ref, mod_off):
    shift = mod_ref[mod_off:mod_off + 1, :]
    scale = mod_ref[mod_off + 1:mod_off + 2, :]
    gate = mod_ref[mod_off + 2:mod_off + 3, :]
    xb = _ada_rms(x, nw_ref[...], shift, scale).astype(BF16)
    acc = None
    for off, width in FFN_CHUNKS:
        g = _dot(xb, w1_ref[:, off:off + width])
        u = _dot(xb, w1_ref[:, D_FF + off:D_FF + off + width])
        a = (jax.nn.silu(g) * u).astype(BF16)
        part = _dot(a, w2_ref[off:off + width, :])
        acc = part if acc is None else acc + part
    return x + (0.5 * gate) * acc


def _ffn_kernel(x_ref, mod_ref, nw_ref, w1_ref, w2_ref, o_ref, *, mod_off):
    o_ref[...] = _ffn_half_step(x_ref[...], mod_ref, nw_ref, w1_ref, w2_ref, mod_off)


def _ffn(x, mod, mod_off, nw, w1, w2, lead, tm):
    b, l, d = x.shape
    mod_map = (lambda i, j: (i, 0, 0)) if mod.shape[0] == b else (lambda i, j: (0, 0, 0))
    return pl.pallas_call(
        functools.partial(_ffn_kernel, mod_off=mod_off),
        out_shape=jax.ShapeDtypeStruct(x.shape, x.dtype),
        grid=(b, l // tm),
        in_specs=[
            pl.BlockSpec((None, tm, d), lambda i, j: (i, j, 0)),
            pl.BlockSpec((None, N_MOD, d), mod_map),
            _resident((1, d), lambda i, j: (0, 0)),
            _stacked(w1, lead),
            _stacked(w2, lead),
        ],
        out_specs=pl.BlockSpec((None, tm, d), lambda i, j: (i, j, 0)),
        compiler_params=_params("parallel", "parallel"),
        name="ffn",
    )(x, mod, nw.reshape(1, d), w1, w2)


def _head_rms(t, w, ones_blk):
    ms = _seg_sum(t * t, ones_blk) * (1.0 / HEAD_DIM)
    return (t * lax.rsqrt(ms + RMS_EPS)) * w


def _rope(t, cos, sin_next, sin_prev):
    outs = []
    for s in range(RET_WIDTH // SLAB):
        ts = t[:, s * SLAB:(s + 1) * SLAB]
        nxt = pltpu.roll(ts, SLAB - ROPE_FREQS, axis=1)
        prv = pltpu.roll(ts, ROPE_FREQS, axis=1)
        outs.append(ts * cos + nxt * sin_next + prv * sin_prev)
    return jnp.concatenate(outs, axis=1)


def _mix_in_kernel(x_ref, mod_ref, nw_ref, w_ref, qn_ref, kn_ref, cos_ref, sn_ref, sp_ref,
                   sgw_ref, sgb_ref, lnw_ref, lnb_ref,
                   nq_ref, nk_ref, nv_ref, sg_ref, rq_ref, rk_ref, rv_ref, gf_ref, gb_ref):
    tm = x_ref.shape[0]
    x = x_ref[...]
    hb = _ada_rms(x, nw_ref[...], mod_ref[3:4, :], mod_ref[4:5, :]).astype(BF16)
    p = _dot(hb, w_ref[...])
    seg_ones = _seg_ones()
    scale = HEAD_DIM ** -0.5

    nq_ref[...] = (_head_rms(p[:, OFF_NQ:OFF_NQ + NA_WIDTH], qn_ref[...], seg_ones)
                   * (scale * LOG2E)).astype(BF16)
    nk_ref[...] = _head_rms(p[:, OFF_NK:OFF_NK + NA_WIDTH], kn_ref[...], seg_ones).astype(BF16)
    nv_ref[...] = p[:, OFF_NV:OFF_NV + NA_WIDTH].astype(BF16)

    u = jax.nn.gelu(p[:, OFF_SU:OFF_SU + SG_WIDTH])
    v = jax.nn.gelu(p[:, OFF_SV:OFF_SV + SG_WIDTH])
    mu = _seg_sum(v, seg_ones) * (1.0 / HEAD_DIM)
    dv = v - mu
    var = _seg_sum(dv * dv, seg_ones) * (1.0 / HEAD_DIM)
    vn = ((dv * lax.rsqrt(var + LN_EPS)) * lnw_ref[...] + lnb_ref[...]).astype(BF16)
    lane_grp = lax.broadcasted_iota(jnp.int32, (CHUNK, SG_WIDTH), 1) >> HEAD_SHIFT
    for j in range(tm // CHUNK):
        vch = vn[j * CHUNK:(j + 1) * CHUNK, :]
        stacked = jnp.concatenate(
            [jnp.where(lane_grp == g, vch, jnp.zeros_like(vch)) for g in range(SG_GROUPS)], axis=0)
        mixed = _dot(sgw_ref[...], stacked) + sgb_ref[...]
        sg_ref[j * CHUNK:(j + 1) * CHUNK, :] = (u[j * CHUNK:(j + 1) * CHUNK, :] * mixed).astype(BF16)

    cos, sn, sp = cos_ref[...], sn_ref[...], sp_ref[...]
    rq_ref[...] = _rope(p[:, OFF_RQ:OFF_RQ + RET_WIDTH], cos, sn, sp).astype(BF16)
    rk_ref[...] = (_rope(p[:, OFF_RK:OFF_RK + RET_WIDTH], cos, sn, sp) * scale).astype(BF16)
    rv_ref[...] = p[:, OFF_RV:OFF_RV + RET_WIDTH].astype(BF16)
    gf_ref[...] = jax.nn.silu(p[:, OFF_GF:OFF_GF + RET_WIDTH])
    gb_ref[...] = jax.nn.silu(p[:, OFF_GB:OFF_GB + RET_WIDTH])


def _mix_in(x, mod, nw, w_in, layer, qn, kn, rope_tabs, sgw, sgb_tab, lnw, lnb, tm):
    b, l, d = x.shape
    mod_map = (lambda i, j: (i, 0, 0)) if mod.shape[0] == b else (lambda i, j: (0, 0, 0))
    tok = lambda w: pl.BlockSpec((None, tm, w), lambda i, j: (i, j, 0))
    tab = pl.BlockSpec((tm, SLAB), lambda i, j: (j, 0))
    const = lambda shape: _resident(shape, lambda i, j: (0,) * len(shape))
    out_bf = lambda w: jax.ShapeDtypeStruct((b, l, w), BF16)
    out_f32 = lambda w: jax.ShapeDtypeStruct((b, l, w), F32)
    return pl.pallas_call(
        _mix_in_kernel,
        out_shape=(out_bf(NA_WIDTH), out_bf(NA_WIDTH), out_bf(NA_WIDTH), out_bf(SG_WIDTH),
                   out_bf(RET_WIDTH), out_bf(RET_WIDTH), out_bf(RET_WIDTH),
                   out_f32(RET_WIDTH), out_f32(RET_WIDTH)),
        grid=(b, l // tm),
        in_specs=[
            tok(d),
            pl.BlockSpec((None, N_MOD, d), mod_map),
            const((1, d)),
            _stacked(w_in, (layer,)),
            const((1, NA_WIDTH)), const((1, NA_WIDTH)),
            tab, tab, tab,
            const((CHUNK, SG_GROUPS * CHUNK)), const((CHUNK, SG_WIDTH)),
            const((1, SG_WIDTH)), const((1, SG_WIDTH)),
        ],
        out_specs=(tok(NA_WIDTH), tok(NA_WIDTH), tok(NA_WIDTH), tok(SG_WIDTH),
                   tok(RET_WIDTH), tok(RET_WIDTH), tok(RET_WIDTH), tok(RET_WIDTH), tok(RET_WIDTH)),
        compiler_params=_params("parallel", "parallel"),
        name="mix_in",
    )(x, mod, nw.reshape(1, d), w_in, qn, kn, *rope_tabs, sgw, sgb_tab, lnw, lnb)


def _own_lanes(shape, parity):
    even = lax.broadcasted_iota(jnp.int32, shape, 1) < HEAD_DIM
    return even if parity == 0 else jnp.logical_not(even)


def _scores_t(q, parity, keys, biases):
    qp = jnp.where(_own_lanes(q.shape, parity), q, jnp.zeros_like(q))
    scores = [_dot_nt(kk, qp) if bias is None else _dot_nt(kk, qp) + bias for kk, bias in zip(keys, biases)]
    mx = functools.reduce(jnp.maximum, [jnp.max(s, axis=0, keepdims=True) for s in scores])
    return scores, mx


def _weighted_values_t(scores, mx, parity, vals):
    num = None
    for s, vv in zip(scores, vals):
        vp = jnp.where(_own_lanes(vv.shape, parity), vv, jnp.ones_like(vv))
        part = _dot_tn(vp, jnp.exp2((s - mx).astype(BF16)))
        num = part if num is None else num + part
    return num


def _normalised_pair(num_even, num_odd):
    even_row = lax.broadcasted_iota(jnp.int32, num_even.shape, 0) < HEAD_DIM
    out_t = jnp.where(even_row, num_even / num_even[HEAD_DIM:HEAD_DIM + 1, :], num_odd / num_odd[0:1, :])
    return out_t.T


def _attend_all_heads(q_ref, o_ref, n_sub, operands):
    m = q_ref.shape[0] // n_sub
    units = [(t, s, parity) for t in range(n_sub) for s in range(NA_WIDTH // SLAB) for parity in range(2)]

    def logits(u):
        t, s, parity = units[u]
        keys, vals, biases = operands(t, s, parity)
        return _scores_t(q_ref[t * m:(t + 1) * m, s * SLAB:(s + 1) * SLAB], parity, keys, biases), vals

    pending = [logits(u) for u in range(NA_LOGITS_AHEAD)]
    nums = []
    for u, (t, s, parity) in enumerate(units):
        (scores, mx), vals = pending.pop(0)
        if u + NA_LOGITS_AHEAD < len(units):
            pending.append(logits(u + NA_LOGITS_AHEAD))
        nums.append(_weighted_values_t(scores, mx, parity, vals))
        if parity == 1:
            o_ref[t * m:(t + 1) * m, s * SLAB:(s + 1) * SLAB] = (
                _normalised_pair(nums[-2], nums[-1]).astype(BF16))


def _na_kernel(q_ref, k_ref, v_ref, kc_ref, vc_ref, bias_ref, o_ref, *, rows):
    n_blocks = rows // NA_BLOCK_ROWS

    def operands(t, s, parity):
        blk = pl.program_id(1) * NA_BLOCKS_PER_STEP + t
        band_row = jnp.clip(blk * NA_BLOCK_ROWS - NA_WIN_R // 2, 0, rows - NA_BAND_ROWS)
        band = pl.ds(pl.multiple_of(band_row * GRID_W, GRID_W), NA_BAND_ROWS * GRID_W)
        variant = jnp.where(blk == 0, 0, jnp.where(blk == n_blocks - 1, 2, 1))
        sl = slice(s * SLAB, (s + 1) * SLAB)
        return ([k_ref[band, sl], kc_ref[:, sl]], [v_ref[band, sl], vc_ref[:, sl]],
                [bias_ref[variant, 2 * s + parity], None])

    _attend_all_heads(q_ref, o_ref, NA_BLOCKS_PER_STEP, operands)


def _na_bias_table(rpb, rows):
    depth = rpb.shape[0]
    r, band = NA_BLOCK_ROWS, NA_BAND_ROWS
    n_blocks = rows // r
    n_off = 2 * NA_WIN_R - 1
    kcol = np.arange(GRID_W)[:, None]
    qc = np.arange(GRID_W)[None, :]
    c_start = np.clip(qc - NA_WIN_C // 2, 0, GRID_W - NA_WIN_C)
    col_ok = (kcol >= c_start) & (kcol < c_start + NA_WIN_C)
    col_sel = (kcol - qc + NA_WIN_C - 1)[:, :, None] == np.arange(2 * NA_WIN_C - 1)
    tiles = jnp.einsum('lhij,kqj->lihkq', rpb, jnp.asarray(col_sel, F32), precision=lax.Precision.HIGHEST)
    tiles = jnp.where(col_ok, tiles * LOG2E, NEG_INF)
    tile_of = np.full((3, band, r), n_off, np.int64)
    for vi, blk in enumerate((0, 1, n_blocks - 1)):
        r0 = blk * r
        b0 = int(np.clip(r0 - NA_WIN_R // 2, 0, rows - band))
        for ki in range(band):
            for qi in range(r):
                r_start = int(np.clip(r0 + qi - NA_WIN_R // 2, 0, rows - NA_WIN_R))
                if r_start <= b0 + ki < r_start + NA_WIN_R:
                    tile_of[vi, ki, qi] = b0 + ki - (r0 + qi) + NA_WIN_R - 1

    def assemble(tiles_ref, o_ref):
        for vi in range(3):
            for ki in range(band):
                for qi in range(r):
                    dst = (vi, slice(None), slice(ki * GRID_W, (ki + 1) * GRID_W),
                           slice(qi * GRID_W, (qi + 1) * GRID_W))
                    idx = int(tile_of[vi, ki, qi])
                    if idx == n_off:
                        o_ref[dst] = jnp.full((NA_HEADS, GRID_W, GRID_W), NEG_INF, F32)
                    else:
                        o_ref[dst] = tiles_ref[idx]

    return pl.pallas_call(
        assemble,
        out_shape=jax.ShapeDtypeStruct((depth, 3, NA_HEADS, band * GRID_W, r * GRID_W), F32),
        grid=(depth,),
        in_specs=[pl.BlockSpec((None, n_off, NA_HEADS, GRID_W, GRID_W), lambda l: (l, 0, 0, 0, 0))],
        out_specs=pl.BlockSpec((None, 3, NA_HEADS, band * GRID_W, r * GRID_W), lambda l: (l, 0, 0, 0, 0)),
        compiler_params=_params("parallel"),
        name="na_bias",
    )(tiles)


def _na(q, k, v, kc, vc, bias_tabs, layer, rows):
    b, s, w = q.shape
    lc = kc.shape[1]
    n_steps = rows // (NA_BLOCK_ROWS * NA_BLOCKS_PER_STEP)
    tq = NA_BLOCK_ROWS * NA_BLOCKS_PER_STEP * GRID_W
    return pl.pallas_call(
        functools.partial(_na_kernel, rows=rows),
        out_shape=jax.ShapeDtypeStruct((b, s, w), BF16),
        grid=(b, n_steps),
        in_specs=[
            pl.BlockSpec((None, tq, w), lambda i, j: (i, j, 0)),
            pl.BlockSpec((None, s, w), lambda i, j: (i, 0, 0)),
            pl.BlockSpec((None, s, w), lambda i, j: (i, 0, 0)),
            pl.BlockSpec((None, lc, w), lambda i, j: (i, 0, 0)),
            pl.BlockSpec((None, lc, w), lambda i, j: (i, 0, 0)),
            _stacked(bias_tabs, (layer,)),
        ],
        out_specs=pl.BlockSpec((None, tq, w), lambda i, j: (i, j, 0)),
        compiler_params=_params("parallel", "arbitrary"),
        name="na",
    )(q, k, v, kc, vc, bias_tabs)


def _ctx_attn_kernel(q_ref, k_ref, v_ref, o_ref):
    def operands(t, s, parity):
        sl = slice(s * SLAB, (s + 1) * SLAB)
        return [k_ref[:, sl]], [v_ref[:, sl]], [None]

    _attend_all_heads(q_ref, o_ref, 1, operands)


def _ctx_attn(q, k, v):
    b, lc, w = q.shape
    spec = pl.BlockSpec((None, lc, w), lambda i: (i, 0, 0))
    return pl.pallas_call(
        _ctx_attn_kernel,
        out_shape=jax.ShapeDtypeStruct((b, lc, w), BF16),
        grid=(b,),
        in_specs=[spec, spec, spec],
        out_specs=spec,
        compiler_params=_params("parallel"),
        name="ctx_attn",
    )(q, k, v)


def _ret_kernel(qf_ref, kf_ref, vf_ref, qb_ref, kb_ref, vb_ref, ks_ref, vs_ref, ws_ref, dm_ref, te_ref,
                fs_ref, cd_ref, of_ref, ob_ref, state_ref, *, n_chunks):
    n = pl.program_id(1)
    same_head = ((lax.broadcasted_iota(jnp.int32, (SLAB, SLAB), 0) >> HEAD_SHIFT)
                 == (lax.broadcasted_iota(jnp.int32, (SLAB, SLAB), 1) >> HEAD_SHIFT))
    even = lax.broadcasted_iota(jnp.int32, (CHUNK, SLAB), 1) < HEAD_DIM

    @pl.when(n == 0)
    def _seed():
        for d in range(2):
            for s in range(RET_WIDTH // SLAB):
                sl = slice(s * SLAB, (s + 1) * SLAB)
                kw = (ks_ref[:, sl].astype(F32) * ws_ref[d, :, sl]).astype(BF16)
                state_ref[d, s] = jnp.where(same_head, _dot_tn(kw, vs_ref[:, sl]), 0.0)

    streams = ((qf_ref, kf_ref, vf_ref, of_ref), (qb_ref, kb_ref, vb_ref, ob_ref))
    chunk = lambda c: slice(c * CHUNK, (c + 1) * CHUNK)
    units = []
    for s in range(RET_WIDTH // SLAB):
        sl = slice(s * SLAB, (s + 1) * SLAB)
        for d, (_, k_ref, v_ref, _) in enumerate(streams):
            order = [j if d == 0 else n_chunks - 1 - j for j in range(n_chunks)]
            kvs = []
            for c in order:
                kt = (k_ref[chunk(c), sl].astype(F32) * te_ref[d, :, sl]).astype(BF16)
                kvs.append(jnp.where(same_head, _dot_tn(kt, v_ref[chunk(c), sl]), 0.0))
            st = state_ref[d, s]
            for c, kv in zip(order, kvs):
                units.append((s, d, c, st))
                st = cd_ref[d, s] * st + kv
            state_ref[d, s] = st

    def decayed_scores(u):
        s, d, c, _ = units[u]
        sl = slice(s * SLAB, (s + 1) * SLAB)
        q = streams[d][0][chunk(c), sl]
        k = streams[d][1][chunk(c), sl]
        zero = jnp.zeros_like(k)
        k2 = jnp.concatenate([jnp.where(even, k, zero), jnp.where(even, zero, k)], axis=0)
        return (_dot_nt(q, k2) * dm_ref[d, s]).astype(BF16)

    pending = [decayed_scores(u) for u in range(min(RET_SCORES_AHEAD, len(units)))]
    for u, (s, d, c, st) in enumerate(units):
        sl = slice(s * SLAB, (s + 1) * SLAB)
        scores = pending.pop(0)
        if u + RET_SCORES_AHEAD < len(units):
            pending.append(decayed_scores(u + RET_SCORES_AHEAD))
        q_ref, _, v_ref, o_ref = streams[d]
        v = v_ref[chunk(c), sl]
        zero = jnp.zeros_like(v)
        v2 = jnp.concatenate([jnp.where(even, v, zero), jnp.where(even, zero, v)], axis=0)
        qf = (q_ref[chunk(c), sl].astype(F32) * fs_ref[d, :, sl]).astype(BF16)
        o_ref[chunk(c), sl] = _dot(scores, v2) + _dot(qf, st.astype(BF16))


def _ret_tables(log_g, seed_len, seeded):
    pos = jnp.arange(CHUNK, dtype=F32)
    diff = pos[:, None] - pos[None, :]
    lg = log_g[:, :, None, None]
    dm_f = jnp.where(diff >= 0, jnp.exp(jnp.maximum(diff, 0.0)[None, None] * lg), 0.0)[0]
    dm_b = jnp.where(diff <= 0, jnp.exp(jnp.maximum(-diff, 0.0)[None, None] * lg), 0.0)[1]
    dm = jnp.stack([dm_f, dm_b])
    dm = dm.reshape(2, RET_HEADS // 2, 2, CHUNK, CHUNK).transpose(0, 1, 3, 2, 4)
    dm = dm.reshape(2, RET_HEADS // 2, CHUNK, 2 * CHUNK)
    lane = lambda t: jnp.repeat(t, HEAD_DIM, axis=-1)
    te = jnp.stack([jnp.exp((CHUNK - 1 - pos)[:, None] * log_g[0][None, :]),
                    jnp.exp(pos[:, None] * log_g[1][None, :])])
    fs = jnp.stack([jnp.exp((pos + 1.0)[:, None] * log_g[0][None, :]),
                    jnp.exp((CHUNK - pos)[:, None] * log_g[1][None, :])])
    cd = jnp.broadcast_to(lane(jnp.exp(CHUNK * log_g)).reshape(2, RET_HEADS // 2, SLAB, 1),
                          (2, RET_HEADS // 2, SLAB, SLAB))
    sp = jnp.arange(seed_len, dtype=F32)
    ws = jnp.stack([jnp.exp((seed_len - 1 - sp)[:, None] * log_g[0][None, :]),
                    jnp.exp(sp[:, None] * log_g[1][None, :])])
    if not seeded:
        ws = jnp.zeros_like(ws)
    return dm, lane(te), lane(fs), cd, lane(ws)


def _retention(q, k, v, k_seed, v_seed, tables, tb):
    b, l, w = q.shape
    ls = k_seed.shape[1]
    nb = l // tb
    dm, te, fs, cd, ws = tables
    fwd = pl.BlockSpec((None, tb, w), lambda i, n: (i, n, 0))
    bwd = pl.BlockSpec((None, tb, w), lambda i, n: (i, nb - 1 - n, 0))
    seed = pl.BlockSpec((None, ls, w), lambda i, n: (i, 0, 0))
    const = lambda shape: _resident(shape, lambda i, n: (0,) * len(shape))
    out = jax.ShapeDtypeStruct((b, l, w), F32)
    return pl.pallas_call(
        functools.partial(_ret_kernel, n_chunks=tb // CHUNK),
        out_shape=(out, out),
        grid=(b, nb),
        in_specs=[
            fwd, fwd, fwd, bwd, bwd, bwd, seed, seed,
            const(ws.shape), const(dm.shape), const(te.shape), const(fs.shape), const(cd.shape),
        ],
        out_specs=(fwd, bwd),
        scratch_shapes=[pltpu.VMEM((2, RET_HEADS // 2, SLAB, SLAB), F32)],
        compiler_params=_params("parallel", "arbitrary"),
        name="retention",
    )(q, k, v, q, k, v, k_seed, v_seed, ws, dm, te, fs, cd)


def _head_group_norm(o, ones_blk):
    mu = _seg_sum(o, ones_blk) * (1.0 / HEAD_DIM)
    dlt = o - mu
    var = _seg_sum(dlt * dlt, ones_blk) * (1.0 / HEAD_DIM)
    return dlt * lax.rsqrt(var + LN_EPS)


def _mix_out_ffn_kernel(x_ref, mod_ref, na_ref, sg_ref, of_ref, ob_ref, gf_ref, gb_ref, gnw_ref,
                        w_ref, nw_ref, w1_ref, w2_ref, o_ref):
    seg_ones = _seg_ones()
    y = (_head_group_norm(of_ref[...], seg_ones) * gf_ref[...]
         + _head_group_norm(ob_ref[...], seg_ones) * gb_ref[...])
    ret = (y * gnw_ref[...]).astype(BF16)
    o_sg = NA_WIDTH + SG_WIDTH
    out = (_dot(na_ref[...], w_ref[0:NA_WIDTH, :])
           + _dot(sg_ref[...], w_ref[NA_WIDTH:o_sg, :])
           + _dot(ret, w_ref[o_sg:, :]))
    x = x_ref[...] + mod_ref[5:6, :] * out
    o_ref[...] = _ffn_half_step(x, mod_ref, nw_ref, w1_ref, w2_ref, 6)


def _mix_out_ffn(x, mod, na, sg, o_fwd, o_bwd, gf, gb, gnw, w_out, nw, w1, w2, layer, tm):
    b, l, d = x.shape
    mod_map = (lambda i, j: (i, 0, 0)) if mod.shape[0] == b else (lambda i, j: (0, 0, 0))
    tok = lambda w: pl.BlockSpec((None, tm, w), lambda i, j: (i, j, 0))
    const = lambda shape: _resident(shape, lambda i, j: (0,) * len(shape))
    return pl.pallas_call(
        _mix_out_ffn_kernel,
        out_shape=jax.ShapeDtypeStruct(x.shape, x.dtype),
        grid=(b, l // tm),
        in_specs=[
            tok(d),
            pl.BlockSpec((None, N_MOD, d), mod_map),
            tok(NA_WIDTH), tok(SG_WIDTH), tok(RET_WIDTH), tok(RET_WIDTH), tok(RET_WIDTH), tok(RET_WIDTH),
            const((1, RET_WIDTH)), _stacked(w_out, (layer,)),
            const((1, d)), _stacked(w1, (layer, 1)), _stacked(w2, (layer, 1)),
        ],
        out_specs=tok(d),
        compiler_params=_params("parallel", "parallel"),
        name="mix_out_ffn",
    )(x, mod, na, sg, o_fwd, o_bwd, gf, gb, gnw, w_out, nw.reshape(1, d), w1, w2)


def _rope_tables(n_lat):
    t = jnp.arange(n_lat)[:, None]
    lane = np.arange(SLAB)
    inv = ROPE_BASE ** (-jnp.asarray(lane % ROPE_FREQS, F32) / ROPE_FREQS)
    pos = jnp.where(lane % HEAD_DIM < HEAD_DIM // 2, t // GRID_W, t % GRID_W).astype(F32)
    ang = pos * inv[None, :]
    first = (lane // ROPE_FREQS) % 2 == 0
    sin = jnp.sin(ang)
    return jnp.cos(ang), jnp.where(first, -sin, 0.0), jnp.where(first, 0.0, sin)


def kernel(x, c, ctx, c_ctx, ada_w, ada_b, norm_w, ffn_w1, ffn_w2, mix_w_in, mix_w_out, na_q_norm,
           na_k_norm, na_rpb, sg_w, sg_b, sg_ln_w, sg_ln_b, ret_decay_logit, ret_gn_w):
    b, n_lat, d = x.shape
    lc = ctx.shape[1]
    depth = ada_w.shape[0]
    rows = n_lat // GRID_W
    assert d == D_MODEL and n_lat % (GRID_W * NA_BLOCK_ROWS * NA_BLOCKS_PER_STEP) == 0
    assert rows >= 2 * NA_BAND_ROWS - NA_WIN_R
    assert lc % CHUNK == 0
    tm_x = min(TILE_TOKENS, n_lat)
    tm_c = min(CTX_TILE_TOKENS, lc)
    tb_x = min(RET_BLOCK_TOKENS, n_lat)
    assert n_lat % tb_x == 0
    assert n_lat % tm_x == 0 and lc % tm_c == 0 and tm_x % CHUNK == 0 and tm_c % CHUNK == 0

    assert b + 1 <= COND_ROWS
    cc = jnp.zeros((COND_ROWS, d), F32).at[:b].set(c).at[b].set(c_ctx)
    mod = _adaln(cc, ada_w, ada_b)

    rope_x = _rope_tables(n_lat)
    rope_c = (jnp.ones((lc, SLAB), F32), jnp.zeros((lc, SLAB), F32), jnp.zeros((lc, SLAB), F32))
    w1 = ffn_w1.astype(BF16)
    w2 = ffn_w2.astype(BF16)
    w_in = mix_w_in.astype(BF16)
    w_out = mix_w_out.astype(BF16)
    sgw = sg_w.astype(BF16).transpose(0, 2, 1, 3).reshape(depth, CHUNK, SG_GROUPS * CHUNK)
    log_g = jax.nn.log_sigmoid(ret_decay_logit.astype(F32))
    bias_tabs = _na_bias_table(na_rpb, rows)

    for layer in range(depth):
        last = layer == depth - 1
        mod_x = mod[layer, :b].reshape(b, N_MOD, d)
        mod_c = mod[layer, b:b + 1].reshape(1, N_MOD, d)
        nw = norm_w[layer]
        x = _ffn(x, mod_x, 0, nw[0], w1, w2, (layer, 0), tm_x)
        ctx = _ffn(ctx, mod_c, 0, nw[0], w1, w2, (layer, 0), tm_c)

        qn = jnp.tile(na_q_norm[layer], NA_HEADS).reshape(1, NA_WIDTH)
        kn = jnp.tile(na_k_norm[layer], NA_HEADS).reshape(1, NA_WIDTH)
        sgb_tab = jnp.repeat(sg_b[layer].T, HEAD_DIM, axis=1)
        lnw = sg_ln_w[layer].reshape(1, SG_WIDTH)
        lnb = sg_ln_b[layer].reshape(1, SG_WIDTH)
        gnw = ret_gn_w[layer].reshape(1, RET_WIDTH)
        px = _mix_in(x, mod_x, nw[1], w_in, layer, qn, kn, rope_x, sgw[layer], sgb_tab, lnw, lnb, tm_x)
        pc = _mix_in(ctx, mod_c, nw[1], w_in, layer, qn, kn, rope_c, sgw[layer], sgb_tab, lnw, lnb, tm_c)
        nq_x, nk_x, nv_x, sg_x, rq_x, rk_x, rv_x, gf_x, gb_x = px
        nq_c, nk_c, nv_c, sg_c, rq_c, rk_c, rv_c, gf_c, gb_c = pc

        na_x = _na(nq_x, nk_x, nv_x, nk_c, nv_c, bias_tabs, layer, rows)
        of_x, ob_x = _retention(rq_x, rk_x, rv_x, rk_c, rv_c, _ret_tables(log_g[layer], lc, True), tb_x)
        x = _mix_out_ffn(x, mod_x, na_x, sg_x, of_x, ob_x, gf_x, gb_x, gnw, w_out, nw[2], w1, w2,
                         layer, tm_x)
        if not last:
            na_c = _ctx_attn(nq_c, nk_c, nv_c)
            of_c, ob_c = _retention(rq_c, rk_c, rv_c, rk_c, rv_c, _ret_tables(log_g[layer], lc, False),
                                    tm_c)
            ctx = _mix_out_ffn(ctx, mod_c, na_c, sg_c, of_c, ob_c, gf_c, gb_c, gnw, w_out, nw[2], w1, w2,
                               layer, tm_c)
    return x
```

```python
import functools

import numpy as np
import jax
import jax.numpy as jnp
from jax import lax
from jax.experimental import pallas as pl
from jax.experimental.pallas import tpu as pltpu

D_MODEL = 1024
GRID_W = 64
N_MOD = 9
D_FF = 2816
NA_HEADS = 6
HEAD_DIM = 64
NA_WIDTH = NA_HEADS * HEAD_DIM
NA_WIN_R = 8
NA_WIN_C = 16
SG_GROUPS = 4
SG_WIDTH = SG_GROUPS * HEAD_DIM
CHUNK = 128
RET_HEADS = 6
RET_WIDTH = RET_HEADS * HEAD_DIM
IN_COLS = 3 * NA_WIDTH + 2 * SG_WIDTH + 5 * RET_WIDTH
ROPE_BASE = 10000.0
ROPE_FREQS = HEAD_DIM // 4
RMS_EPS = 1e-6
LN_EPS = 1e-5
NEG_INF = -1e30
LOG2E = 1.4426950408889634

OFF_NQ, OFF_NK, OFF_NV = 0, NA_WIDTH, 2 * NA_WIDTH
OFF_SU = 3 * NA_WIDTH
OFF_SV = OFF_SU + SG_WIDTH
OFF_RQ = OFF_SV + SG_WIDTH
OFF_RK = OFF_RQ + RET_WIDTH
OFF_RV = OFF_RK + RET_WIDTH
OFF_GF = OFF_RV + RET_WIDTH
OFF_GB = OFF_GF + RET_WIDTH

V7X_VMEM_LIMIT_BYTES = 56 * 1024 * 1024
SLAB = 128
HEAD_SHIFT = 6
MXU_DEPTH = 256
TILE_TOKENS = 512
CTX_TILE_TOKENS = 256
RET_BLOCK_TOKENS = 2048
COND_ROWS = 16
NA_BLOCK_ROWS = 4
NA_BAND_ROWS = NA_BLOCK_ROWS + NA_WIN_R
NA_LOGITS_AHEAD = 2
NA_BLOCKS_PER_STEP = 8
RET_SCORES_AHEAD = 5
FFN_CHUNKS = ((0, 1536), (1536, 1280))

BF16 = jnp.bfloat16
F32 = jnp.float32


def _params(*sem):
    return pltpu.CompilerParams(dimension_semantics=sem, vmem_limit_bytes=V7X_VMEM_LIMIT_BYTES)


def _resident(shape, index_map):
    return pl.BlockSpec(shape, index_map, pipeline_mode=pl.Buffered(1))


def _stacked(w, lead):
    n = len(lead)
    return _resident((None,) * n + w.shape[n:], lambda i, j: tuple(lead) + (0,) * (w.ndim - n))


def _dot(a, b):
    return jnp.dot(a, b, preferred_element_type=F32)


def _dot_nt(a, b):
    return lax.dot_general(a, b, (((1,), (1,)), ((), ())), preferred_element_type=F32)


def _dot_tn(a, b):
    return lax.dot_general(a, b, (((0,), (0,)), ((), ())), preferred_element_type=F32)


def _seg_ones():
    assert MXU_DEPTH == 2 * SLAB
    r = lax.broadcasted_iota(jnp.int32, (MXU_DEPTH, SLAB), 0)
    c = lax.broadcasted_iota(jnp.int32, (MXU_DEPTH, SLAB), 1)
    return jnp.where(((r >> HEAD_SHIFT) & 1) == (c >> HEAD_SHIFT), 1.0, 0.0).astype(BF16)


def _seg_sum(t, seg_ones):
    hi = t.astype(BF16)
    lo = (t - hi.astype(F32)).astype(BF16)
    outs = []
    for s in range(t.shape[1] // SLAB):
        sl = slice(s * SLAB, (s + 1) * SLAB)
        outs.append(_dot(jnp.concatenate([hi[:, sl], lo[:, sl]], axis=1), seg_ones))
    return jnp.concatenate(outs, axis=1)


def _ada_rms(x, nw, shift, scale):
    ms = jnp.mean(x * x, axis=-1, keepdims=True)
    return (x * lax.rsqrt(ms + RMS_EPS)) * nw * (1.0 + scale) + shift


def _adaln_kernel(c_ref, w_ref, b_ref, o_ref):
    a = jax.nn.silu(c_ref[...])
    o_ref[...] = jnp.dot(a, w_ref[...], precision=lax.Precision.HIGHEST,
                         preferred_element_type=F32) + b_ref[...]


def _adaln(cc, ada_w, ada_b):
    depth = ada_w.shape[0]
    rows = cc.shape[0]
    tn = D_MODEL
    return pl.pallas_call(
        _adaln_kernel,
        out_shape=jax.ShapeDtypeStruct((depth, rows, N_MOD * D_MODEL), F32),
        grid=(depth, N_MOD * D_MODEL // tn),
        in_specs=[
            pl.BlockSpec((rows, D_MODEL), lambda l, j: (0, 0)),
            pl.BlockSpec((None, D_MODEL, tn), lambda l, j: (l, 0, j)),
            pl.BlockSpec((None, 1, tn), lambda l, j: (l, 0, j)),
        ],
        out_specs=pl.BlockSpec((None, rows, tn), lambda l, j: (l, 0, j)),
        compiler_params=_params("arbitrary", "arbitrary"),
        name="adaln",
    )(cc, ada_w, ada_b.reshape(depth, 1, N_MOD * D_MODEL))


def _ffn_half_step(x, mod_ref, nw_ref, w1_ref, w2_ref, mod_off):
    shift = mod_ref[mod_off:mod_off + 1, :]
    scale = mod_ref[mod_off + 1:mod_off + 2, :]
    gate = mod_ref[mod_off + 2:mod_off + 3, :]
    xb = _ada_rms(x, nw_ref[...], shift, scale).astype(BF16)
    acc = None
    for off, width in FFN_CHUNKS:
        g = _dot(xb, w1_ref[:, off:off + width])
        u = _dot(xb, w1_ref[:, D_FF + off:D_FF + off + width])
        a = (jax.nn.silu(g) * u).astype(BF16)
        part = _dot(a, w2_ref[off:off + width, :])
        acc = part if acc is None else acc + part
    return x + (0.5 * gate) * acc


def _ffn_kernel(x_ref, mod_ref, nw_ref, w1_ref, w2_ref, o_ref, *, mod_off):
    o_ref[...] = _ffn_half_step(x_ref[...], mod_ref, nw_ref, w1_ref, w2_ref, mod_off)


def _ffn(x, mod, mod_off, nw, w1, w2, lead, tm):
    b, l, d = x.shape
    mod_map = (lambda i, j: (i, 0, 0)) if mod.shape[0] == b else (lambda i, j: (0, 0, 0))
    return pl.pallas_call(
        functools.partial(_ffn_kernel, mod_off=mod_off),
        out_shape=jax.ShapeDtypeStruct(x.shape, x.dtype),
        grid=(b, l // tm),
        in_specs=[
            pl.BlockSpec((None, tm, d), lambda i, j: (i, j, 0)),
            pl.BlockSpec((None, N_MOD, d), mod_map),
            _resident((1, d), lambda i, j: (0, 0)),
            _stacked(w1, lead),
            _stacked(w2, lead),
        ],
        out_specs=pl.BlockSpec((None, tm, d), lambda i, j: (i, j, 0)),
        compiler_params=_params("parallel", "parallel"),
        name="ffn",
    )(x, mod, nw.reshape(1, d), w1, w2)


def _head_rms(t, w, ones_blk):
    ms = _seg_sum(t * t, ones_blk) * (1.0 / HEAD_DIM)
    return (t * lax.rsqrt(ms + RMS_EPS)) * w


def _rope(t, cos, sin_next, sin_prev):
    outs = []
    for s in range(RET_WIDTH // SLAB):
        ts = t[:, s * SLAB:(s + 1) * SLAB]
        nxt = pltpu.roll(ts, SLAB - ROPE_FREQS, axis=1)
        prv = pltpu.roll(ts, ROPE_FREQS, axis=1)
        outs.append(ts * cos + nxt * sin_next + prv * sin_prev)
    return jnp.concatenate(outs, axis=1)


def _mix_in_kernel(x_ref, mod_ref, nw_ref, w_ref, qn_ref, kn_ref, cos_ref, sn_ref, sp_ref,
                   sgw_ref, sgb_ref, lnw_ref, lnb_ref,
                   nq_ref, nk_ref, nv_ref, sg_ref, rq_ref, rk_ref, rv_ref, gf_ref, gb_ref):
    tm = x_ref.shape[0]
    x = x_ref[...]
    hb = _ada_rms(x, nw_ref[...], mod_ref[3:4, :], mod_ref[4:5, :]).astype(BF16)
    p = _dot(hb, w_ref[...])
    seg_ones = _seg_ones()
    scale = HEAD_DIM ** -0.5

    nq_ref[...] = (_head_rms(p[:, OFF_NQ:OFF_NQ + NA_WIDTH], qn_ref[...], seg_ones)
                   * (scale * LOG2E)).astype(BF16)
    nk_ref[...] = _head_rms(p[:, OFF_NK:OFF_NK + NA_WIDTH], kn_ref[...], seg_ones).astype(BF16)
    nv_ref[...] = p[:, OFF_NV:OFF_NV + NA_WIDTH].astype(BF16)

    u = jax.nn.gelu(p[:, OFF_SU:OFF_SU + SG_WIDTH])
    v = jax.nn.gelu(p[:, OFF_SV:OFF_SV + SG_WIDTH])
    mu = _seg_sum(v, seg_ones) * (1.0 / HEAD_DIM)
    dv = v - mu
    var = _seg_sum(dv * dv, seg_ones) * (1.0 / HEAD_DIM)
    vn = ((dv * lax.rsqrt(var + LN_EPS)) * lnw_ref[...] + lnb_ref[...]).astype(BF16)
    lane_grp = lax.broadcasted_iota(jnp.int32, (CHUNK, SG_WIDTH), 1) >> HEAD_SHIFT
    for j in range(tm // CHUNK):
        vch = vn[j * CHUNK:(j + 1) * CHUNK, :]
        stacked = jnp.concatenate(
            [jnp.where(lane_grp == g, vch, jnp.zeros_like(vch)) for g in range(SG_GROUPS)], axis=0)
        mixed = _dot(sgw_ref[...], stacked) + sgb_ref[...]
        sg_ref[j * CHUNK:(j + 1) * CHUNK, :] = (u[j * CHUNK:(j + 1) * CHUNK, :] * mixed).astype(BF16)

    cos, sn, sp = cos_ref[...], sn_ref[...], sp_ref[...]
    rq_ref[...] = _rope(p[:, OFF_RQ:OFF_RQ + RET_WIDTH], cos, sn, sp).astype(BF16)
    rk_ref[...] = (_rope(p[:, OFF_RK:OFF_RK + RET_WIDTH], cos, sn, sp) * scale).astype(BF16)
    rv_ref[...] = p[:, OFF_RV:OFF_RV + RET_WIDTH].astype(BF16)
    gf_ref[...] = jax.nn.silu(p[:, OFF_GF:OFF_GF + RET_WIDTH])
    gb_ref[...] = jax.nn.silu(p[:, OFF_GB:OFF_GB + RET_WIDTH])


def _mix_in(x, mod, nw, w_in, layer, qn, kn, rope_tabs, sgw, sgb_tab, lnw, lnb, tm):
    b, l, d = x.shape
    mod_map = (lambda i, j: (i, 0, 0)) if mod.shape[0] == b else (lambda i, j: (0, 0, 0))
    tok = lambda w: pl.BlockSpec((None, tm, w), lambda i, j: (i, j, 0))
    tab = pl.BlockSpec((tm, SLAB), lambda i, j: (j, 0))
    const = lambda shape: _resident(shape, lambda i, j: (0,) * len(shape))
    out_bf = lambda w: jax.ShapeDtypeStruct((b, l, w), BF16)
    out_f32 = lambda w: jax.ShapeDtypeStruct((b, l, w), F32)
    return pl.pallas_call(
        _mix_in_kernel,
        out_shape=(out_bf(NA_WIDTH), out_bf(NA_WIDTH), out_bf(NA_WIDTH), out_bf(SG_WIDTH),
                   out_bf(RET_WIDTH), out_bf(RET_WIDTH), out_bf(RET_WIDTH),
                   out_f32(RET_WIDTH), out_f32(RET_WIDTH)),
        grid=(b, l // tm),
        in_specs=[
            tok(d),
            pl.BlockSpec((None, N_MOD, d), mod_map),
            const((1, d)),
            _stacked(w_in, (layer,)),
            const((1, NA_WIDTH)), const((1, NA_WIDTH)),
            tab, tab, tab,
            const((CHUNK, SG_GROUPS * CHUNK)), const((CHUNK, SG_WIDTH)),
            const((1, SG_WIDTH)), const((1, SG_WIDTH)),
        ],
        out_specs=(tok(NA_WIDTH), tok(NA_WIDTH), tok(NA_WIDTH), tok(SG_WIDTH),
                   tok(RET_WIDTH), tok(RET_WIDTH), tok(RET_WIDTH), tok(RET_WIDTH), tok(RET_WIDTH)),
        compiler_params=_params("parallel", "parallel"),
        name="mix_in",
    )(x, mod, nw.reshape(1, d), w_in, qn, kn, *rope_tabs, sgw, sgb_tab, lnw, lnb)


def _own_lanes(shape, parity):
    even = lax.broadcasted_iota(jnp.int32, shape, 1) < HEAD_DIM
    return even if parity == 0 else jnp.logical_not(even)


def _scores_t(q, parity, keys, biases):
    qp = jnp.where(_own_lanes(q.shape, parity), q, jnp.zeros_like(q))
    scores = [_dot_nt(kk, qp) if bias is None else _dot_nt(kk, qp) + bias for kk, bias in zip(keys, biases)]
    mx = functools.reduce(jnp.maximum, [jnp.max(s, axis=0, keepdims=True) for s in scores])
    return scores, mx


def _weighted_values_t(scores, mx, parity, vals):
    num = None
    for s, vv in zip(scores, vals):
        vp = jnp.where(_own_lanes(vv.shape, parity), vv, jnp.ones_like(vv))
        part = _dot_tn(vp, jnp.exp2((s - mx).astype(BF16)))
        num = part if num is None else num + part
    return num


def _normalised_pair(num_even, num_odd):
    even_row = lax.broadcasted_iota(jnp.int32, num_even.shape, 0) < HEAD_DIM
    out_t = jnp.where(even_row, num_even / num_even[HEAD_DIM:HEAD_DIM + 1, :], num_odd / num_odd[0:1, :])
    return out_t.T


def _attend_all_heads(q_ref, o_ref, n_sub, operands):
    m = q_ref.shape[0] // n_sub
    units = [(t, s, parity) for t in range(n_sub) for s in range(NA_WIDTH // SLAB) for parity in range(2)]

    def logits(u):
        t, s, parity = units[u]
        keys, vals, biases = operands(t, s, parity)
        return _scores_t(q_ref[t * m:(t + 1) * m, s * SLAB:(s + 1) * SLAB], parity, keys, biases), vals

    pending = [logits(u) for u in range(NA_LOGITS_AHEAD)]
    nums = []
    for u, (t, s, parity) in enumerate(units):
        (scores, mx), vals = pending.pop(0)
        if u + NA_LOGITS_AHEAD < len(units):
            pending.append(logits(u + NA_LOGITS_AHEAD))
        nums.append(_weighted_values_t(scores, mx, parity, vals))
        if parity == 1:
            o_ref[t * m:(t + 1) * m, s * SLAB:(s + 1) * SLAB] = (
                _normalised_pair(nums[-2], nums[-1]).astype(BF16))


def _na_kernel(q_ref, k_ref, v_ref, kc_ref, vc_ref, bias_ref, o_ref, *, rows):
    n_blocks = rows // NA_BLOCK_ROWS

    def operands(t, s, parity):
        blk = pl.program_id(1) * NA_BLOCKS_PER_STEP + t
        band_row = jnp.clip(blk * NA_BLOCK_ROWS - NA_WIN_R // 2, 0, rows - NA_BAND_ROWS)
        band = pl.ds(pl.multiple_of(band_row * GRID_W, GRID_W), NA_BAND_ROWS * GRID_W)
        variant = jnp.where(blk == 0, 0, jnp.where(blk == n_blocks - 1, 2, 1))
        sl = slice(s * SLAB, (s + 1) * SLAB)
        return ([k_ref[band, sl], kc_ref[:, sl]], [v_ref[band, sl], vc_ref[:, sl]],
                [bias_ref[variant, 2 * s + parity], None])

    _attend_all_heads(q_ref, o_ref, NA_BLOCKS_PER_STEP, operands)


def _na_bias_table(rpb, rows):
    depth = rpb.shape[0]
    r, band = NA_BLOCK_ROWS, NA_BAND_ROWS
    n_blocks = rows // r
    n_off = 2 * NA_WIN_R - 1
    kcol = np.arange(GRID_W)[:, None]
    qc = np.arange(GRID_W)[None, :]
    c_start = np.clip(qc - NA_WIN_C // 2, 0, GRID_W - NA_WIN_C)
    col_ok = (kcol >= c_start) & (kcol < c_start + NA_WIN_C)
    col_sel = (kcol - qc + NA_WIN_C - 1)[:, :, None] == np.arange(2 * NA_WIN_C - 1)
    tiles = jnp.einsum('lhij,kqj->lihkq', rpb, jnp.asarray(col_sel, F32), precision=lax.Precision.HIGHEST)
    tiles = jnp.where(col_ok, tiles * LOG2E, NEG_INF)
    tile_of = np.full((3, band, r), n_off, np.int64)
    for vi, blk in enumerate((0, 1, n_blocks - 1)):
        r0 = blk * r
        b0 = int(np.clip(r0 - NA_WIN_R // 2, 0, rows - band))
        for ki in range(band):
            for qi in range(r):
                r_start = int(np.clip(r0 + qi - NA_WIN_R // 2, 0, rows - NA_WIN_R))
                if r_start <= b0 + ki < r_start + NA_WIN_R:
                    tile_of[vi, ki, qi] = b0 + ki - (r0 + qi) + NA_WIN_R - 1

    def assemble(tiles_ref, o_ref):
        for vi in range(3):
            for ki in range(band):
                for qi in range(r):
                    dst = (vi, slice(None), slice(ki * GRID_W, (ki + 1) * GRID_W),
                           slice(qi * GRID_W, (qi + 1) * GRID_W))
                    idx = int(tile_of[vi, ki, qi])
                    if idx == n_off:
                        o_ref[dst] = jnp.full((NA_HEADS, GRID_W, GRID_W), NEG_INF, F32)
                    else:
                        o_ref[dst] = tiles_ref[idx]

    return pl.pallas_call(
        assemble,
        out_shape=jax.ShapeDtypeStruct((depth, 3, NA_HEADS, band * GRID_W, r * GRID_W), F32),
        grid=(depth,),
        in_specs=[pl.BlockSpec((None, n_off, NA_HEADS, GRID_W, GRID_W), lambda l: (l, 0, 0, 0, 0))],
        out_specs=pl.BlockSpec((None, 3, NA_HEADS, band * GRID_W, r * GRID_W), lambda l: (l, 0, 0, 0, 0)),
        compiler_params=_params("parallel"),
        name="na_bias",
    )(tiles)


def _na(q, k, v, kc, vc, bias_tabs, layer, rows):
    b, s, w = q.shape
    lc = kc.shape[1]
    n_steps = rows // (NA_BLOCK_ROWS * NA_BLOCKS_PER_STEP)
    tq = NA_BLOCK_ROWS * NA_BLOCKS_PER_STEP * GRID_W
    return pl.pallas_call(
        functools.partial(_na_kernel, rows=rows),
        out_shape=jax.ShapeDtypeStruct((b, s, w), BF16),
        grid=(b, n_steps),
        in_specs=[
            pl.BlockSpec((None, tq, w), lambda i, j: (i, j, 0)),
            pl.BlockSpec((None, s, w), lambda i, j: (i, 0, 0)),
            pl.BlockSpec((None, s, w), lambda i, j: (i, 0, 0)),
            pl.BlockSpec((None, lc, w), lambda i, j: (i, 0, 0)),
            pl.BlockSpec((None, lc, w), lambda i, j: (i, 0, 0)),
            _stacked(bias_tabs, (layer,)),
        ],
        out_specs=pl.BlockSpec((None, tq, w), lambda i, j: (i, j, 0)),
        compiler_params=_params("parallel", "arbitrary"),
        name="na",
    )(q, k, v, kc, vc, bias_tabs)


def _ctx_attn_kernel(q_ref, k_ref, v_ref, o_ref):
    def operands(t, s, parity):
        sl = slice(s * SLAB, (s + 1) * SLAB)
        return [k_ref[:, sl]], [v_ref[:, sl]], [None]

    _attend_all_heads(q_ref, o_ref, 1, operands)


def _ctx_attn(q, k, v):
    b, lc, w = q.shape
    spec = pl.BlockSpec((None, lc, w), lambda i: (i, 0, 0))
    return pl.pallas_call(
        _ctx_attn_kernel,
        out_shape=jax.ShapeDtypeStruct((b, lc, w), BF16),
        grid=(b,),
        in_specs=[spec, spec, spec],
        out_specs=spec,
        compiler_params=_params("parallel"),
        name="ctx_attn",
    )(q, k, v)


def _ret_kernel(qf_ref, kf_ref, vf_ref, qb_ref, kb_ref, vb_ref, ks_ref, vs_ref, ws_ref, dm_ref, te_ref,
                fs_ref, cd_ref, of_ref, ob_ref, state_ref, *, n_chunks):
    n = pl.program_id(1)
    same_head = ((lax.broadcasted_iota(jnp.int32, (SLAB, SLAB), 0) >> HEAD_SHIFT)
                 == (lax.broadcasted_iota(jnp.int32, (SLAB, SLAB), 1) >> HEAD_SHIFT))
    even = lax.broadcasted_iota(jnp.int32, (CHUNK, SLAB), 1) < HEAD_DIM

    @pl.when(n == 0)
    def _seed():
        for d in range(2):
            for s in range(RET_WIDTH // SLAB):
                sl = slice(s * SLAB, (s + 1) * SLAB)
                kw = (ks_ref[:, sl].astype(F32) * ws_ref[d, :, sl]).astype(BF16)
                state_ref[d, s] = jnp.where(same_head, _dot_tn(kw, vs_ref[:, sl]), 0.0)

    streams = ((qf_ref, kf_ref, vf_ref, of_ref), (qb_ref, kb_ref, vb_ref, ob_ref))
    chunk = lambda c: slice(c * CHUNK, (c + 1) * CHUNK)
    units = []
    for s in range(RET_WIDTH // SLAB):
        sl = slice(s * SLAB, (s + 1) * SLAB)
        for d, (_, k_ref, v_ref, _) in enumerate(streams):
            order = [j if d == 0 else n_chunks - 1 - j for j in range(n_chunks)]
            kvs = []
            for c in order:
                kt = (k_ref[chunk(c), sl].astype(F32) * te_ref[d, :, sl]).astype(BF16)
                kvs.append(jnp.where(same_head, _dot_tn(kt, v_ref[chunk(c), sl]), 0.0))
            st = state_ref[d, s]
            for c, kv in zip(order, kvs):
                units.append((s, d, c, st))
                st = cd_ref[d, s] * st + kv
            state_ref[d, s] = st

    def decayed_scores(u):
        s, d, c, _ = units[u]
        sl = slice(s * SLAB, (s + 1) * SLAB)
        q = streams[d][0][chunk(c), sl]
        k = streams[d][1][chunk(c), sl]
        zero = jnp.zeros_like(k)
        k2 = jnp.concatenate([jnp.where(even, k, zero), jnp.where(even, zero, k)], axis=0)
        return (_dot_nt(q, k2) * dm_ref[d, s]).astype(BF16)

    pending = [decayed_scores(u) for u in range(min(RET_SCORES_AHEAD, len(units)))]
    for u, (s, d, c, st) in enumerate(units):
        sl = slice(s * SLAB, (s + 1) * SLAB)
        scores = pending.pop(0)
        if u + RET_SCORES_AHEAD < len(units):
            pending.append(decayed_scores(u + RET_SCORES_AHEAD))
        q_ref, _, v_ref, o_ref = streams[d]
        v = v_ref[chunk(c), sl]
        zero = jnp.zeros_like(v)
        v2 = jnp.concatenate([jnp.where(even, v, zero), jnp.where(even, zero, v)], axis=0)
        qf = (q_ref[chunk(c), sl].astype(F32) * fs_ref[d, :, sl]).astype(BF16)
        o_ref[chunk(c), sl] = _dot(scores, v2) + _dot(qf, st.astype(BF16))


def _ret_tables(log_g, seed_len, seeded):
    pos = jnp.arange(CHUNK, dtype=F32)
    diff = pos[:, None] - pos[None, :]
    lg = log_g[:, :, None, None]
    dm_f = jnp.where(diff >= 0, jnp.exp(jnp.maximum(diff, 0.0)[None, None] * lg), 0.0)[0]
    dm_b = jnp.where(diff <= 0, jnp.exp(jnp.maximum(-diff, 0.0)[None, None] * lg), 0.0)[1]
    dm = jnp.stack([dm_f, dm_b])
    dm = dm.reshape(2, RET_HEADS // 2, 2, CHUNK, CHUNK).transpose(0, 1, 3, 2, 4)
    dm = dm.reshape(2, RET_HEADS // 2, CHUNK, 2 * CHUNK)
    lane = lambda t: jnp.repeat(t, HEAD_DIM, axis=-1)
    te = jnp.stack([jnp.exp((CHUNK - 1 - pos)[:, None] * log_g[0][None, :]),
                    jnp.exp(pos[:, None] * log_g[1][None, :])])
    fs = jnp.stack([jnp.exp((pos + 1.0)[:, None] * log_g[0][None, :]),
                    jnp.exp((CHUNK - pos)[:, None] * log_g[1][None, :])])
    cd = jnp.broadcast_to(lane(jnp.exp(CHUNK * log_g)).reshape(2, RET_HEADS // 2, SLAB, 1),
                          (2, RET_HEADS // 2, SLAB, SLAB))
    sp = jnp.arange(seed_len, dtype=F32)
    ws = jnp.stack([jnp.exp((seed_len - 1 - sp)[:, None] * log_g[0][None, :]),
                    jnp.exp(sp[:, None] * log_g[1][None, :])])
    if not seeded:
        ws = jnp.zeros_like(ws)
    return dm, lane(te), lane(fs), cd, lane(ws)


def _retention(q, k, v, k_seed, v_seed, tables, tb):
    b, l, w = q.shape
    ls = k_seed.shape[1]
    nb = l // tb
    dm, te, fs, cd, ws = tables
    fwd = pl.BlockSpec((None, tb, w), lambda i, n: (i, n, 0))
    bwd = pl.BlockSpec((None, tb, w), lambda i, n: (i, nb - 1 - n, 0))
    seed = pl.BlockSpec((None, ls, w), lambda i, n: (i, 0, 0))
    const = lambda shape: _resident(shape, lambda i, n: (0,) * len(shape))
    out = jax.ShapeDtypeStruct((b, l, w), F32)
    return pl.pallas_call(
        functools.partial(_ret_kernel, n_chunks=tb // CHUNK),
        out_shape=(out, out),
        grid=(b, nb),
        in_specs=[
            fwd, fwd, fwd, bwd, bwd, bwd, seed, seed,
            const(ws.shape), const(dm.shape), const(te.shape), const(fs.shape), const(cd.shape),
        ],
        out_specs=(fwd, bwd),
        scratch_shapes=[pltpu.VMEM((2, RET_HEADS // 2, SLAB, SLAB), F32)],
        compiler_params=_params("parallel", "arbitrary"),
        name="retention",
    )(q, k, v, q, k, v, k_seed, v_seed, ws, dm, te, fs, cd)


def _head_group_norm(o, ones_blk):
    mu = _seg_sum(o, ones_blk) * (1.0 / HEAD_DIM)
    dlt = o - mu
    var = _seg_sum(dlt * dlt, ones_blk) * (1.0 / HEAD_DIM)
    return dlt * lax.rsqrt(var + LN_EPS)


def _mix_out_ffn_kernel(x_ref, mod_ref, na_ref, sg_ref, of_ref, ob_ref, gf_ref, gb_ref, gnw_ref,
                        w_ref, nw_ref, w1_ref, w2_ref, o_ref):
    seg_ones = _seg_ones()
    y = (_head_group_norm(of_ref[...], seg_ones) * gf_ref[...]
         + _head_group_norm(ob_ref[...], seg_ones) * gb_ref[...])
    ret = (y * gnw_ref[...]).astype(BF16)
    o_sg = NA_WIDTH + SG_WIDTH
    out = (_dot(na_ref[...], w_ref[0:NA_WIDTH, :])
           + _dot(sg_ref[...], w_ref[NA_WIDTH:o_sg, :])
           + _dot(ret, w_ref[o_sg:, :]))
    x = x_ref[...] + mod_ref[5:6, :] * out
    o_ref[...] = _ffn_half_step(x, mod_ref, nw_ref, w1_ref, w2_ref, 6)


def _mix_out_ffn(x, mod, na, sg, o_fwd, o_bwd, gf, gb, gnw, w_out, nw, w1, w2, layer, tm):
    b, l, d = x.shape
    mod_map = (lambda i, j: (i, 0, 0)) if mod.shape[0] == b else (lambda i, j: (0, 0, 0))
    tok = lambda w: pl.BlockSpec((None, tm, w), lambda i, j: (i, j, 0))
    const = lambda shape: _resident(shape, lambda i, j: (0,) * len(shape))
    return pl.pallas_call(
        _mix_out_ffn_kernel,
        out_shape=jax.ShapeDtypeStruct(x.shape, x.dtype),
        grid=(b, l // tm),
        in_specs=[
            tok(d),
            pl.BlockSpec((None, N_MOD, d), mod_map),
            tok(NA_WIDTH), tok(SG_WIDTH), tok(RET_WIDTH), tok(RET_WIDTH), tok(RET_WIDTH), tok(RET_WIDTH),
            const((1, RET_WIDTH)), _stacked(w_out, (layer,)),
            const((1, d)), _stacked(w1, (layer, 1)), _stacked(w2, (layer, 1)),
        ],
        out_specs=tok(d),
        compiler_params=_params("parallel", "parallel"),
        name="mix_out_ffn",
    )(x, mod, na, sg, o_fwd, o_bwd, gf, gb, gnw, w_out, nw.reshape(1, d), w1, w2)


def _rope_tables(n_lat):
    t = jnp.arange(n_lat)[:, None]
    lane = np.arange(SLAB)
    inv = ROPE_BASE ** (-jnp.asarray(lane % ROPE_FREQS, F32) / ROPE_FREQS)
    pos = jnp.where(lane % HEAD_DIM < HEAD_DIM // 2, t // GRID_W, t % GRID_W).astype(F32)
    ang = pos * inv[None, :]
    first = (lane // ROPE_FREQS) % 2 == 0
    sin = jnp.sin(ang)
    return jnp.cos(ang), jnp.where(first, -sin, 0.0), jnp.where(first, 0.0, sin)


def kernel(x, c, ctx, c_ctx, ada_w, ada_b, norm_w, ffn_w1, ffn_w2, mix_w_in, mix_w_out, na_q_norm,
           na_k_norm, na_rpb, sg_w, sg_b, sg_ln_w, sg_ln_b, ret_decay_logit, ret_gn_w):
    b, n_lat, d = x.shape
    lc = ctx.shape[1]
    depth = ada_w.shape[0]
    rows = n_lat // GRID_W
    assert d == D_MODEL and n_lat % (GRID_W * NA_BLOCK_ROWS * NA_BLOCKS_PER_STEP) == 0
    assert rows >= 2 * NA_BAND_ROWS - NA_WIN_R
    assert lc % CHUNK == 0
    tm_x = min(TILE_TOKENS, n_lat)
    tm_c = min(CTX_TILE_TOKENS, lc)
    tb_x = min(RET_BLOCK_TOKENS, n_lat)
    assert n_lat % tb_x == 0
    assert n_lat % tm_x == 0 and lc % tm_c == 0 and tm_x % CHUNK == 0 and tm_c % CHUNK == 0

    assert b + 1 <= COND_ROWS
    cc = jnp.zeros((COND_ROWS, d), F32).at[:b].set(c).at[b].set(c_ctx)
    mod = _adaln(cc, ada_w, ada_b)

    rope_x = _rope_tables(n_lat)
    n_ctx = b * lc
    tm_cf = min(TILE_TOKENS, n_ctx)
    assert n_ctx % tm_cf == 0
    flat = lambda t: t.reshape(1, n_ctx, t.shape[-1])
    per_batch = lambda t: t.reshape(b, lc, t.shape[-1])
    ctx = flat(ctx)
    rope_c = (jnp.ones((n_ctx, SLAB), F32), jnp.zeros((n_ctx, SLAB), F32), jnp.zeros((n_ctx, SLAB), F32))
    w1 = ffn_w1.astype(BF16)
    w2 = ffn_w2.astype(BF16)
    w_in = mix_w_in.astype(BF16)
    w_out = mix_w_out.astype(BF16)
    sgw = sg_w.astype(BF16).transpose(0, 2, 1, 3).reshape(depth, CHUNK, SG_GROUPS * CHUNK)
    log_g = jax.nn.log_sigmoid(ret_decay_logit.astype(F32))
    bias_tabs = _na_bias_table(na_rpb, rows)

    for layer in range(depth):
        last = layer == depth - 1
        mod_x = mod[layer, :b].reshape(b, N_MOD, d)
        mod_c = mod[layer, b:b + 1].reshape(1, N_MOD, d)
        nw = norm_w[layer]
        x = _ffn(x, mod_x, 0, nw[0], w1, w2, (layer, 0), tm_x)
        ctx = _ffn(ctx, mod_c, 0, nw[0], w1, w2, (layer, 0), tm_cf)

        qn = jnp.tile(na_q_norm[layer], NA_HEADS).reshape(1, NA_WIDTH)
        kn = jnp.tile(na_k_norm[layer], NA_HEADS).reshape(1, NA_WIDTH)
        sgb_tab = jnp.repeat(sg_b[layer].T, HEAD_DIM, axis=1)
        lnw = sg_ln_w[layer].reshape(1, SG_WIDTH)
        lnb = sg_ln_b[layer].reshape(1, SG_WIDTH)
        gnw = ret_gn_w[layer].reshape(1, RET_WIDTH)
        px = _mix_in(x, mod_x, nw[1], w_in, layer, qn, kn, rope_x, sgw[layer], sgb_tab, lnw, lnb, tm_x)
        pc = _mix_in(ctx, mod_c, nw[1], w_in, layer, qn, kn, rope_c, sgw[layer], sgb_tab, lnw, lnb, tm_cf)
        nq_x, nk_x, nv_x, sg_x, rq_x, rk_x, rv_x, gf_x, gb_x = px
        nq_c, nk_c, nv_c, sg_c, rq_c, rk_c, rv_c, gf_c, gb_c = pc
        nq_c, nk_c, nv_c, rq_c, rk_c, rv_c = (per_batch(t) for t in (nq_c, nk_c, nv_c, rq_c, rk_c, rv_c))

        na_x = _na(nq_x, nk_x, nv_x, nk_c, nv_c, bias_tabs, layer, rows)
        of_x, ob_x = _retention(rq_x, rk_x, rv_x, rk_c, rv_c, _ret_tables(log_g[layer], lc, True), tb_x)
        x = _mix_out_ffn(x, mod_x, na_x, sg_x, of_x, ob_x, gf_x, gb_x, gnw, w_out, nw[2], w1, w2,
                         layer, tm_x)
        if not last:
            na_c = _ctx_attn(nq_c, nk_c, nv_c)
            of_c, ob_c = _retention(rq_c, rk_c, rv_c, rk_c, rv_c, _ret_tables(log_g[layer], lc, False),
                                    tm_c)
            ctx = _mix_out_ffn(ctx, mod_c, flat(na_c), sg_c, flat(of_c), flat(ob_c), gf_c, gb_c, gnw, w_out,
                               nw[2], w1, w2, layer, tm_cf)
    return x
```

```python
import functools

import numpy as np
import jax
import jax.numpy as jnp
from jax import lax
from jax.experimental import pallas as pl
from jax.experimental.pallas import tpu as pltpu

D_MODEL = 1024
GRID_W = 64
N_MOD = 9
D_FF = 2816
NA_HEADS = 6
HEAD_DIM = 64
NA_WIDTH = NA_HEADS * HEAD_DIM
NA_WIN_R = 8
NA_WIN_C = 16
SG_GROUPS = 4
SG_WIDTH = SG_GROUPS * HEAD_DIM
CHUNK = 128
RET_HEADS = 6
RET_WIDTH = RET_HEADS * HEAD_DIM
IN_COLS = 3 * NA_WIDTH + 2 * SG_WIDTH + 5 * RET_WIDTH
ROPE_BASE = 10000.0
ROPE_FREQS = HEAD_DIM // 4
RMS_EPS = 1e-6
LN_EPS = 1e-5
NEG_INF = -1e30
LOG2E = 1.4426950408889634

OFF_NQ, OFF_NK, OFF_NV = 0, NA_WIDTH, 2 * NA_WIDTH
OFF_SU = 3 * NA_WIDTH
OFF_SV = OFF_SU + SG_WIDTH
OFF_RQ = OFF_SV + SG_WIDTH
OFF_RK = OFF_RQ + RET_WIDTH
OFF_RV = OFF_RK + RET_WIDTH
OFF_GF = OFF_RV + RET_WIDTH
OFF_GB = OFF_GF + RET_WIDTH

V7X_VMEM_LIMIT_BYTES = 56 * 1024 * 1024
SLAB = 128
HEAD_SHIFT = 6
MXU_DEPTH = 256
TILE_TOKENS = 512
WIDE_TILE_TOKENS = 1024
CTX_TILE_TOKENS = 256
RET_BLOCK_TOKENS = 2048
COND_ROWS = 16
NA_BLOCK_ROWS = 4
NA_BAND_ROWS = NA_BLOCK_ROWS + NA_WIN_R
NA_LOGITS_AHEAD = 2
NA_BLOCKS_PER_STEP = 8
RET_SCORES_AHEAD = 5
FFN_CHUNKS = ((0, 768), (768, 768), (1536, 768), (2304, 512))

BF16 = jnp.bfloat16
F32 = jnp.float32


def _params(*sem):
    return pltpu.CompilerParams(dimension_semantics=sem, vmem_limit_bytes=V7X_VMEM_LIMIT_BYTES)


def _resident(shape, index_map):
    return pl.BlockSpec(shape, index_map, pipeline_mode=pl.Buffered(1))


def _stacked(w, lead):
    n = len(lead)
    return _resident((None,) * n + w.shape[n:], lambda i, j: tuple(lead) + (0,) * (w.ndim - n))


def _dot(a, b):
    return jnp.dot(a, b, preferred_element_type=F32)


def _dot_nt(a, b):
    return lax.dot_general(a, b, (((1,), (1,)), ((), ())), preferred_element_type=F32)


def _dot_tn(a, b):
    return lax.dot_general(a, b, (((0,), (0,)), ((), ())), preferred_element_type=F32)


def _seg_ones():
    assert MXU_DEPTH == 2 * SLAB
    r = lax.broadcasted_iota(jnp.int32, (MXU_DEPTH, SLAB), 0)
    c = lax.broadcasted_iota(jnp.int32, (MXU_DEPTH, SLAB), 1)
    return jnp.where(((r >> HEAD_SHIFT) & 1) == (c >> HEAD_SHIFT), 1.0, 0.0).astype(BF16)


def _seg_sum(t, seg_ones):
    hi = t.astype(BF16)
    lo = (t - hi.astype(F32)).astype(BF16)
    outs = []
    for s in range(t.shape[1] // SLAB):
        sl = slice(s * SLAB, (s + 1) * SLAB)
        outs.append(_dot(jnp.concatenate([hi[:, sl], lo[:, sl]], axis=1), seg_ones))
    return jnp.concatenate(outs, axis=1)


def _ada_rms(x, nw, shift, scale):
    ms = jnp.mean(x * x, axis=-1, keepdims=True)
    return (x * lax.rsqrt(ms + RMS_EPS)) * nw * (1.0 + scale) + shift


def _adaln_kernel(c_ref, w_ref, b_ref, o_ref):
    a = jax.nn.silu(c_ref[...])
    o_ref[...] = jnp.dot(a, w_ref[...], precision=lax.Precision.HIGHEST,
                         preferred_element_type=F32) + b_ref[...]


def _adaln(cc, ada_w, ada_b):
    depth = ada_w.shape[0]
    rows = cc.shape[0]
    tn = D_MODEL
    return pl.pallas_call(
        _adaln_kernel,
        out_shape=jax.ShapeDtypeStruct((depth, rows, N_MOD * D_MODEL), F32),
        grid=(depth, N_MOD * D_MODEL // tn),
        in_specs=[
            pl.BlockSpec((rows, D_MODEL), lambda l, j: (0, 0)),
            pl.BlockSpec((None, D_MODEL, tn), lambda l, j: (l, 0, j)),
            pl.BlockSpec((None, 1, tn), lambda l, j: (l, 0, j)),
        ],
        out_specs=pl.BlockSpec((None, rows, tn), lambda l, j: (l, 0, j)),
        compiler_params=_params("arbitrary", "arbitrary"),
        name="adaln",
    )(cc, ada_w, ada_b.reshape(depth, 1, N_MOD * D_MODEL))


def _ffn_half_step(x, mod_ref, nw_ref, w1_ref, w2_ref, mod_off):
    shift = mod_ref[mod_off:mod_off + 1, :]
    scale = mod_ref[mod_off + 1:mod_off + 2, :]
    gate = mod_ref[mod_off + 2:mod_off + 3, :]
    xb = _ada_rms(x, nw_ref[...], shift, scale).astype(BF16)
    acc = None
    for off, width in FFN_CHUNKS:
        g = _dot(xb, w1_ref[:, off:off + width])
        u = _dot(xb, w1_ref[:, D_FF + off:D_FF + off + width])
        a = (jax.nn.silu(g) * u).astype(BF16)
        part = _dot(a, w2_ref[off:off + width, :])
        acc = part if acc is None else acc + part
    return x + (0.5 * gate) * acc


def _ffn_kernel(x_ref, mod_ref, nw_ref, w1_ref, w2_ref, o_ref, *, mod_off):
    o_ref[...] = _ffn_half_step(x_ref[...], mod_ref, nw_ref, w1_ref, w2_ref, mod_off)


def _ffn(x, mod, mod_off, nw, w1, w2, lead, tm):
    b, l, d = x.shape
    mod_map = (lambda i, j: (i, 0, 0)) if mod.shape[0] == b else (lambda i, j: (0, 0, 0))
    return pl.pallas_call(
        functools.partial(_ffn_kernel, mod_off=mod_off),
        out_shape=jax.ShapeDtypeStruct(x.shape, x.dtype),
        grid=(b, l // tm),
        in_specs=[
            pl.BlockSpec((None, tm, d), lambda i, j: (i, j, 0)),
            pl.BlockSpec((None, N_MOD, d), mod_map),
            _resident((1, d), lambda i, j: (0, 0)),
            _stacked(w1, lead),
            _stacked(w2, lead),
        ],
        out_specs=pl.BlockSpec((None, tm, d), lambda i, j: (i, j, 0)),
        compiler_params=_params("parallel", "parallel"),
        name="ffn",
    )(x, mod, nw.reshape(1, d), w1, w2)


def _head_rms(t, w, ones_blk):
    ms = _seg_sum(t * t, ones_blk) * (1.0 / HEAD_DIM)
    return (t * lax.rsqrt(ms + RMS_EPS)) * w


def _rope(t, cos, sin_next, sin_prev):
    outs = []
    for s in range(RET_WIDTH // SLAB):
        ts = t[:, s * SLAB:(s + 1) * SLAB]
        nxt = pltpu.roll(ts, SLAB - ROPE_FREQS, axis=1)
        prv = pltpu.roll(ts, ROPE_FREQS, axis=1)
        outs.append(ts * cos + nxt * sin_next + prv * sin_prev)
    return jnp.concatenate(outs, axis=1)


def _mix_in_kernel(x_ref, mod_ref, nw_ref, w_ref, qn_ref, kn_ref, cos_ref, sn_ref, sp_ref,
                   sgw_ref, sgb_ref, lnw_ref, lnb_ref,
                   nq_ref, nk_ref, nv_ref, sg_ref, rq_ref, rk_ref, rv_ref, gf_ref, gb_ref):
    tm = x_ref.shape[0]
    x = x_ref[...]
    hb = _ada_rms(x, nw_ref[...], mod_ref[3:4, :], mod_ref[4:5, :]).astype(BF16)
    p = _dot(hb, w_ref[...])
    seg_ones = _seg_ones()
    scale = HEAD_DIM ** -0.5

    nq_ref[...] = (_head_rms(p[:, OFF_NQ:OFF_NQ + NA_WIDTH], qn_ref[...], seg_ones)
                   * (scale * LOG2E)).astype(BF16)
    nk_ref[...] = _head_rms(p[:, OFF_NK:OFF_NK + NA_WIDTH], kn_ref[...], seg_ones).astype(BF16)
    nv_ref[...] = p[:, OFF_NV:OFF_NV + NA_WIDTH].astype(BF16)

    u = jax.nn.gelu(p[:, OFF_SU:OFF_SU + SG_WIDTH])
    v = jax.nn.gelu(p[:, OFF_SV:OFF_SV + SG_WIDTH])
    mu = _seg_sum(v, seg_ones) * (1.0 / HEAD_DIM)
    dv = v - mu
    var = _seg_sum(dv * dv, seg_ones) * (1.0 / HEAD_DIM)
    vn = ((dv * lax.rsqrt(var + LN_EPS)) * lnw_ref[...] + lnb_ref[...]).astype(BF16)
    lane_grp = lax.broadcasted_iota(jnp.int32, (CHUNK, SG_WIDTH), 1) >> HEAD_SHIFT
    for j in range(tm // CHUNK):
        vch = vn[j * CHUNK:(j + 1) * CHUNK, :]
        stacked = jnp.concatenate(
            [jnp.where(lane_grp == g, vch, jnp.zeros_like(vch)) for g in range(SG_GROUPS)], axis=0)
        mixed = _dot(sgw_ref[...], stacked) + sgb_ref[...]
        sg_ref[j * CHUNK:(j + 1) * CHUNK, :] = (u[j * CHUNK:(j + 1) * CHUNK, :] * mixed).astype(BF16)

    cos, sn, sp = cos_ref[...], sn_ref[...], sp_ref[...]
    rq_ref[...] = _rope(p[:, OFF_RQ:OFF_RQ + RET_WIDTH], cos, sn, sp).astype(BF16)
    rk_ref[...] = (_rope(p[:, OFF_RK:OFF_RK + RET_WIDTH], cos, sn, sp) * scale).astype(BF16)
    rv_ref[...] = p[:, OFF_RV:OFF_RV + RET_WIDTH].astype(BF16)
    gf_ref[...] = jax.nn.silu(p[:, OFF_GF:OFF_GF + RET_WIDTH])
    gb_ref[...] = jax.nn.silu(p[:, OFF_GB:OFF_GB + RET_WIDTH])


def _mix_in(x, mod, nw, w_in, layer, qn, kn, rope_tabs, sgw, sgb_tab, lnw, lnb, tm):
    b, l, d = x.shape
    mod_map = (lambda i, j: (i, 0, 0)) if mod.shape[0] == b else (lambda i, j: (0, 0, 0))
    tok = lambda w: pl.BlockSpec((None, tm, w), lambda i, j: (i, j, 0))
    tab = pl.BlockSpec((tm, SLAB), lambda i, j: (j, 0))
    const = lambda shape: _resident(shape, lambda i, j: (0,) * len(shape))
    out_bf = lambda w: jax.ShapeDtypeStruct((b, l, w), BF16)
    out_f32 = lambda w: jax.ShapeDtypeStruct((b, l, w), F32)
    return pl.pallas_call(
        _mix_in_kernel,
        out_shape=(out_bf(NA_WIDTH), out_bf(NA_WIDTH), out_bf(NA_WIDTH), out_bf(SG_WIDTH),
                   out_bf(RET_WIDTH), out_bf(RET_WIDTH), out_bf(RET_WIDTH),
                   out_f32(RET_WIDTH), out_f32(RET_WIDTH)),
        grid=(b, l // tm),
        in_specs=[
            tok(d),
            pl.BlockSpec((None, N_MOD, d), mod_map),
            const((1, d)),
            _stacked(w_in, (layer,)),
            const((1, NA_WIDTH)), const((1, NA_WIDTH)),
            tab, tab, tab,
            const((CHUNK, SG_GROUPS * CHUNK)), const((CHUNK, SG_WIDTH)),
            const((1, SG_WIDTH)), const((1, SG_WIDTH)),
        ],
        out_specs=(tok(NA_WIDTH), tok(NA_WIDTH), tok(NA_WIDTH), tok(SG_WIDTH),
                   tok(RET_WIDTH), tok(RET_WIDTH), tok(RET_WIDTH), tok(RET_WIDTH), tok(RET_WIDTH)),
        compiler_params=_params("parallel", "parallel"),
        name="mix_in",
    )(x, mod, nw.reshape(1, d), w_in, qn, kn, *rope_tabs, sgw, sgb_tab, lnw, lnb)


def _own_lanes(shape, parity):
    even = lax.broadcasted_iota(jnp.int32, shape, 1) < HEAD_DIM
    return even if parity == 0 else jnp.logical_not(even)


def _scores_t(q, parity, keys, biases):
    qp = jnp.where(_own_lanes(q.shape, parity), q, jnp.zeros_like(q))
    scores = [_dot_nt(kk, qp) if bias is None else _dot_nt(kk, qp) + bias for kk, bias in zip(keys, biases)]
    mx = functools.reduce(jnp.maximum, [jnp.max(s, axis=0, keepdims=True) for s in scores])
    return scores, mx


def _weighted_values_t(scores, mx, parity, vals):
    num = None
    for s, vv in zip(scores, vals):
        vp = jnp.where(_own_lanes(vv.shape, parity), vv, jnp.ones_like(vv))
        part = _dot_tn(vp, jnp.exp2((s - mx).astype(BF16)))
        num = part if num is None else num + part
    return num


def _normalised_pair(num_even, num_odd):
    even_row = lax.broadcasted_iota(jnp.int32, num_even.shape, 0) < HEAD_DIM
    out_t = jnp.where(even_row, num_even / num_even[HEAD_DIM:HEAD_DIM + 1, :], num_odd / num_odd[0:1, :])
    return out_t.T


def _attend_all_heads(q_ref, o_ref, n_sub, operands):
    m = q_ref.shape[0] // n_sub
    units = [(t, s, parity) for t in range(n_sub) for s in range(NA_WIDTH // SLAB) for parity in range(2)]

    def logits(u):
        t, s, parity = units[u]
        keys, vals, biases = operands(t, s, parity)
        return _scores_t(q_ref[t * m:(t + 1) * m, s * SLAB:(s + 1) * SLAB], parity, keys, biases), vals

    pending = [logits(u) for u in range(NA_LOGITS_AHEAD)]
    nums = []
    for u, (t, s, parity) in enumerate(units):
        (scores, mx), vals = pending.pop(0)
        if u + NA_LOGITS_AHEAD < len(units):
            pending.append(logits(u + NA_LOGITS_AHEAD))
        nums.append(_weighted_values_t(scores, mx, parity, vals))
        if parity == 1:
            o_ref[t * m:(t + 1) * m, s * SLAB:(s + 1) * SLAB] = (
                _normalised_pair(nums[-2], nums[-1]).astype(BF16))


def _na_kernel(q_ref, k_ref, v_ref, kc_ref, vc_ref, bias_ref, o_ref, *, rows):
    n_blocks = rows // NA_BLOCK_ROWS

    def operands(t, s, parity):
        blk = pl.program_id(1) * NA_BLOCKS_PER_STEP + t
        band_row = jnp.clip(blk * NA_BLOCK_ROWS - NA_WIN_R // 2, 0, rows - NA_BAND_ROWS)
        band = pl.ds(pl.multiple_of(band_row * GRID_W, GRID_W), NA_BAND_ROWS * GRID_W)
        variant = jnp.where(blk == 0, 0, jnp.where(blk == n_blocks - 1, 2, 1))
        sl = slice(s * SLAB, (s + 1) * SLAB)
        return ([k_ref[band, sl], kc_ref[:, sl]], [v_ref[band, sl], vc_ref[:, sl]],
                [bias_ref[variant, 2 * s + parity], None])

    _attend_all_heads(q_ref, o_ref, NA_BLOCKS_PER_STEP, operands)


def _na_bias_table(rpb, rows):
    depth = rpb.shape[0]
    r, band = NA_BLOCK_ROWS, NA_BAND_ROWS
    n_blocks = rows // r
    n_off = 2 * NA_WIN_R - 1
    kcol = np.arange(GRID_W)[:, None]
    qc = np.arange(GRID_W)[None, :]
    c_start = np.clip(qc - NA_WIN_C // 2, 0, GRID_W - NA_WIN_C)
    col_ok = (kcol >= c_start) & (kcol < c_start + NA_WIN_C)
    col_sel = (kcol - qc + NA_WIN_C - 1)[:, :, None] == np.arange(2 * NA_WIN_C - 1)
    tiles = jnp.einsum('lhij,kqj->lihkq', rpb, jnp.asarray(col_sel, F32), precision=lax.Precision.HIGHEST)
    tiles = jnp.where(col_ok, tiles * LOG2E, NEG_INF)
    tile_of = np.full((3, band, r), n_off, np.int64)
    for vi, blk in enumerate((0, 1, n_blocks - 1)):
        r0 = blk * r
        b0 = int(np.clip(r0 - NA_WIN_R // 2, 0, rows - band))
        for ki in range(band):
            for qi in range(r):
                r_start = int(np.clip(r0 + qi - NA_WIN_R // 2, 0, rows - NA_WIN_R))
                if r_start <= b0 + ki < r_start + NA_WIN_R:
                    tile_of[vi, ki, qi] = b0 + ki - (r0 + qi) + NA_WIN_R - 1

    def assemble(tiles_ref, o_ref):
        for vi in range(3):
            for ki in range(band):
                for qi in range(r):
                    dst = (vi, slice(None), slice(ki * GRID_W, (ki + 1) * GRID_W),
                           slice(qi * GRID_W, (qi + 1) * GRID_W))
                    idx = int(tile_of[vi, ki, qi])
                    if idx == n_off:
                        o_ref[dst] = jnp.full((NA_HEADS, GRID_W, GRID_W), NEG_INF, F32)
                    else:
                        o_ref[dst] = tiles_ref[idx]

    return pl.pallas_call(
        assemble,
        out_shape=jax.ShapeDtypeStruct((depth, 3, NA_HEADS, band * GRID_W, r * GRID_W), F32),
        grid=(depth,),
        in_specs=[pl.BlockSpec((None, n_off, NA_HEADS, GRID_W, GRID_W), lambda l: (l, 0, 0, 0, 0))],
        out_specs=pl.BlockSpec((None, 3, NA_HEADS, band * GRID_W, r * GRID_W), lambda l: (l, 0, 0, 0, 0)),
        compiler_params=_params("parallel"),
        name="na_bias",
    )(tiles)


def _na(q, k, v, kc, vc, bias_tabs, layer, rows):
    b, s, w = q.shape
    lc = kc.shape[1]
    n_steps = rows // (NA_BLOCK_ROWS * NA_BLOCKS_PER_STEP)
    tq = NA_BLOCK_ROWS * NA_BLOCKS_PER_STEP * GRID_W
    return pl.pallas_call(
        functools.partial(_na_kernel, rows=rows),
        out_shape=jax.ShapeDtypeStruct((b, s, w), BF16),
        grid=(b, n_steps),
        in_specs=[
            pl.BlockSpec((None, tq, w), lambda i, j: (i, j, 0)),
            pl.BlockSpec((None, s, w), lambda i, j: (i, 0, 0)),
            pl.BlockSpec((None, s, w), lambda i, j: (i, 0, 0)),
            pl.BlockSpec((None, lc, w), lambda i, j: (i, 0, 0)),
            pl.BlockSpec((None, lc, w), lambda i, j: (i, 0, 0)),
            _stacked(bias_tabs, (layer,)),
        ],
        out_specs=pl.BlockSpec((None, tq, w), lambda i, j: (i, j, 0)),
        compiler_params=_params("parallel", "arbitrary"),
        name="na",
    )(q, k, v, kc, vc, bias_tabs)


def _ctx_attn_kernel(q_ref, k_ref, v_ref, o_ref):
    def operands(t, s, parity):
        sl = slice(s * SLAB, (s + 1) * SLAB)
        return [k_ref[:, sl]], [v_ref[:, sl]], [None]

    _attend_all_heads(q_ref, o_ref, 1, operands)


def _ctx_attn(q, k, v):
    b, lc, w = q.shape
    spec = pl.BlockSpec((None, lc, w), lambda i: (i, 0, 0))
    return pl.pallas_call(
        _ctx_attn_kernel,
        out_shape=jax.ShapeDtypeStruct((b, lc, w), BF16),
        grid=(b,),
        in_specs=[spec, spec, spec],
        out_specs=spec,
        compiler_params=_params("parallel"),
        name="ctx_attn",
    )(q, k, v)


def _ret_kernel(qf_ref, kf_ref, vf_ref, qb_ref, kb_ref, vb_ref, ks_ref, vs_ref, ws_ref, dm_ref, te_ref,
                fs_ref, cd_ref, of_ref, ob_ref, state_ref, *, n_chunks):
    n = pl.program_id(1)
    same_head = ((lax.broadcasted_iota(jnp.int32, (SLAB, SLAB), 0) >> HEAD_SHIFT)
                 == (lax.broadcasted_iota(jnp.int32, (SLAB, SLAB), 1) >> HEAD_SHIFT))
    even = lax.broadcasted_iota(jnp.int32, (CHUNK, SLAB), 1) < HEAD_DIM

    @pl.when(n == 0)
    def _seed():
        for d in range(2):
            for s in range(RET_WIDTH // SLAB):
                sl = slice(s * SLAB, (s + 1) * SLAB)
                kw = (ks_ref[:, sl].astype(F32) * ws_ref[d, :, sl]).astype(BF16)
                state_ref[d, s] = jnp.where(same_head, _dot_tn(kw, vs_ref[:, sl]), 0.0)

    streams = ((qf_ref, kf_ref, vf_ref, of_ref), (qb_ref, kb_ref, vb_ref, ob_ref))
    chunk = lambda c: slice(c * CHUNK, (c + 1) * CHUNK)
    units = []
    for s in range(RET_WIDTH // SLAB):
        sl = slice(s * SLAB, (s + 1) * SLAB)
        for d, (_, k_ref, v_ref, _) in enumerate(streams):
            order = [j if d == 0 else n_chunks - 1 - j for j in range(n_chunks)]
            kvs = []
            for c in order:
                kt = (k_ref[chunk(c), sl].astype(F32) * te_ref[d, :, sl]).astype(BF16)
                kvs.append(jnp.where(same_head, _dot_tn(kt, v_ref[chunk(c), sl]), 0.0))
            st = state_ref[d, s]
            for c, kv in zip(order, kvs):
                units.append((s, d, c, st))
                st = cd_ref[d, s] * st + kv
            state_ref[d, s] = st

    def decayed_scores(u):
        s, d, c, _ = units[u]
        sl = slice(s * SLAB, (s + 1) * SLAB)
        q = streams[d][0][chunk(c), sl]
        k = streams[d][1][chunk(c), sl]
        zero = jnp.zeros_like(k)
        k2 = jnp.concatenate([jnp.where(even, k, zero), jnp.where(even, zero, k)], axis=0)
        return (_dot_nt(q, k2) * dm_ref[d, s]).astype(BF16)

    pending = [decayed_scores(u) for u in range(min(RET_SCORES_AHEAD, len(units)))]
    for u, (s, d, c, st) in enumerate(units):
        sl = slice(s * SLAB, (s + 1) * SLAB)
        scores = pending.pop(0)
        if u + RET_SCORES_AHEAD < len(units):
            pending.append(decayed_scores(u + RET_SCORES_AHEAD))
        q_ref, _, v_ref, o_ref = streams[d]
        v = v_ref[chunk(c), sl]
        zero = jnp.zeros_like(v)
        v2 = jnp.concatenate([jnp.where(even, v, zero), jnp.where(even, zero, v)], axis=0)
        qf = (q_ref[chunk(c), sl].astype(F32) * fs_ref[d, :, sl]).astype(BF16)
        o_ref[chunk(c), sl] = _dot(scores, v2) + _dot(qf, st.astype(BF16))


def _ret_tables(log_g, seed_len, seeded):
    pos = jnp.arange(CHUNK, dtype=F32)
    diff = pos[:, None] - pos[None, :]
    lg = log_g[:, :, None, None]
    dm_f = jnp.where(diff >= 0, jnp.exp(jnp.maximum(diff, 0.0)[None, None] * lg), 0.0)[0]
    dm_b = jnp.where(diff <= 0, jnp.exp(jnp.maximum(-diff, 0.0)[None, None] * lg), 0.0)[1]
    dm = jnp.stack([dm_f, dm_b])
    dm = dm.reshape(2, RET_HEADS // 2, 2, CHUNK, CHUNK).transpose(0, 1, 3, 2, 4)
    dm = dm.reshape(2, RET_HEADS // 2, CHUNK, 2 * CHUNK)
    lane = lambda t: jnp.repeat(t, HEAD_DIM, axis=-1)
    te = jnp.stack([jnp.exp((CHUNK - 1 - pos)[:, None] * log_g[0][None, :]),
                    jnp.exp(pos[:, None] * log_g[1][None, :])])
    fs = jnp.stack([jnp.exp((pos + 1.0)[:, None] * log_g[0][None, :]),
                    jnp.exp((CHUNK - pos)[:, None] * log_g[1][None, :])])
    cd = jnp.broadcast_to(lane(jnp.exp(CHUNK * log_g)).reshape(2, RET_HEADS // 2, SLAB, 1),
                          (2, RET_HEADS // 2, SLAB, SLAB))
    sp = jnp.arange(seed_len, dtype=F32)
    ws = jnp.stack([jnp.exp((seed_len - 1 - sp)[:, None] * log_g[0][None, :]),
                    jnp.exp(sp[:, None] * log_g[1][None, :])])
    if not seeded:
        ws = jnp.zeros_like(ws)
    return dm, lane(te), lane(fs), cd, lane(ws)


def _retention(q, k, v, k_seed, v_seed, tables, tb):
    b, l, w = q.shape
    ls = k_seed.shape[1]
    nb = l // tb
    dm, te, fs, cd, ws = tables
    fwd = pl.BlockSpec((None, tb, w), lambda i, n: (i, n, 0))
    bwd = pl.BlockSpec((None, tb, w), lambda i, n: (i, nb - 1 - n, 0))
    seed = pl.BlockSpec((None, ls, w), lambda i, n: (i, 0, 0))
    const = lambda shape: _resident(shape, lambda i, n: (0,) * len(shape))
    out = jax.ShapeDtypeStruct((b, l, w), F32)
    return pl.pallas_call(
        functools.partial(_ret_kernel, n_chunks=tb // CHUNK),
        out_shape=(out, out),
        grid=(b, nb),
        in_specs=[
            fwd, fwd, fwd, bwd, bwd, bwd, seed, seed,
            const(ws.shape), const(dm.shape), const(te.shape), const(fs.shape), const(cd.shape),
        ],
        out_specs=(fwd, bwd),
        scratch_shapes=[pltpu.VMEM((2, RET_HEADS // 2, SLAB, SLAB), F32)],
        compiler_params=_params("parallel", "arbitrary"),
        name="retention",
    )(q, k, v, q, k, v, k_seed, v_seed, ws, dm, te, fs, cd)


def _head_group_norm(o, ones_blk):
    mu = _seg_sum(o, ones_blk) * (1.0 / HEAD_DIM)
    dlt = o - mu
    var = _seg_sum(dlt * dlt, ones_blk) * (1.0 / HEAD_DIM)
    return dlt * lax.rsqrt(var + LN_EPS)


def _mix_out_ffn_kernel(x_ref, mod_ref, na_ref, sg_ref, of_ref, ob_ref, gf_ref, gb_ref, gnw_ref,
                        w_ref, nw_ref, w1_ref, w2_ref, o_ref):
    seg_ones = _seg_ones()
    y = (_head_group_norm(of_ref[...], seg_ones) * gf_ref[...]
         + _head_group_norm(ob_ref[...], seg_ones) * gb_ref[...])
    ret = (y * gnw_ref[...]).astype(BF16)
    o_sg = NA_WIDTH + SG_WIDTH
    out = (_dot(na_ref[...], w_ref[0:NA_WIDTH, :])
           + _dot(sg_ref[...], w_ref[NA_WIDTH:o_sg, :])
           + _dot(ret, w_ref[o_sg:, :]))
    x = x_ref[...] + mod_ref[5:6, :] * out
    o_ref[...] = _ffn_half_step(x, mod_ref, nw_ref, w1_ref, w2_ref, 6)


def _mix_out_ffn(x, mod, na, sg, o_fwd, o_bwd, gf, gb, gnw, w_out, nw, w1, w2, layer, tm):
    b, l, d = x.shape
    mod_map = (lambda i, j: (i, 0, 0)) if mod.shape[0] == b else (lambda i, j: (0, 0, 0))
    tok = lambda w: pl.BlockSpec((None, tm, w), lambda i, j: (i, j, 0))
    const = lambda shape: _resident(shape, lambda i, j: (0,) * len(shape))
    return pl.pallas_call(
        _mix_out_ffn_kernel,
        out_shape=jax.ShapeDtypeStruct(x.shape, x.dtype),
        grid=(b, l // tm),
        in_specs=[
            tok(d),
            pl.BlockSpec((None, N_MOD, d), mod_map),
            tok(NA_WIDTH), tok(SG_WIDTH), tok(RET_WIDTH), tok(RET_WIDTH), tok(RET_WIDTH), tok(RET_WIDTH),
            const((1, RET_WIDTH)), _stacked(w_out, (layer,)),
            const((1, d)), _stacked(w1, (layer, 1)), _stacked(w2, (layer, 1)),
        ],
        out_specs=tok(d),
        compiler_params=_params("parallel", "parallel"),
        name="mix_out_ffn",
    )(x, mod, na, sg, o_fwd, o_bwd, gf, gb, gnw, w_out, nw.reshape(1, d), w1, w2)


def _rope_tables(n_lat):
    t = jnp.arange(n_lat)[:, None]
    lane = np.arange(SLAB)
    inv = ROPE_BASE ** (-jnp.asarray(lane % ROPE_FREQS, F32) / ROPE_FREQS)
    pos = jnp.where(lane % HEAD_DIM < HEAD_DIM // 2, t // GRID_W, t % GRID_W).astype(F32)
    ang = pos * inv[None, :]
    first = (lane // ROPE_FREQS) % 2 == 0
    sin = jnp.sin(ang)
    return jnp.cos(ang), jnp.where(first, -sin, 0.0), jnp.where(first, 0.0, sin)


def kernel(x, c, ctx, c_ctx, ada_w, ada_b, norm_w, ffn_w1, ffn_w2, mix_w_in, mix_w_out, na_q_norm,
           na_k_norm, na_rpb, sg_w, sg_b, sg_ln_w, sg_ln_b, ret_decay_logit, ret_gn_w):
    b, n_lat, d = x.shape
    lc = ctx.shape[1]
    depth = ada_w.shape[0]
    rows = n_lat // GRID_W
    assert d == D_MODEL and n_lat % (GRID_W * NA_BLOCK_ROWS * NA_BLOCKS_PER_STEP) == 0
    assert rows >= 2 * NA_BAND_ROWS - NA_WIN_R
    assert lc % CHUNK == 0
    tm_x = min(TILE_TOKENS, n_lat)
    tm_c = min(CTX_TILE_TOKENS, lc)
    tb_x = min(RET_BLOCK_TOKENS, n_lat)
    tw_x = min(WIDE_TILE_TOKENS, n_lat)
    assert n_lat % tw_x == 0 and tw_x % CHUNK == 0
    assert n_lat % tb_x == 0
    assert n_lat % tm_x == 0 and lc % tm_c == 0 and tm_x % CHUNK == 0 and tm_c % CHUNK == 0

    assert b + 1 <= COND_ROWS
    cc = jnp.zeros((COND_ROWS, d), F32).at[:b].set(c).at[b].set(c_ctx)
    mod = _adaln(cc, ada_w, ada_b)

    rope_x = _rope_tables(n_lat)
    rope_c = (jnp.ones((lc, SLAB), F32), jnp.zeros((lc, SLAB), F32), jnp.zeros((lc, SLAB), F32))
    w1 = ffn_w1.astype(BF16)
    w2 = ffn_w2.astype(BF16)
    w_in = mix_w_in.astype(BF16)
    w_out = mix_w_out.astype(BF16)
    sgw = sg_w.astype(BF16).transpose(0, 2, 1, 3).reshape(depth, CHUNK, SG_GROUPS * CHUNK)
    log_g = jax.nn.log_sigmoid(ret_decay_logit.astype(F32))
    bias_tabs = _na_bias_table(na_rpb, rows)

    for layer in range(depth):
        last = layer == depth - 1
        mod_x = mod[layer, :b].reshape(b, N_MOD, d)
        mod_c = mod[layer, b:b + 1].reshape(1, N_MOD, d)
        nw = norm_w[layer]
        x = _ffn(x, mod_x, 0, nw[0], w1, w2, (layer, 0), tw_x)
        ctx = _ffn(ctx, mod_c, 0, nw[0], w1, w2, (layer, 0), tm_c)

        qn = jnp.tile(na_q_norm[layer], NA_HEADS).reshape(1, NA_WIDTH)
        kn = jnp.tile(na_k_norm[layer], NA_HEADS).reshape(1, NA_WIDTH)
        sgb_tab = jnp.repeat(sg_b[layer].T, HEAD_DIM, axis=1)
        lnw = sg_ln_w[layer].reshape(1, SG_WIDTH)
        lnb = sg_ln_b[layer].reshape(1, SG_WIDTH)
        gnw = ret_gn_w[layer].reshape(1, RET_WIDTH)
        px = _mix_in(x, mod_x, nw[1], w_in, layer, qn, kn, rope_x, sgw[layer], sgb_tab, lnw, lnb, tw_x)
        pc = _mix_in(ctx, mod_c, nw[1], w_in, layer, qn, kn, rope_c, sgw[layer], sgb_tab, lnw, lnb, tm_c)
        nq_x, nk_x, nv_x, sg_x, rq_x, rk_x, rv_x, gf_x, gb_x = px
        nq_c, nk_c, nv_c, sg_c, rq_c, rk_c, rv_c, gf_c, gb_c = pc

        na_x = _na(nq_x, nk_x, nv_x, nk_c, nv_c, bias_tabs, layer, rows)
        of_x, ob_x = _retention(rq_x, rk_x, rv_x, rk_c, rv_c, _ret_tables(log_g[layer], lc, True), tb_x)
        x = _mix_out_ffn(x, mod_x, na_x, sg_x, of_x, ob_x, gf_x, gb_x, gnw, w_out, nw[2], w1, w2,
                         layer, tm_x)
        if not last:
            na_c = _ctx_attn(nq_c, nk_c, nv_c)
            of_c, ob_c = _retention(rq_c, rk_c, rv_c, rk_c, rv_c, _ret_tables(log_g[layer], lc, False),
                                    tm_c)
            ctx = _mix_out_ffn(ctx, mod_c, na_c, sg_c, of_c, ob_c, gf_c, gb_c, gnw, w_out, nw[2], w1, w2,
                               layer, tm_c)
    return x
```
